```python
import jax
import jax.numpy as jnp
from jax import lax
import numpy as np

D_MODEL = 2048
BATCH = 4
SEQ = 2048
DEPTH = 2
DEC_BATCH = 8
DEC_SEQ = 1
PAST_LEN = 16384
PAGE_SIZE = 128

N_A_LAYERS = DEPTH // 2
N_B_LAYERS = DEPTH - N_A_LAYERS
N_DENSE = (DEPTH + 1) // 2
N_MOE = DEPTH // 2

MLSTM_HEADS = 4
MLSTM_DK = D_MODEL // (2 * MLSTM_HEADS)
MLSTM_DV = D_MODEL // MLSTM_HEADS
MLSTM_CHUNK = 64
GATE_SOFTCAP = 15.0
MLSTM_PROJ = 2 * MLSTM_HEADS * MLSTM_DK + 2 * MLSTM_HEADS * MLSTM_DV + 2 * MLSTM_HEADS

ATTN_HEAD_DIM = 128
ATTN_HEADS = D_MODEL // ATTN_HEAD_DIM
DILATED_GROUPS = ((128, 1), (512, 4), (2048, 16))
N_GROUPS = len(DILATED_GROUPS)
ATTN_WIDTH = ATTN_HEADS * ATTN_HEAD_DIM

D_FF = 5632
N_EXPERTS = 8
TOP_K = 2
D_FF_EXPERT = 2816
EPS = 1e-6

kernel_name = 'yoco_mlstm_dilated_swa_moe_step'

F32 = jnp.float32


def rmsnorm(x, g):
    xf = x.astype(F32)
    y = xf * lax.rsqrt(jnp.mean(xf * xf, axis=-1, keepdims=True) + EPS)
    return (y * g.astype(F32)).astype(x.dtype)


def softcap(x):
    return GATE_SOFTCAP * jnp.tanh(x / GATE_SOFTCAP)


def swiglu(x, w_gate, w_up, w_down):
    return (jax.nn.silu(x @ w_gate) * (x @ w_up)) @ w_down


def moe_swiglu(x, w_router, b_router, w_gate, w_up, w_down):
    logits = (x @ w_router).astype(F32) + b_router.astype(F32)
    probs = jax.nn.softmax(logits, axis=-1)
    top_p, top_i = lax.top_k(probs, TOP_K)
    top_p = top_p / jnp.sum(top_p, axis=-1, keepdims=True)
    gates = jnp.sum(jax.nn.one_hot(top_i, N_EXPERTS, dtype=F32) * top_p[..., None], axis=-2)
    y = jnp.zeros(x.shape, F32)
    for e in range(N_EXPERTS):
        y = y + gates[..., e:e + 1] * swiglu(x, w_gate[e], w_up[e], w_down[e]).astype(F32)
    return y.astype(x.dtype)


def mlstm_chunk_step(carry, inp):
    C, n, m = carry
    q, k, v, ig, lf = inp
    L = q.shape[2]
    b = jnp.cumsum(lf, axis=-1)
    causal = jnp.tril(jnp.ones((L, L), bool))
    dmat = jnp.where(causal, b[..., :, None] - b[..., None, :] + ig[..., None, :], -jnp.inf)
    inter = b + m[..., None]
    m_t = jnp.maximum(inter, jnp.max(dmat, axis=-1))
    w_intra = jnp.exp(dmat - m_t[..., None])
    w_inter = jnp.exp(inter - m_t)
    s = jnp.einsum('bhtd,bhsd->bhts', q, k) * w_intra
    num = w_inter[..., None] * jnp.einsum('bhtd,bhde->bhte', q, C) + jnp.einsum('bhts,bhse->bhte', s, v)
    den = w_inter * jnp.einsum('bhtd,bhd->bht', q, n) + jnp.sum(s, axis=-1)
    h = num / jnp.maximum(jnp.abs(den), jnp.exp(-m_t))[..., None]
    b_last = b[..., -1]
    g = b_last[..., None] - b + ig
    m_new = jnp.maximum(b_last + m, jnp.max(g, axis=-1))
    decay = jnp.exp(b_last + m - m_new)
    w_k = jnp.exp(g - m_new[..., None])
    kw = k * w_k[..., None]
    C_new = decay[..., None, None] * C + jnp.einsum('bhsd,bhse->bhde', kw, v)
    n_new = decay[..., None] * n + jnp.sum(kw, axis=2)
    return (C_new, n_new, m_new), h


def mlstm_mixer(xn, w_in, b_gates, out_norm, w_out, C0, n0, m0):
    Bn, S, _ = xn.shape
    hk = MLSTM_HEADS * MLSTM_DK
    hv = MLSTM_HEADS * MLSTM_DV
    q, k, v, o, gates = jnp.split(xn @ w_in, [hk, 2 * hk, 2 * hk + hv, 2 * hk + 2 * hv], axis=-1)

    def heads(t, d):
        return t.reshape(Bn, S, MLSTM_HEADS, d).transpose(0, 2, 1, 3).astype(F32)

    q = heads(q, MLSTM_DK)
    k = heads(k, MLSTM_DK) * (MLSTM_DK ** -0.5)
    v = heads(v, MLSTM_DV)
    gates = softcap(gates.astype(F32) + b_gates.astype(F32)).transpose(0, 2, 1)
    ig = gates[:, :MLSTM_HEADS]
    lf = jax.nn.log_sigmoid(gates[:, MLSTM_HEADS:])
    L = MLSTM_CHUNK if S % MLSTM_CHUNK == 0 else S
    nc = S // L

    def chunks(t):
        return jnp.moveaxis(t.reshape(t.shape[:2] + (nc, L) + t.shape[3:]), 2, 0)

    (C, n, m), h = lax.scan(mlstm_chunk_step, (C0.astype(F32), n0.astype(F32), m0.astype(F32)),
                            (chunks(q), chunks(k), chunks(v), chunks(ig), chunks(lf)))
    h = jnp.moveaxis(h, 0, 2).reshape(Bn, MLSTM_HEADS, S, MLSTM_DV).transpose(0, 2, 1, 3)
    h = h * lax.rsqrt(jnp.mean(h * h, axis=-1, keepdims=True) + EPS)
    h = h.reshape(Bn, S, hv) * out_norm.astype(F32) * jax.nn.sigmoid(o.astype(F32))
    return h.astype(xn.dtype) @ w_out, C, n, m


def masked_softmax(s, valid):
    s = jnp.where(valid, s, -jnp.inf)
    mx = jnp.max(s, axis=-1, keepdims=True)
    p = jnp.exp(s - mx)
    l = jnp.sum(p, axis=-1, keepdims=True)
    return p / l, (mx + jnp.log(l))[..., 0]


def band_attention(q, k, v, steps):
    N, T, H, D = q.shape
    bq = steps
    nb = -(-T // bq)
    tail = nb * bq - T
    qb = jnp.pad(q, ((0, 0), (0, tail), (0, 0), (0, 0))).reshape(N, nb, bq, H, D)

    def blocks(t):
        tp = jnp.pad(t, ((0, 0), (bq, tail), (0, 0), (0, 0)))
        prev = tp[:, :nb * bq].reshape(N, nb, bq, H, D)
        cur = tp[:, bq:].reshape(N, nb, bq, H, D)
        return jnp.concatenate([prev, cur], axis=2)

    kb, vb = blocks(k), blocks(v)
    s = jnp.einsum('nbqhd,nbkhd->nbhqk', qb, kb, preferred_element_type=F32) * (D ** -0.5)
    i = jnp.arange(bq)[:, None]
    j = jnp.arange(2 * bq)[None, :]
    dist = bq + i - j
    kpos = (jnp.arange(nb) * bq)[:, None, None] - bq + j[None]
    valid = (dist >= 0) & (dist <= steps) & (kpos >= 0)
    p, lse = masked_softmax(s, valid[None, :, None])
    o = jnp.einsum('nbhqk,nbkhd->nbqhd', p.astype(vb.dtype), vb).reshape(N, nb * bq, H, D)[:, :T]
    lse = lse.transpose(0, 1, 3, 2).reshape(N, nb * bq, H)[:, :T]
    return o, lse


def dilated_prompt(q, k, v, dilation, steps):
    B, S, H, D = q.shape
    sub = S // dilation

    def fold(t):
        return t.reshape(B, sub, dilation, H, D).transpose(0, 2, 1, 3, 4).reshape(B * dilation, sub, H, D)

    o, lse = band_attention(fold(q), fold(k), fold(v), steps)
    o = o.reshape(B, dilation, sub, H, D).transpose(0, 2, 1, 3, 4).reshape(B, S, H, D)
    lse = lse.reshape(B, dilation, sub, H).transpose(0, 2, 1, 3).reshape(B, S, H)
    return o, lse


def dilated_decode(q, k_new, v_new, kv_buf, dilation, steps):
    Lb = kv_buf.shape[1]
    T = q.shape[1]
    k_all = jnp.concatenate([kv_buf[:, :, 0].astype(k_new.dtype), k_new], axis=1)
    v_all = jnp.concatenate([kv_buf[:, :, 1].astype(v_new.dtype), v_new], axis=1)
    idx = Lb + jnp.arange(T)[:, None] - dilation * jnp.arange(steps + 1)[None, :]
    valid = idx >= 0
    idx = jnp.maximum(idx, 0)
    kg = k_all[:, idx]
    vg = v_all[:, idx]
    s = jnp.einsum('bthd,btjhd->bthj', q, kg, preferred_element_type=F32) * (q.shape[-1] ** -0.5)
    p, lse = masked_softmax(s, valid[None, :, None, :])
    o = jnp.einsum('bthj,btjhd->bthd', p.astype(vg.dtype), vg)
    return o, lse


def dilated_mixer(xn, w_q, kv, w_out, kv_bufs):
    Bn, S, _ = xn.shape
    q = (xn @ w_q).reshape(Bn, S, N_GROUPS, ATTN_HEADS, ATTN_HEAD_DIM)
    outs, lses = [], []
    for g, (window, dilation) in enumerate(DILATED_GROUPS):
        steps = window // dilation
        if kv_bufs is None:
            o, lse = dilated_prompt(q[:, :, g], kv[:, :, g, 0], kv[:, :, g, 1], dilation, steps)
        else:
            o, lse = dilated_decode(q[:, :, g], kv[:, :, g, 0], kv[:, :, g, 1], kv_bufs[g], dilation, steps)
        outs.append(o.astype(F32))
        lses.append(lse)
    w = jax.nn.softmax(jnp.stack(lses), axis=0)[..., None]
    o = jnp.sum(w * jnp.stack(outs), axis=0).reshape(Bn, S, ATTN_WIDTH)
    return o.astype(xn.dtype) @ w_out


def trunk(x, C0, n0, m0, kv_bufs, norm_mix, norm_ffn, mlstm_w_in, mlstm_b_gates, mlstm_out_norm,
          mlstm_w_out, kv_norm, w_kv, attn_w_q, attn_w_out, ffn_w_gate, ffn_w_up, ffn_w_down,
          moe_w_router, moe_b_router, moe_w_gate, moe_w_up, moe_w_down, final_norm):
    Bn, S, _ = x.shape
    h = x
    Cs, ns, ms = [], [], []
    kv = None
    kv_rows = []
    for layer in range(DEPTH):
        xn = rmsnorm(h, norm_mix[layer])
        if layer < N_A_LAYERS:
            y, C, n, m = mlstm_mixer(xn, mlstm_w_in[layer], mlstm_b_gates[layer], mlstm_out_norm[layer],
                                     mlstm_w_out[layer], C0[layer], n0[layer], m0[layer])
            Cs.append(C)
            ns.append(n)
            ms.append(m)
        else:
            if kv is None:
                kv = (rmsnorm(h, kv_norm) @ w_kv).reshape(Bn, S, N_GROUPS, 2, ATTN_HEADS, ATTN_HEAD_DIM)
                for g, (window, _) in enumerate(DILATED_GROUPS):
                    keep = S if kv_bufs is not None else min(window, S)
                    kv_rows.append(kv[:, S - keep:, g])
            bi = layer - N_A_LAYERS
            y = dilated_mixer(xn, attn_w_q[bi], kv, attn_w_out[bi], kv_bufs)
        h = h + y
        xn = rmsnorm(h, norm_ffn[layer])
        j = layer // 2
        if layer % 2 == 0:
            f = swiglu(xn, ffn_w_gate[j], ffn_w_up[j], ffn_w_down[j])
        else:
            f = moe_swiglu(xn, moe_w_router[j], moe_b_router[j], moe_w_gate[j], moe_w_up[j], moe_w_down[j])
        h = h + f
    return (rmsnorm(h, final_norm), jnp.stack(Cs), jnp.stack(ns), jnp.stack(ms),
            kv_rows[0], kv_rows[1], kv_rows[2])


def setup_inputs(seed: int = 0) -> dict:
    key = jax.random.key(seed)
    keys = list(jax.random.split(key, 40))

    def nrm(i, shape, scale=1.0):
        return jax.random.normal(keys[i], shape, jnp.float32) * scale

    D = D_MODEL
    H = MLSTM_HEADS
    hv = MLSTM_HEADS * MLSTM_DV
    kv_shape = lambda w: (DEC_BATCH, min(w, PAST_LEN), 2, ATTN_HEADS, ATTN_HEAD_DIM)
    gate_bias = jnp.concatenate([nrm(12, (N_A_LAYERS, H), 0.1),
                                 3.0 + nrm(13, (N_A_LAYERS, H), 0.5)], axis=-1)
    return {
        'x_prompt': nrm(0, (BATCH, SEQ, D)),
        'x_sample': nrm(1, (DEC_BATCH, DEC_SEQ, D)),
        'state_mlstm_C': nrm(2, (N_A_LAYERS, DEC_BATCH, H, MLSTM_DK, MLSTM_DV), 0.05),
        'state_mlstm_n': nrm(3, (N_A_LAYERS, DEC_BATCH, H, MLSTM_DK), 0.05),
        'state_mlstm_m': nrm(4, (N_A_LAYERS, DEC_BATCH, H), 0.5),
        'cache_kv_w128': nrm(5, kv_shape(DILATED_GROUPS[0][0])),
        'cache_kv_w512': nrm(6, kv_shape(DILATED_GROUPS[1][0])),
        'cache_kv_w2048': nrm(7, kv_shape(DILATED_GROUPS[2][0])),
        'norm_mix': 1.0 + nrm(8, (DEPTH, D), 0.02),
        'norm_ffn': 1.0 + nrm(9, (DEPTH, D), 0.02),
        'mlstm_w_in': nrm(10, (N_A_LAYERS, D, MLSTM_PROJ), D ** -0.5),
        'mlstm_b_gates': gate_bias,
        'mlstm_out_norm': 1.0 + nrm(14, (N_A_LAYERS, hv), 0.02),
        'mlstm_w_out': nrm(15, (N_A_LAYERS, hv, D), hv ** -0.5),
        'kv_norm': 1.0 + nrm(16, (D,), 0.02),
        'w_kv': nrm(17, (D, N_GROUPS * 2 * ATTN_WIDTH), D ** -0.5),
        'attn_w_q': nrm(18, (N_B_LAYERS, D, N_GROUPS * ATTN_WIDTH), D ** -0.5),
        'attn_w_out': nrm(19, (N_B_LAYERS, ATTN_WIDTH, D), ATTN_WIDTH ** -0.5),
        'ffn_w_gate': nrm(20, (N_DENSE, D, D_FF), D ** -0.5),
        'ffn_w_up': nrm(21, (N_DENSE, D, D_FF), D ** -0.5),
        'ffn_w_down': nrm(22, (N_DENSE, D_FF, D), D_FF ** -0.5),
        'moe_w_router': nrm(23, (N_MOE, D, N_EXPERTS), D ** -0.5),
        'moe_b_router': nrm(24, (N_MOE, N_EXPERTS), 0.01),
        'moe_w_gate': nrm(25, (N_MOE, N_EXPERTS, D, D_FF_EXPERT), D ** -0.5),
        'moe_w_up': nrm(26, (N_MOE, N_EXPERTS, D, D_FF_EXPERT), D ** -0.5),
        'moe_w_down': nrm(27, (N_MOE, N_EXPERTS, D_FF_EXPERT, D), D_FF_EXPERT ** -0.5),
        'final_norm': 1.0 + nrm(28, (D,), 0.02),
    }


def reference(x_prompt, x_sample, state_mlstm_C, state_mlstm_n, state_mlstm_m, cache_kv_w128,
              cache_kv_w512, cache_kv_w2048, norm_mix, norm_ffn, mlstm_w_in, mlstm_b_gates,
              mlstm_out_norm, mlstm_w_out, kv_norm, w_kv, attn_w_q, attn_w_out, ffn_w_gate, ffn_w_up,
              ffn_w_down, moe_w_router, moe_b_router, moe_w_gate, moe_w_up, moe_w_down, final_norm):
    weights = (norm_mix, norm_ffn, mlstm_w_in, mlstm_b_gates, mlstm_out_norm, mlstm_w_out, kv_norm,
               w_kv, attn_w_q, attn_w_out, ffn_w_gate, ffn_w_up, ffn_w_down, moe_w_router,
               moe_b_router, moe_w_gate, moe_w_up, moe_w_down, final_norm)
    bp = x_prompt.shape[0]
    zero_C = jnp.zeros((N_A_LAYERS, bp, MLSTM_HEADS, MLSTM_DK, MLSTM_DV), F32)
    zero_n = jnp.zeros((N_A_LAYERS, bp, MLSTM_HEADS, MLSTM_DK), F32)
    zero_m = jnp.zeros((N_A_LAYERS, bp, MLSTM_HEADS), F32)
    y_prompt, p_C, p_n, p_m, p_kv128, p_kv512, p_kv2048 = trunk(
        x_prompt, zero_C, zero_n, zero_m, None, *weights)
    y_sample, s_C, s_n, s_m, s_kv128, s_kv512, s_kv2048 = trunk(
        x_sample, state_mlstm_C, state_mlstm_n, state_mlstm_m,
        (cache_kv_w128, cache_kv_w512, cache_kv_w2048), *weights)
    return (y_prompt, y_sample, p_C, p_n, p_m, p_kv128, p_kv512, p_kv2048,
            s_C, s_n, s_m, s_kv128, s_kv512, s_kv2048)
```

```python
import functools

import jax
import jax.numpy as jnp
from jax import lax
from jax.experimental import pallas as pl
from jax.experimental.pallas import tpu as pltpu

F32 = jnp.float32
BF16 = jnp.bfloat16
I32 = jnp.int32

EPS = 1e-6
GATE_SOFTCAP = 15.0
MLSTM_HEADS = 4
ATTN_HEAD_DIM = 128
DILATED_GROUPS = ((128, 1), (512, 4), (2048, 16))
N_EXPERTS = 8
TOP_K = 2
MLSTM_CHUNK = 256
ATTN_BLOCK = 128
SAMPLE_ROWS = 16

VMEM_LIMIT = 56 * 1024 * 1024


def _params(*sem):
    return pltpu.CompilerParams(dimension_semantics=sem, vmem_limit_bytes=VMEM_LIMIT)


def _dot(a, b):
    return jnp.dot(a, b, preferred_element_type=F32)


def _dot_nt(a, b):
    return lax.dot_general(a, b, (((1,), (1,)), ((), ())), preferred_element_type=F32)


def _dot_tn(a, b):
    return lax.dot_general(a, b, (((0,), (0,)), ((), ())), preferred_element_type=F32)


def _split(x):
    hi = x.astype(BF16)
    lo = (x - hi.astype(F32)).astype(BF16)
    return hi, lo


def _dot_nt_3pass(x, w):
    xh, xl = _split(x)
    wh, wl = _split(w)
    return _dot_nt(xh, wh) + (_dot_nt(xl, wh) + _dot_nt(xh, wl))


def _sigmoid(x):
    return 1.0 / (1.0 + jnp.exp(-x))


def _rms(x):
    return x * lax.rsqrt(jnp.mean(x * x, axis=-1, keepdims=True) + EPS)


def _norm_kernel(x_ref, g_ref, *o_refs):
    y = _rms(x_ref[...])
    for i, o_ref in enumerate(o_refs):
        o_ref[...] = (y * g_ref[i:i + 1, :]).astype(o_ref.dtype)


def _norm(x, gains, dtypes):
    T, D = x.shape
    tm = min(T, 512)
    return pl.pallas_call(
        _norm_kernel,
        out_shape=[jax.ShapeDtypeStruct((T, D), dt) for dt in dtypes],
        grid=(T // tm,),
        in_specs=[pl.BlockSpec((tm, D), lambda i: (i, 0)),
                  pl.BlockSpec(gains.shape, lambda i: (0, 0))],
        out_specs=[pl.BlockSpec((tm, D), lambda i: (i, 0)) for _ in dtypes],
        compiler_params=_params("parallel"),
        name="rmsnorm",
    )(x, gains)


def _mm_kernel(a_ref, w_ref, *rest, has_res):
    if has_res:
        r_ref, o_ref, wb_ref = rest
    else:
        o_ref, wb_ref = rest

    @pl.when(pl.program_id(1) == 0)
    def _():
        wb_ref[...] = w_ref[...].astype(BF16)

    acc = _dot(a_ref[...], wb_ref[...])
    if has_res:
        acc = r_ref[...] + acc
    o_ref[...] = acc.astype(o_ref.dtype)


def _matmul(a, w, n_cols, out_dtype, res=None, tn=1024):
    M, K = a.shape
    tm = min(M, 512)
    tn = min(tn, n_cols)
    in_specs = [pl.BlockSpec((tm, K), lambda j, i: (i, 0)),
                pl.BlockSpec((K, tn), lambda j, i: (0, j))]
    args = [a, w]
    if res is not None:
        in_specs.append(pl.BlockSpec((tm, tn), lambda j, i: (i, j)))
        args.append(res)
    return pl.pallas_call(
        functools.partial(_mm_kernel, has_res=res is not None),
        out_shape=jax.ShapeDtypeStruct((M, n_cols), out_dtype),
        grid=(n_cols // tn, M // tm),
        in_specs=in_specs,
        out_specs=pl.BlockSpec((tm, tn), lambda j, i: (i, j)),
        scratch_shapes=[pltpu.VMEM((K, tn), BF16)],
        compiler_params=_params("arbitrary", "arbitrary"),
        name="matmul",
    )(*args)


def _glu_kernel(a_ref, wg_ref, wu_ref, o_ref, wgb_ref, wub_ref):
    @pl.when(pl.program_id(1) == 0)
    def _():
        wgb_ref[...] = wg_ref[...].astype(BF16)
        wub_ref[...] = wu_ref[...].astype(BF16)

    a = a_ref[...]
    g = _dot(a, wgb_ref[...])
    u = _dot(a, wub_ref[...])
    o_ref[...] = (g * _sigmoid(g) * u).astype(o_ref.dtype)


def _glu(a, wg, wu, tf=512):
    M, K = a.shape
    F = wg.shape[1]
    tm = min(M, 512)
    return pl.pallas_call(
        _glu_kernel,
        out_shape=jax.ShapeDtypeStruct((M, F), BF16),
        grid=(F // tf, M // tm),
        in_specs=[pl.BlockSpec((tm, K), lambda j, i: (i, 0)),
                  pl.BlockSpec((K, tf), lambda j, i: (0, j)),
                  pl.BlockSpec((K, tf), lambda j, i: (0, j))],
        out_specs=pl.BlockSpec((tm, tf), lambda j, i: (i, j)),
        scratch_shapes=[pltpu.VMEM((K, tf), BF16), pltpu.VMEM((K, tf), BF16)],
        compiler_params=_params("arbitrary", "arbitrary"),
        name="swiglu_up",
    )(a, wg, wu)


def _gates_kernel(x_ref, g_ref, wt_ref, bc_ref, br_ref, gc_ref, gr_ref):
    H = MLSTM_HEADS
    xn = _rms(x_ref[...]) * g_ref[...]
    wt = wt_ref[...]

    def finish(pre, head_axis):
        t = GATE_SOFTCAP * jnp.tanh(pre / GATE_SOFTCAP)
        ls = jnp.minimum(t, 0.0) - jnp.log1p(jnp.exp(-jnp.abs(t)))
        idx = lax.broadcasted_iota(I32, t.shape, head_axis)
        return jnp.where(idx < H, t, ls)

    gc_ref[...] = finish(_dot_nt_3pass(xn, wt) + bc_ref[...], 1)
    gr_ref[...] = finish(_dot_nt_3pass(wt, xn) + br_ref[...], 0)


def _gates(x, gain, w_gate_t, b_gates):
    T, D = x.shape
    G = w_gate_t.shape[0]
    tm = min(T, 512)
    return pl.pallas_call(
        _gates_kernel,
        out_shape=[jax.ShapeDtypeStruct((T, G), F32), jax.ShapeDtypeStruct((G, T), F32)],
        grid=(T // tm,),
        in_specs=[pl.BlockSpec((tm, D), lambda i: (i, 0)),
                  pl.BlockSpec((1, D), lambda i: (0, 0)),
                  pl.BlockSpec((G, D), lambda i: (0, 0)),
                  pl.BlockSpec((1, G), lambda i: (0, 0)),
                  pl.BlockSpec((G, 1), lambda i: (0, 0))],
        out_specs=[pl.BlockSpec((tm, G), lambda i: (i, 0)),
                   pl.BlockSpec((G, tm), lambda i: (0, i))],
        compiler_params=_params("parallel"),
        name="mlstm_gates",
    )(x, gain, w_gate_t, b_gates.reshape(1, G), b_gates.reshape(G, 1))


def _mlstm_kernel(q_ref, k_ref, v_ref, o_ref, gc_ref, gr_ref, on_ref,
                  h_ref, c_out, n_out, m_out, c_s, n_s, m_s, *, dk, dv):
    H = MLSTM_HEADS
    c = pl.program_id(1)
    L = q_ref.shape[0]

    @pl.when(c == 0)
    def _():
        c_s[...] = jnp.zeros_like(c_s)
        n_s[...] = jnp.zeros_like(n_s)
        m_s[...] = jnp.zeros_like(m_s)

    row = lax.broadcasted_iota(I32, (L, L), 0)
    col = lax.broadcasted_iota(I32, (L, L), 1)
    tril = row >= col
    for h in range(H):
        q = q_ref[:, h * dk:(h + 1) * dk]
        k = k_ref[:, h * dk:(h + 1) * dk] * (dk ** -0.5)
        v = v_ref[:, h * dv:(h + 1) * dv]
        ig_c = gc_ref[:, h:h + 1]
        lf_c = gc_ref[:, H + h:H + h + 1]
        ig_r = gr_ref[h:h + 1, :]
        lf_r = gr_ref[H + h:H + h + 1, :]
        b_c = jnp.sum(jnp.where(tril, lf_r, 0.0), axis=1, keepdims=True)
        b_r = jnp.sum(jnp.where(row <= col, lf_c, 0.0), axis=0, keepdims=True)
        m_prev = m_s[h][:, 0:1]
        C = c_s[h]
        n = n_s[h]
        dmat = jnp.where(tril, b_c - b_r + ig_r, -jnp.inf)
        inter = b_c + m_prev
        m_t = jnp.maximum(inter, jnp.max(dmat, axis=1, keepdims=True))
        w_intra = jnp.exp(dmat - m_t)
        w_inter = jnp.exp(inter - m_t)
        qb = q.astype(BF16)
        vb = v.astype(BF16)
        s = _dot_nt(qb, k.astype(BF16)) * w_intra
        num = w_inter * _dot(qb, C.astype(BF16)) + _dot(s.astype(BF16), vb)
        den = w_inter * jnp.sum(q * n, axis=1, keepdims=True) + jnp.sum(s, axis=1, keepdims=True)
        hh = num / jnp.maximum(jnp.abs(den), jnp.exp(-m_t))
        b_last = b_c[L - 1:L, :]
        g_r = b_last - b_r + ig_r
        g_c = b_last - b_c + ig_c
        m_new = jnp.maximum(b_last + m_prev, jnp.max(g_r, axis=1, keepdims=True))
        decay = jnp.exp(b_last + m_prev - m_new)
        kw = k * jnp.exp(g_c - m_new)
        c_s[h] = decay * C + _dot_tn(kw.astype(BF16), vb)
        n_s[h] = decay * n + jnp.sum(kw, axis=0, keepdims=True)
        m_s[h] = jnp.broadcast_to(m_new, m_s.shape[1:])
        sl = slice(h * dv, (h + 1) * dv)
        h_ref[:, sl] = (_rms(hh) * on_ref[:, sl] * _sigmoid(o_ref[:, sl])).astype(h_ref.dtype)

    @pl.when(c == pl.num_programs(1) - 1)
    def _():
        c_out[0] = c_s[...]
        n_out[0] = n_s[...]
        m_out[0] = m_s[...]


def _mlstm_prompt(proj, gc, gr, out_norm, B, S):
    H = MLSTM_HEADS
    T = B * S
    hv = out_norm.shape[1]
    dv = hv // H
    dk = dv // 2
    hk = H * dk
    L = MLSTM_CHUNK
    nc = S // L
    G = gc.shape[1]
    rows = lambda b, c: b * nc + c
    return pl.pallas_call(
        functools.partial(_mlstm_kernel, dk=dk, dv=dv),
        out_shape=[jax.ShapeDtypeStruct((T, hv), BF16),
                   jax.ShapeDtypeStruct((B, H, dk, dv), F32),
                   jax.ShapeDtypeStruct((B, H, 1, dk), F32),
                   jax.ShapeDtypeStruct((B, H, 1, 128), F32)],
        grid=(B, nc),
        in_specs=[pl.BlockSpec((L, hk), lambda b, c: (rows(b, c), 0)),
                  pl.BlockSpec((L, hk), lambda b, c: (rows(b, c), 1)),
                  pl.BlockSpec((L, hv), lambda b, c: (rows(b, c), 1)),
                  pl.BlockSpec((L, hv), lambda b, c: (rows(b, c), 2)),
                  pl.BlockSpec((L, G), lambda b, c: (rows(b, c), 0)),
                  pl.BlockSpec((G, L), lambda b, c: (0, rows(b, c))),
                  pl.BlockSpec((1, hv), lambda b, c: (0, 0))],
        out_specs=[pl.BlockSpec((L, hv), lambda b, c: (rows(b, c), 0)),
                   pl.BlockSpec((1, H, dk, dv), lambda b, c: (b, 0, 0, 0)),
                   pl.BlockSpec((1, H, 1, dk), lambda b, c: (b, 0, 0, 0)),
                   pl.BlockSpec((1, H, 1, 128), lambda b, c: (b, 0, 0, 0))],
        scratch_shapes=[pltpu.VMEM((H, dk, dv), F32), pltpu.VMEM((H, 1, dk), F32),
                        pltpu.VMEM((H, 1, 128), F32)],
        compiler_params=_params("arbitrary", "arbitrary"),
        name="mlstm_prompt",
    )(proj, proj, proj, proj, gc, gr, out_norm)


def _mlstm_step_kernel(qr_ref, qc_ref, kr_ref, kc_ref, v_ref, o_ref, ig_ref, lf_ref, on_ref,
                       c_ref, n_ref, m_ref, h_ref, c_out, n_out, m_out):
    H = MLSTM_HEADS
    dk = qr_ref.shape[-1]
    dv = v_ref.shape[-1]
    for h in range(H):
        q_r = qr_ref[0, h]
        q_c = qc_ref[0, h]
        k_r = kr_ref[0, h] * (dk ** -0.5)
        k_c = kc_ref[0, h] * (dk ** -0.5)
        v = v_ref[0, h]
        ig = ig_ref[0, h]
        lf = lf_ref[0, h]
        C = c_ref[0, h]
        n = n_ref[0, h]
        m = m_ref[0, h]
        inter = lf + m
        m_t = jnp.maximum(inter, ig)
        w_intra = jnp.exp(ig - m_t)
        w_inter = jnp.exp(inter - m_t)
        s = jnp.sum(q_r * k_r, axis=1, keepdims=True) * w_intra
        num = w_inter * jnp.sum(q_c * C, axis=0, keepdims=True) + s * v
        den = w_inter * jnp.sum(q_r * n, axis=1, keepdims=True) + s
        hh = num / jnp.maximum(jnp.abs(den), jnp.exp(-m_t))
        decay = jnp.exp(inter - m_t)
        w_k = jnp.exp(ig - m_t)
        c_out[0, h] = decay * C + (k_c * w_k) * v
        n_out[0, h] = decay * n + k_r * w_k
        m_out[0, h] = m_t
        sl = slice(h * dv, (h + 1) * dv)
        h_ref[0, :, sl] = _rms(hh) * on_ref[:, sl] * _sigmoid(o_ref[0, :, sl])


def _mlstm_decode(proj, gc, out_norm, C0, n0, m0):
    H = MLSTM_HEADS
    Bd, _, dk, dv = C0.shape
    hk, hv = H * dk, H * dv
    q = proj[:Bd, :hk].reshape(Bd, H, dk)
    k = proj[:Bd, hk:2 * hk].reshape(Bd, H, dk)
    v = proj[:Bd, 2 * hk:2 * hk + hv].reshape(Bd, H, 1, dv)
    o = proj[:Bd, 2 * hk + hv:2 * hk + 2 * hv].reshape(Bd, 1, hv)
    ig = gc[:Bd, :H].reshape(Bd, H, 1, 1)
    lf = gc[:Bd, H:].reshape(Bd, H, 1, 1)
    spec = lambda *tail: pl.BlockSpec((1, H) + tail, lambda b: (b, 0, 0, 0))
    return pl.pallas_call(
        _mlstm_step_kernel,
        out_shape=[jax.ShapeDtypeStruct((Bd, 1, hv), F32),
                   jax.ShapeDtypeStruct((Bd, H, dk, dv), F32),
                   jax.ShapeDtypeStruct((Bd, H, 1, dk), F32),
                   jax.ShapeDtypeStruct((Bd, H, 1, 1), F32)],
        grid=(Bd,),
        in_specs=[spec(1, dk), spec(dk, 1), spec(1, dk), spec(dk, 1), spec(1, dv),
                  pl.BlockSpec((1, 1, hv), lambda b: (b, 0, 0)),
                  spec(1, 1), spec(1, 1),
                  pl.BlockSpec((1, hv), lambda b: (0, 0)),
                  spec(dk, dv), spec(1, dk), spec(1, 1)],
        out_specs=[pl.BlockSpec((1, 1, hv), lambda b: (b, 0, 0)),
                   spec(dk, dv), spec(1, dk), spec(1, 1)],
        compiler_params=_params("parallel"),
        name="mlstm_decode",
    )(q.reshape(Bd, H, 1, dk), q.reshape(Bd, H, dk, 1), k.reshape(Bd, H, 1, dk),
      k.reshape(Bd, H, dk, 1), v, o, ig, lf, out_norm, C0, n0.reshape(Bd, H, 1, dk),
      m0.reshape(Bd, H, 1, 1))


def _band_kernel(q_ref, kc_ref, kp_ref, o_ref, lse_ref, *, heads):
    D = ATTN_HEAD_DIM
    bq = q_ref.shape[1]
    W = heads * D
    u = pl.program_id(2)
    row = lax.broadcasted_iota(I32, (bq, bq), 0)
    col = lax.broadcasted_iota(I32, (bq, bq), 1)
    cur_ok = row >= col
    prev_ok = jnp.logical_and(row <= col, u > 0)
    lane = lax.broadcasted_iota(I32, (bq, 128), 1)
    lse_blk = jnp.zeros((bq, 128), F32)
    scale = D ** -0.5
    for h in range(heads):
        sl = slice(h * D, (h + 1) * D)
        vsl = slice(W + h * D, W + (h + 1) * D)
        qh = q_ref[0, :, sl].astype(BF16)
        s_c = jnp.where(cur_ok, _dot_nt(qh, kc_ref[0, :, sl].astype(BF16)) * scale, -jnp.inf)
        s_p = jnp.where(prev_ok, _dot_nt(qh, kp_ref[0, :, sl].astype(BF16)) * scale, -jnp.inf)
        mx = jnp.maximum(jnp.max(s_c, axis=1, keepdims=True), jnp.max(s_p, axis=1, keepdims=True))
        e_c = jnp.exp(s_c - mx)
        e_p = jnp.exp(s_p - mx)
        l = jnp.sum(e_c, axis=1, keepdims=True) + jnp.sum(e_p, axis=1, keepdims=True)
        p_c = (e_c / l).astype(BF16)
        p_p = (e_p / l).astype(BF16)
        o_ref[0, :, sl] = (_dot(p_p, kp_ref[0, :, vsl].astype(BF16))
                           + _dot(p_c, kc_ref[0, :, vsl].astype(BF16)))
        lse_blk = jnp.where(lane == h, mx + jnp.log(l), lse_blk)
    lse_ref[0] = lse_blk


def _band_attention(q, kv, g, dilation, B, S, heads):
    D = ATTN_HEAD_DIM
    W = heads * D
    T = B * S
    sub = S // dilation
    bq = ATTN_BLOCK
    nb = sub // bq
    n_groups = q.shape[1] // W
    qv = q.reshape(B, sub, dilation * n_groups * W)
    kvv = kv.reshape(B, sub, dilation * n_groups * 2 * W)
    o, lse = pl.pallas_call(
        functools.partial(_band_kernel, heads=heads),
        out_shape=[jax.ShapeDtypeStruct((B, sub, dilation * W), F32),
                   jax.ShapeDtypeStruct((B, sub, dilation * 128), F32)],
        grid=(B, dilation, nb),
        in_specs=[pl.BlockSpec((1, bq, W), lambda b, r, u: (b, u, r * n_groups + g)),
                  pl.BlockSpec((1, bq, 2 * W), lambda b, r, u: (b, u, r * n_groups + g)),
                  pl.BlockSpec((1, bq, 2 * W),
                               lambda b, r, u: (b, jnp.maximum(u - 1, 0), r * n_groups + g))],
        out_specs=[pl.BlockSpec((1, bq, W), lambda b, r, u: (b, u, r)),
                   pl.BlockSpec((1, bq, 128), lambda b, r, u: (b, u, r))],
        compiler_params=_params("parallel", "parallel", "arbitrary"),
        name=f"band_attention_g{g}",
    )(qv, kvv, kvv)
    return o.reshape(T, W), lse.reshape(T, 128)


def _merge_kernel(*refs, heads, n_groups):
    D = ATTN_HEAD_DIM
    o_refs = refs[:n_groups]
    lse_refs = refs[n_groups:2 * n_groups]
    out_ref = refs[2 * n_groups]
    lses = [r[...] for r in lse_refs]
    mx = functools.reduce(jnp.maximum, lses)
    es = [jnp.exp(l - mx) for l in lses]
    tot = functools.reduce(lambda a, b: a + b, es)
    ws = [e / tot for e in es]
    for h in range(heads):
        sl = slice(h * D, (h + 1) * D)
        acc = ws[0][:, h:h + 1] * o_refs[0][:, sl]
        for g in range(1, n_groups):
            acc = acc + ws[g][:, h:h + 1] * o_refs[g][:, sl]
        out_ref[:, sl] = acc.astype(out_ref.dtype)


def _merge_groups(outs, lses, heads):
    T, W = outs[0].shape
    tm = 512
    n = len(outs)
    return pl.pallas_call(
        functools.partial(_merge_kernel, heads=heads, n_groups=n),
        out_shape=jax.ShapeDtypeStruct((T, W), BF16),
        grid=(T // tm,),
        in_specs=[pl.BlockSpec((tm, W), lambda i: (i, 0)) for _ in outs]
                 + [pl.BlockSpec((tm, 128), lambda i: (i, 0)) for _ in lses],
        out_specs=pl.BlockSpec((tm, W), lambda i: (i, 0)),
        compiler_params=_params("parallel"),
        name="merge_groups",
    )(*outs, *lses)


def _decode_attn_kernel(q_ref, kvn_ref, *refs, heads):
    D = ATTN_HEAD_DIM
    W = heads * D
    n_groups = len(refs) - 1
    buf_refs = refs[:n_groups]
    out_ref = refs[n_groups]
    scale = D ** -0.5
    for h in range(heads):
        outs, lses = [], []
        for g in range(n_groups):
            qh = q_ref[0, :, g * W + h * D:g * W + (h + 1) * D]
            k_new = kvn_ref[0, :, g * 2 * W + h * D:g * 2 * W + (h + 1) * D]
            v_new = kvn_ref[0, :, g * 2 * W + W + h * D:g * 2 * W + W + (h + 1) * D]
            k_buf = buf_refs[g][0, :, h * D:(h + 1) * D]
            v_buf = buf_refs[g][0, :, W + h * D:W + (h + 1) * D]
            s_buf = jnp.sum(k_buf * qh, axis=1, keepdims=True) * scale
            s_new = jnp.sum(k_new * qh, axis=1, keepdims=True) * scale
            mx = jnp.maximum(jnp.max(s_buf, axis=0, keepdims=True), s_new)
            e_buf = jnp.exp(s_buf - mx)
            e_new = jnp.exp(s_new - mx)
            l = jnp.sum(e_buf, axis=0, keepdims=True) + e_new
            outs.append(jnp.sum((e_buf / l) * v_buf, axis=0, keepdims=True) + (e_new / l) * v_new)
            lses.append(mx + jnp.log(l))
        mx = functools.reduce(jnp.maximum, lses)
        es = [jnp.exp(l - mx) for l in lses]
        tot = functools.reduce(lambda a, b: a + b, es)
        acc = (es[0] / tot) * outs[0]
        for g in range(1, n_groups):
            acc = acc + (es[g] / tot) * outs[g]
        out_ref[0, :, h * D:(h + 1) * D] = acc


def _decode_attention(q, kv_new, caches, heads):
    D = ATTN_HEAD_DIM
    W = heads * D
    Bd = caches[0].shape[0]
    steps = ATTN_BLOCK
    views = []
    for cache, (window, dilation) in zip(caches, DILATED_GROUPS):
        Lb = cache.shape[1]
        assert window // dilation == steps and Lb == steps * dilation
        views.append(cache.reshape(Bd, steps, dilation * 2 * W))
    return pl.pallas_call(
        functools.partial(_decode_attn_kernel, heads=heads),
        out_shape=jax.ShapeDtypeStruct((Bd, 1, W), F32),
        grid=(Bd,),
        in_specs=[pl.BlockSpec((1, 1, q.shape[-1]), lambda b: (b, 0, 0)),
                  pl.BlockSpec((1, 1, kv_new.shape[-1]), lambda b: (b, 0, 0))]
                 + [pl.BlockSpec((1, steps, 2 * W), lambda b: (b, 0, 0)) for _ in views],
        out_specs=pl.BlockSpec((1, 1, W), lambda b: (b, 0, 0)),
        compiler_params=_params("parallel"),
        name="decode_attention",
    )(q[:Bd].reshape(Bd, 1, -1), kv_new[:Bd].reshape(Bd, 1, -1), *views)


def _router_kernel(x_ref, wt_ref, b_ref, o_ref):
    logits = _dot_nt_3pass(x_ref[...], wt_ref[...]) + b_ref[...]
    e = jnp.exp(logits - jnp.max(logits, axis=1, keepdims=True))
    p = e / jnp.sum(e, axis=1, keepdims=True)
    E = p.shape[1]
    idx = lax.broadcasted_iota(I32, p.shape, 1)
    p1 = jnp.max(p, axis=1, keepdims=True)
    i1 = jnp.min(jnp.where(p == p1, idx, E), axis=1, keepdims=True)
    rest = jnp.where(idx == i1, -1.0, p)
    p2 = jnp.max(rest, axis=1, keepdims=True)
    i2 = jnp.min(jnp.where(rest == p2, idx, E), axis=1, keepdims=True)
    tot = p1 + p2
    o_ref[...] = jnp.where(idx == 0, i1.astype(F32),
                 jnp.where(idx == 1, i2.astype(F32),
                 jnp.where(idx == 2, p1 / tot,
                 jnp.where(idx == 3, p2 / tot, 0.0))))


def _router(xn, w_router_t, b_router):
    T, D = xn.shape
    E = w_router_t.shape[0]
    tm = min(T, 512)
    return pl.pallas_call(
        _router_kernel,
        out_shape=jax.ShapeDtypeStruct((T, E), F32),
        grid=(T // tm,),
        in_specs=[pl.BlockSpec((tm, D), lambda i: (i, 0)),
                  pl.BlockSpec((E, D), lambda i: (0, 0)),
                  pl.BlockSpec((1, E), lambda i: (0, 0))],
        out_specs=pl.BlockSpec((tm, E), lambda i: (i, 0)),
        compiler_params=_params("parallel"),
        name="router",
    )(xn, w_router_t, b_router.reshape(1, E))


def _route_plan(top_i, tm, n_tiles):
    T = top_i.shape[0]
    e_flat = top_i.reshape(-1)
    onehot = (e_flat[:, None] == jnp.arange(N_EXPERTS, dtype=I32)[None, :]).astype(I32)
    csum = jnp.cumsum(onehot, axis=0)
    rank = jnp.sum(onehot * csum, axis=1) - 1
    counts = csum[-1]
    gsz = ((counts + tm - 1) // tm) * tm
    gend = jnp.cumsum(gsz)
    gstart = gend - gsz
    pos = (jnp.sum(onehot * gstart[None, :], axis=1) + rank).astype(I32)
    src = jnp.zeros((n_tiles * tm,), I32).at[pos].set(jnp.arange(TOP_K * T, dtype=I32) // TOP_K)
    tile_start = jnp.arange(n_tiles, dtype=I32) * tm
    tile_valid = (tile_start < gend[-1]).astype(I32)
    tile_e = jnp.sum((tile_start[:, None] >= gend[None, :]).astype(I32), axis=1)
    last_e = jnp.sum((gend[-1] - 1 >= gend).astype(I32))
    tile_e = jnp.where(tile_valid > 0, tile_e, last_e).astype(I32)
    return pos, src, tile_e, tile_valid


def _experts_kernel(te_ref, tv_ref, src_ref, x_hbm, wg_ref, wu_ref, wd_ref, o_ref, xf_ref, xb_ref, sem,
                    *, tm):
    i = pl.program_id(0)
    c = pl.program_id(1)
    valid = tv_ref[i] > 0

    def row_copy(r):
        return pltpu.make_async_copy(x_hbm.at[pl.ds(src_ref[i * tm + r], 1)],
                                     xf_ref.at[pl.ds(r, 1)], sem.at[0])

    @pl.when(jnp.logical_and(valid, c == 0))
    def _():
        def start(r, carry):
            row_copy(r).start()
            return carry

        def wait(r, carry):
            row_copy(r).wait()
            return carry

        lax.fori_loop(0, tm, start, 0)
        lax.fori_loop(0, tm, wait, 0)
        xb_ref[...] = xf_ref[...].astype(BF16)

    @pl.when(valid)
    def _():
        x = xb_ref[...]
        g = _dot(x, wg_ref[0].astype(BF16))
        u = _dot(x, wu_ref[0].astype(BF16))
        part = _dot((g * _sigmoid(g) * u).astype(BF16), wd_ref[0].astype(BF16))

        @pl.when(c == 0)
        def _():
            o_ref[...] = part

        @pl.when(c > 0)
        def _():
            o_ref[...] += part

    @pl.when(jnp.logical_and(jnp.logical_not(valid), c == 0))
    def _():
        o_ref[...] = jnp.zeros_like(o_ref)


def _experts(xn, pos_plan, w_gate, w_up, w_down, tm, tf):
    _, src, tile_e, tile_valid = pos_plan
    T, D = xn.shape
    Fe = w_gate.shape[-1]
    nf = Fe // tf
    n_tiles = tile_e.shape[0]

    def chunk(i, c, tv):
        return jnp.where(tv[i] > 0, c, nf - 1)

    return pl.pallas_call(
        functools.partial(_experts_kernel, tm=tm),
        out_shape=jax.ShapeDtypeStruct((n_tiles * tm, D), F32),
        grid_spec=pltpu.PrefetchScalarGridSpec(
            num_scalar_prefetch=3,
            grid=(n_tiles, nf),
            in_specs=[pl.BlockSpec(memory_space=pl.ANY),
                      pl.BlockSpec((1, D, tf), lambda i, c, te, tv, src: (te[i], 0, chunk(i, c, tv))),
                      pl.BlockSpec((1, D, tf), lambda i, c, te, tv, src: (te[i], 0, chunk(i, c, tv))),
                      pl.BlockSpec((1, tf, D), lambda i, c, te, tv, src: (te[i], chunk(i, c, tv), 0))],
            out_specs=pl.BlockSpec((tm, D), lambda i, c, te, tv, src: (i, 0)),
            scratch_shapes=[pltpu.VMEM((tm, D), F32), pltpu.VMEM((tm, D), BF16),
                            pltpu.SemaphoreType.DMA((1,))]),
        compiler_params=_params("arbitrary", "arbitrary"),
        name="experts",
    )(tile_e, tile_valid, src, xn, w_gate, w_up, w_down)


def _combine_kernel(pos_ref, h_ref, gate_ref, y_hbm, fg_ref, o_ref, ya_ref, yb_ref, sem, *, tm):
    i = pl.program_id(0)

    def copies(r):
        t = i * tm + r
        return (pltpu.make_async_copy(y_hbm.at[pl.ds(pos_ref[TOP_K * t], 1)],
                                      ya_ref.at[pl.ds(r, 1)], sem.at[0]),
                pltpu.make_async_copy(y_hbm.at[pl.ds(pos_ref[TOP_K * t + 1], 1)],
                                      yb_ref.at[pl.ds(r, 1)], sem.at[1]))

    def start(r, carry):
        a, b = copies(r)
        a.start()
        b.start()
        return carry

    def wait(r, carry):
        a, b = copies(r)
        a.wait()
        b.wait()
        return carry

    lax.fori_loop(0, tm, start, 0)
    lax.fori_loop(0, tm, wait, 0)
    hsum = h_ref[...] + (gate_ref[:, 0:1] * ya_ref[...] + gate_ref[:, 1:2] * yb_ref[...])
    o_ref[...] = _rms(hsum) * fg_ref[...]


def _combine(h, gates, y_sorted, pos, final_gain):
    T, D = h.shape
    tm = min(T, 256)
    return pl.pallas_call(
        functools.partial(_combine_kernel, tm=tm),
        out_shape=jax.ShapeDtypeStruct((T, D), F32),
        grid_spec=pltpu.PrefetchScalarGridSpec(
            num_scalar_prefetch=1,
            grid=(T // tm,),
            in_specs=[pl.BlockSpec((tm, D), lambda i, pos: (i, 0)),
                      pl.BlockSpec((tm, TOP_K), lambda i, pos: (i, 0)),
                      pl.BlockSpec(memory_space=pl.ANY),
                      pl.BlockSpec((1, D), lambda i, pos: (0, 0))],
            out_specs=pl.BlockSpec((tm, D), lambda i, pos: (i, 0)),
            scratch_shapes=[pltpu.VMEM((tm, D), F32), pltpu.VMEM((tm, D), F32),
                            pltpu.SemaphoreType.DMA((2,))]),
        compiler_params=_params("arbitrary"),
        name="moe_combine",
    )(pos, h, gates, y_sorted, final_gain)


def _trunk(x, mixer_state, kv_caches, B, S, p):
    T, D = x.shape
    H = MLSTM_HEADS
    hv = p["mlstm_out_norm"].shape[-1]
    hk = hv // 2
    heads = p["attn_w_out"].shape[1] // ATTN_HEAD_DIM
    W = heads * ATTN_HEAD_DIM
    n_groups = len(DILATED_GROUPS)

    w_in = p["mlstm_w_in"][0]
    (xn,) = _norm(x, p["norm_mix"][0:1], [BF16])
    proj = _matmul(xn, w_in, 2 * hk + 2 * hv, F32)
    gc, gr = _gates(x, p["norm_mix"][0:1], w_in[:, 2 * hk + 2 * hv:].T, p["mlstm_b_gates"][0])
    if mixer_state is None:
        hg, C, n, m = _mlstm_prompt(proj, gc, gr, p["mlstm_out_norm"], B, S)
        m = m[:, :, 0, 0]
    else:
        hg, C, n, m = _mlstm_decode(proj, gc, p["mlstm_out_norm"], *mixer_state)
        hg = jnp.zeros((T, hv), BF16).at[:B].set(hg[:, 0].astype(BF16))
        m = m[:, :, 0, 0]
    n = n[:, :, 0]
    h1 = _matmul(hg, p["mlstm_w_out"][0], D, F32, res=x)

    (xn,) = _norm(h1, p["norm_ffn"][0:1], [BF16])
    act = _glu(xn, p["ffn_w_gate"][0], p["ffn_w_up"][0])
    h2 = _matmul(act, p["ffn_w_down"][0], D, F32, res=h1, tn=512)

    xkv, xq = _norm(h2, jnp.concatenate([p["kv_norm"][None], p["norm_mix"][1:2]]), [BF16, BF16])
    kv = _matmul(xkv, p["w_kv"], n_groups * 2 * W, F32)
    q = _matmul(xq, p["attn_w_q"][0], n_groups * W, F32)
    if kv_caches is None:
        outs, lses = [], []
        for g, (_, dilation) in enumerate(DILATED_GROUPS):
            o_g, lse_g = _band_attention(q, kv, g, dilation, B, S, heads)
            outs.append(o_g)
            lses.append(lse_g)
        att = _merge_groups(outs, lses, heads)
    else:
        att = _decode_attention(q, kv, kv_caches, heads)
        att = jnp.zeros((T, W), BF16).at[:B].set(att[:, 0].astype(BF16))
    h3 = _matmul(att, p["attn_w_out"][0], D, F32, res=h2)

    (xn,) = _norm(h3, p["norm_ffn"][1:2], [F32])
    route = _router(xn, p["moe_w_router"][0].T, p["moe_b_router"][0])
    top_i = route[:, :TOP_K].astype(I32)
    gates = route[:, TOP_K:2 * TOP_K]
    tm_e = min(1024, T)
    n_tiles = (TOP_K * T) // tm_e + N_EXPERTS
    plan = _route_plan(top_i, tm_e, n_tiles)
    y_sorted = _experts(xn, plan, p["moe_w_gate"][0], p["moe_w_up"][0], p["moe_w_down"][0], tm_e, 256)
    y = _combine(h3, gates, y_sorted, plan[0], p["final_norm"][None])

    kv6 = kv.reshape(-1, n_groups, 2, heads, ATTN_HEAD_DIM)
    return y, C[None], n[None], m[None], kv6


def kernel(x_prompt, x_sample, state_mlstm_C, state_mlstm_n, state_mlstm_m, cache_kv_w128, cache_kv_w512, cache_kv_w2048, norm_mix, norm_ffn, mlstm_w_in, mlstm_b_gates, mlstm_out_norm, mlstm_w_out, kv_norm, w_kv, attn_w_q, attn_w_out, ffn_w_gate, ffn_w_up, ffn_w_down, moe_w_router, moe_b_router, moe_w_gate, moe_w_up, moe_w_down, final_norm):
    assert norm_mix.shape[0] == 2 and mlstm_w_in.shape[0] == 1 and attn_w_q.shape[0] == 1
    p = dict(norm_mix=norm_mix, norm_ffn=norm_ffn, mlstm_w_in=mlstm_w_in, mlstm_b_gates=mlstm_b_gates,
             mlstm_out_norm=mlstm_out_norm, mlstm_w_out=mlstm_w_out, kv_norm=kv_norm, w_kv=w_kv,
             attn_w_q=attn_w_q, attn_w_out=attn_w_out, ffn_w_gate=ffn_w_gate, ffn_w_up=ffn_w_up,
             ffn_w_down=ffn_w_down, moe_w_router=moe_w_router, moe_b_router=moe_b_router,
             moe_w_gate=moe_w_gate, moe_w_up=moe_w_up, moe_w_down=moe_w_down, final_norm=final_norm)
    B, S, D = x_prompt.shape
    Bd = x_sample.shape[0]
    assert x_sample.shape[1] == 1 and Bd <= SAMPLE_ROWS

    y_p, C_p, n_p, m_p, kv_p = _trunk(x_prompt.reshape(B * S, D), None, None, B, S, p)
    kv_p = kv_p.reshape((B, S) + kv_p.shape[1:])
    kv_rows_p = [kv_p[:, S - min(window, S):, g] for g, (window, _) in enumerate(DILATED_GROUPS)]

    xs = jnp.zeros((SAMPLE_ROWS, D), F32).at[:Bd].set(x_sample[:, 0])
    state = (state_mlstm_C[0], state_mlstm_n[0], state_mlstm_m[0])
    y_s, C_s, n_s, m_s, kv_s = _trunk(xs, state, (cache_kv_w128, cache_kv_w512, cache_kv_w2048), Bd, 1, p)
    kv_s = kv_s[:Bd, None]
    kv_rows_s = [kv_s[:, :, g] for g in range(len(DILATED_GROUPS))]

    return (y_p.reshape(B, S, D), y_s[:Bd, None], C_p, n_p, m_p, *kv_rows_p,
            C_s, n_s, m_s, *kv_rows_s)
```

```python
import functools

import jax
import jax.numpy as jnp
from jax import lax
from jax.experimental import pallas as pl
from jax.experimental.pallas import tpu as pltpu

F32 = jnp.float32
BF16 = jnp.bfloat16
I32 = jnp.int32

EPS = 1e-6
GATE_SOFTCAP = 15.0
MLSTM_HEADS = 4
ATTN_HEAD_DIM = 128
DILATED_GROUPS = ((128, 1), (512, 4), (2048, 16))
N_EXPERTS = 8
TOP_K = 2
MLSTM_CHUNK = 256
ATTN_BLOCK = 128
SAMPLE_ROWS = 16

VMEM_LIMIT = 56 * 1024 * 1024


def _params(*sem):
    return pltpu.CompilerParams(dimension_semantics=sem, vmem_limit_bytes=VMEM_LIMIT)


def _dot(a, b):
    return jnp.dot(a, b, preferred_element_type=F32)


def _dot_nt(a, b):
    return lax.dot_general(a, b, (((1,), (1,)), ((), ())), preferred_element_type=F32)


def _dot_tn(a, b):
    return lax.dot_general(a, b, (((0,), (0,)), ((), ())), preferred_element_type=F32)


def _split(x):
    hi = x.astype(BF16)
    lo = (x - hi.astype(F32)).astype(BF16)
    return hi, lo


def _dot_nt_3pass(x, w):
    xh, xl = _split(x)
    wh, wl = _split(w)
    return _dot_nt(xh, wh) + (_dot_nt(xl, wh) + _dot_nt(xh, wl))


def _sigmoid(x):
    return 1.0 / (1.0 + jnp.exp(-x))


def _rms(x):
    return x * lax.rsqrt(jnp.mean(x * x, axis=-1, keepdims=True) + EPS)


def _norm_kernel(x_ref, g_ref, *o_refs):
    y = _rms(x_ref[...])
    for i, o_ref in enumerate(o_refs):
        o_ref[...] = (y * g_ref[i:i + 1, :]).astype(o_ref.dtype)


def _norm(x, gains, dtypes):
    T, D = x.shape
    tm = min(T, 512)
    return pl.pallas_call(
        _norm_kernel,
        out_shape=[jax.ShapeDtypeStruct((T, D), dt) for dt in dtypes],
        grid=(T // tm,),
        in_specs=[pl.BlockSpec((tm, D), lambda i: (i, 0)),
                  pl.BlockSpec(gains.shape, lambda i: (0, 0))],
        out_specs=[pl.BlockSpec((tm, D), lambda i: (i, 0)) for _ in dtypes],
        compiler_params=_params("parallel"),
        name="rmsnorm",
    )(x, gains)


def _mm_kernel(a_ref, w_ref, *rest, has_res):
    if has_res:
        r_ref, o_ref, wb_ref = rest
    else:
        o_ref, wb_ref = rest

    @pl.when(pl.program_id(1) == 0)
    def _():
        wb_ref[...] = w_ref[...].astype(BF16)

    acc = _dot(a_ref[...], wb_ref[...])
    if has_res:
        acc = r_ref[...] + acc
    o_ref[...] = acc.astype(o_ref.dtype)


def _matmul(a, w, n_cols, out_dtype, res=None, tn=1024):
    M, K = a.shape
    tm = min(M, 512)
    tn = min(tn, n_cols)
    in_specs = [pl.BlockSpec((tm, K), lambda j, i: (i, 0)),
                pl.BlockSpec((K, tn), lambda j, i: (0, j))]
    args = [a, w]
    if res is not None:
        in_specs.append(pl.BlockSpec((tm, tn), lambda j, i: (i, j)))
        args.append(res)
    return pl.pallas_call(
        functools.partial(_mm_kernel, has_res=res is not None),
        out_shape=jax.ShapeDtypeStruct((M, n_cols), out_dtype),
        grid=(n_cols // tn, M // tm),
        in_specs=in_specs,
        out_specs=pl.BlockSpec((tm, tn), lambda j, i: (i, j)),
        scratch_shapes=[pltpu.VMEM((K, tn), BF16)],
        compiler_params=_params("arbitrary", "arbitrary"),
        name="matmul",
    )(*args)


def _glu_kernel(a_ref, wg_ref, wu_ref, o_ref, wgb_ref, wub_ref):
    @pl.when(pl.program_id(1) == 0)
    def _():
        wgb_ref[...] = wg_ref[...].astype(BF16)
        wub_ref[...] = wu_ref[...].astype(BF16)

    a = a_ref[...]
    g = _dot(a, wgb_ref[...])
    u = _dot(a, wub_ref[...])
    o_ref[...] = (g * _sigmoid(g) * u).astype(o_ref.dtype)


def _glu(a, wg, wu, tf=512):
    M, K = a.shape
    F = wg.shape[1]
    tm = min(M, 512)
    return pl.pallas_call(
        _glu_kernel,
        out_shape=jax.ShapeDtypeStruct((M, F), BF16),
        grid=(F // tf, M // tm),
        in_specs=[pl.BlockSpec((tm, K), lambda j, i: (i, 0)),
                  pl.BlockSpec((K, tf), lambda j, i: (0, j)),
                  pl.BlockSpec((K, tf), lambda j, i: (0, j))],
        out_specs=pl.BlockSpec((tm, tf), lambda j, i: (i, j)),
        scratch_shapes=[pltpu.VMEM((K, tf), BF16), pltpu.VMEM((K, tf), BF16)],
        compiler_params=_params("arbitrary", "arbitrary"),
        name="swiglu_up",
    )(a, wg, wu)


def _gates_kernel(x_ref, g_ref, wt_ref, bc_ref, br_ref, gc_ref, gr_ref):
    H = MLSTM_HEADS
    xn = _rms(x_ref[...]) * g_ref[...]
    wt = wt_ref[...]

    def finish(pre, head_axis):
        t = GATE_SOFTCAP * jnp.tanh(pre / GATE_SOFTCAP)
        ls = jnp.minimum(t, 0.0) - jnp.log1p(jnp.exp(-jnp.abs(t)))
        idx = lax.broadcasted_iota(I32, t.shape, head_axis)
        return jnp.where(idx < H, t, ls)

    gc_ref[...] = finish(_dot_nt_3pass(xn, wt) + bc_ref[...], 1)
    gr_ref[...] = finish(_dot_nt_3pass(wt, xn) + br_ref[...], 0)


def _gates(x, gain, w_gate_t, b_gates):
    T, D = x.shape
    G = w_gate_t.shape[0]
    tm = min(T, 512)
    return pl.pallas_call(
        _gates_kernel,
        out_shape=[jax.ShapeDtypeStruct((T, G), F32), jax.ShapeDtypeStruct((G, T), F32)],
        grid=(T // tm,),
        in_specs=[pl.BlockSpec((tm, D), lambda i: (i, 0)),
                  pl.BlockSpec((1, D), lambda i: (0, 0)),
                  pl.BlockSpec((G, D), lambda i: (0, 0)),
                  pl.BlockSpec((1, G), lambda i: (0, 0)),
                  pl.BlockSpec((G, 1), lambda i: (0, 0))],
        out_specs=[pl.BlockSpec((tm, G), lambda i: (i, 0)),
                   pl.BlockSpec((G, tm), lambda i: (0, i))],
        compiler_params=_params("parallel"),
        name="mlstm_gates",
    )(x, gain, w_gate_t, b_gates.reshape(1, G), b_gates.reshape(G, 1))


def _mlstm_kernel(q_ref, k_ref, v_ref, o_ref, gc_ref, gr_ref, on_ref,
                  h_ref, c_out, n_out, m_out, c_s, n_s, m_s, *, dk, dv):
    H = MLSTM_HEADS
    c = pl.program_id(1)
    L = q_ref.shape[0]

    @pl.when(c == 0)
    def _():
        c_s[...] = jnp.zeros_like(c_s)
        n_s[...] = jnp.zeros_like(n_s)
        m_s[...] = jnp.zeros_like(m_s)

    row = lax.broadcasted_iota(I32, (L, L), 0)
    col = lax.broadcasted_iota(I32, (L, L), 1)
    tril = row >= col
    for h in range(H):
        q = q_ref[:, h * dk:(h + 1) * dk]
        k = k_ref[:, h * dk:(h + 1) * dk] * (dk ** -0.5)
        v = v_ref[:, h * dv:(h + 1) * dv]
        ig_c = gc_ref[:, h:h + 1]
        lf_c = gc_ref[:, H + h:H + h + 1]
        ig_r = gr_ref[h:h + 1, :]
        lf_r = gr_ref[H + h:H + h + 1, :]
        b_c = jnp.sum(jnp.where(tril, lf_r, 0.0), axis=1, keepdims=True)
        b_r = jnp.sum(jnp.where(row <= col, lf_c, 0.0), axis=0, keepdims=True)
        m_prev = m_s[h][:, 0:1]
        C = c_s[h]
        n = n_s[h]
        dmat = jnp.where(tril, b_c - b_r + ig_r, -jnp.inf)
        inter = b_c + m_prev
        m_t = jnp.maximum(inter, jnp.max(dmat, axis=1, keepdims=True))
        w_intra = jnp.exp(dmat - m_t)
        w_inter = jnp.exp(inter - m_t)
        qb = q.astype(BF16)
        vb = v.astype(BF16)
        s = _dot_nt(qb, k.astype(BF16)) * w_intra
        num = w_inter * _dot(qb, C.astype(BF16)) + _dot(s.astype(BF16), vb)
        den = w_inter * jnp.sum(q * n, axis=1, keepdims=True) + jnp.sum(s, axis=1, keepdims=True)
        hh = num / jnp.maximum(jnp.abs(den), jnp.exp(-m_t))
        b_last = b_c[L - 1:L, :]
        g_r = b_last - b_r + ig_r
        g_c = b_last - b_c + ig_c
        m_new = jnp.maximum(b_last + m_prev, jnp.max(g_r, axis=1, keepdims=True))
        decay = jnp.exp(b_last + m_prev - m_new)
        kw = k * jnp.exp(g_c - m_new)
        c_s[h] = decay * C + _dot_tn(kw.astype(BF16), vb)
        n_s[h] = decay * n + jnp.sum(kw, axis=0, keepdims=True)
        m_s[h] = jnp.broadcast_to(m_new, m_s.shape[1:])
        sl = slice(h * dv, (h + 1) * dv)
        h_ref[:, sl] = (_rms(hh) * on_ref[:, sl] * _sigmoid(o_ref[:, sl])).astype(h_ref.dtype)

    @pl.when(c == pl.num_programs(1) - 1)
    def _():
        c_out[0] = c_s[...]
        n_out[0] = n_s[...]
        m_out[0] = m_s[...]


def _mlstm_prompt(proj, gc, gr, out_norm, B, S):
    H = MLSTM_HEADS
    T = B * S
    hv = out_norm.shape[1]
    dv = hv // H
    dk = dv // 2
    hk = H * dk
    L = MLSTM_CHUNK
    nc = S // L
    G = gc.shape[1]
    rows = lambda b, c: b * nc + c
    return pl.pallas_call(
        functools.partial(_mlstm_kernel, dk=dk, dv=dv),
        out_shape=[jax.ShapeDtypeStruct((T, hv), BF16),
                   jax.ShapeDtypeStruct((B, H, dk, dv), F32),
                   jax.ShapeDtypeStruct((B, H, 1, dk), F32),
                   jax.ShapeDtypeStruct((B, H, 1, 128), F32)],
        grid=(B, nc),
        in_specs=[pl.BlockSpec((L, hk), lambda b, c: (rows(b, c), 0)),
                  pl.BlockSpec((L, hk), lambda b, c: (rows(b, c), 1)),
                  pl.BlockSpec((L, hv), lambda b, c: (rows(b, c), 1)),
                  pl.BlockSpec((L, hv), lambda b, c: (rows(b, c), 2)),
                  pl.BlockSpec((L, G), lambda b, c: (rows(b, c), 0)),
                  pl.BlockSpec((G, L), lambda b, c: (0, rows(b, c))),
                  pl.BlockSpec((1, hv), lambda b, c: (0, 0))],
        out_specs=[pl.BlockSpec((L, hv), lambda b, c: (rows(b, c), 0)),
                   pl.BlockSpec((1, H, dk, dv), lambda b, c: (b, 0, 0, 0)),
                   pl.BlockSpec((1, H, 1, dk), lambda b, c: (b, 0, 0, 0)),
                   pl.BlockSpec((1, H, 1, 128), lambda b, c: (b, 0, 0, 0))],
        scratch_shapes=[pltpu.VMEM((H, dk, dv), F32), pltpu.VMEM((H, 1, dk), F32),
                        pltpu.VMEM((H, 1, 128), F32)],
        compiler_params=_params("arbitrary", "arbitrary"),
        name="mlstm_prompt",
    )(proj, proj, proj, proj, gc, gr, out_norm)


def _mlstm_step_kernel(qr_ref, qc_ref, kr_ref, kc_ref, v_ref, o_ref, ig_ref, lf_ref, on_ref,
                       c_ref, n_ref, m_ref, h_ref, c_out, n_out, m_out):
    H = MLSTM_HEADS
    dk = qr_ref.shape[-1]
    dv = v_ref.shape[-1]
    for h in range(H):
        q_r = qr_ref[0, h]
        q_c = qc_ref[0, h]
        k_r = kr_ref[0, h] * (dk ** -0.5)
        k_c = kc_ref[0, h] * (dk ** -0.5)
        v = v_ref[0, h]
        ig = ig_ref[0, h]
        lf = lf_ref[0, h]
        C = c_ref[0, h]
        n = n_ref[0, h]
        m = m_ref[0, h]
        inter = lf + m
        m_t = jnp.maximum(inter, ig)
        w_intra = jnp.exp(ig - m_t)
        w_inter = jnp.exp(inter - m_t)
        s = jnp.sum(q_r * k_r, axis=1, keepdims=True) * w_intra
        num = w_inter * jnp.sum(q_c * C, axis=0, keepdims=True) + s * v
        den = w_inter * jnp.sum(q_r * n, axis=1, keepdims=True) + s
        hh = num / jnp.maximum(jnp.abs(den), jnp.exp(-m_t))
        decay = jnp.exp(inter - m_t)
        w_k = jnp.exp(ig - m_t)
        c_out[0, h] = decay * C + (k_c * w_k) * v
        n_out[0, h] = decay * n + k_r * w_k
        m_out[0, h] = m_t
        sl = slice(h * dv, (h + 1) * dv)
        h_ref[0, :, sl] = _rms(hh) * on_ref[:, sl] * _sigmoid(o_ref[0, :, sl])


def _mlstm_decode(proj, gc, out_norm, C0, n0, m0):
    H = MLSTM_HEADS
    Bd, _, dk, dv = C0.shape
    hk, hv = H * dk, H * dv
    q = proj[:Bd, :hk].reshape(Bd, H, dk)
    k = proj[:Bd, hk:2 * hk].reshape(Bd, H, dk)
    v = proj[:Bd, 2 * hk:2 * hk + hv].reshape(Bd, H, 1, dv)
    o = proj[:Bd, 2 * hk + hv:2 * hk + 2 * hv].reshape(Bd, 1, hv)
    ig = gc[:Bd, :H].reshape(Bd, H, 1, 1)
    lf = gc[:Bd, H:].reshape(Bd, H, 1, 1)
    spec = lambda *tail: pl.BlockSpec((1, H) + tail, lambda b: (b, 0, 0, 0))
    return pl.pallas_call(
        _mlstm_step_kernel,
        out_shape=[jax.ShapeDtypeStruct((Bd, 1, hv), F32),
                   jax.ShapeDtypeStruct((Bd, H, dk, dv), F32),
                   jax.ShapeDtypeStruct((Bd, H, 1, dk), F32),
                   jax.ShapeDtypeStruct((Bd, H, 1, 1), F32)],
        grid=(Bd,),
        in_specs=[spec(1, dk), spec(dk, 1), spec(1, dk), spec(dk, 1), spec(1, dv),
                  pl.BlockSpec((1, 1, hv), lambda b: (b, 0, 0)),
                  spec(1, 1), spec(1, 1),
                  pl.BlockSpec((1, hv), lambda b: (0, 0)),
                  spec(dk, dv), spec(1, dk), spec(1, 1)],
        out_specs=[pl.BlockSpec((1, 1, hv), lambda b: (b, 0, 0)),
                   spec(dk, dv), spec(1, dk), spec(1, 1)],
        compiler_params=_params("parallel"),
        name="mlstm_decode",
    )(q.reshape(Bd, H, 1, dk), q.reshape(Bd, H, dk, 1), k.reshape(Bd, H, 1, dk),
      k.reshape(Bd, H, dk, 1), v, o, ig, lf, out_norm, C0, n0.reshape(Bd, H, 1, dk),
      m0.reshape(Bd, H, 1, 1))


def _rows(start, size, stride):
    return pl.ds(start, size) if stride == 1 else pl.ds(start, size, stride=stride)


def _attn_kernel(*refs, dilations, seq):
    G = len(dilations)
    q_refs, k_refs, v_refs = refs[:G], refs[G:2 * G], refs[2 * G:3 * G]
    out_ref = refs[3 * G]
    o_scr, lse_scr, s_scr, p_scr = refs[3 * G + 1:]
    bq = ATTN_BLOCK
    D = ATTN_HEAD_DIM
    scale = D ** -0.5
    row = lax.broadcasted_iota(I32, (bq, 2 * bq), 0)
    col = lax.broadcasted_iota(I32, (bq, 2 * bq), 1)
    bias_full = jnp.where(jnp.logical_and(col - row >= 0, col - row <= bq), 0.0, -jnp.inf)
    row1 = lax.broadcasted_iota(I32, (bq, bq), 0)
    col1 = lax.broadcasted_iota(I32, (bq, bq), 1)
    bias_first = jnp.where(row1 >= col1, 0.0, -jnp.inf)
    for g, r in enumerate(dilations):
        nb = seq // (r * bq)
        blocks = [(rho, u) for rho in range(r) for u in range(nb)]
        for i, (rho, u) in enumerate(blocks):
            q = q_refs[g][_rows(rho + u * bq * r, bq, r), :].astype(BF16)
            if u == 0:
                k = k_refs[g][_rows(rho, bq, r), :].astype(BF16)
                s_scr[i, :, :bq] = _dot_nt(q, k) * scale + bias_first
            else:
                k = k_refs[g][_rows(rho + (u - 1) * bq * r, 2 * bq, r), :].astype(BF16)
                s_scr[i] = _dot_nt(q, k) * scale + bias_full
        for i, (rho, u) in enumerate(blocks):
            s = s_scr[i, :, :bq] if u == 0 else s_scr[i]
            mx = jnp.max(s, axis=1, keepdims=True)
            e = jnp.exp(s - mx)
            l = jnp.sum(e, axis=1, keepdims=True)
            p = (e / l).astype(BF16)
            if u == 0:
                p_scr[i, :, :bq] = p
            else:
                p_scr[i] = p
            lse_scr[g, _rows(rho + u * bq * r, bq, r), :] = jnp.broadcast_to(mx + jnp.log(l), (bq, D))
        for i, (rho, u) in enumerate(blocks):
            if u == 0:
                v = v_refs[g][_rows(rho, bq, r), :].astype(BF16)
                o = _dot(p_scr[i, :, :bq], v)
            else:
                v = v_refs[g][_rows(rho + (u - 1) * bq * r, 2 * bq, r), :].astype(BF16)
                o = _dot(p_scr[i], v)
            o_scr[g, _rows(rho + u * bq * r, bq, r), :] = o
    lses = [lse_scr[g] for g in range(G)]
    mx = functools.reduce(jnp.maximum, lses)
    es = [jnp.exp(l - mx) for l in lses]
    tot = functools.reduce(lambda a, b: a + b, es)
    acc = (es[0] / tot) * o_scr[0]
    for g in range(1, G):
        acc = acc + (es[g] / tot) * o_scr[g]
    out_ref[...] = acc.astype(out_ref.dtype)


def _attention(q, kv, B, S, heads):
    D = ATTN_HEAD_DIM
    T = B * S
    G = len(DILATED_GROUPS)
    dil = tuple(d for _, d in DILATED_GROUPS)
    assert all(w // d == ATTN_BLOCK and S % (d * ATTN_BLOCK) == 0 for w, d in DILATED_GROUPS)
    nblk = S // ATTN_BLOCK
    spec = lambda col: pl.BlockSpec((S, D), lambda b, h: (b, col(h)))
    in_specs = ([spec(lambda h, g=g: g * heads + h) for g in range(G)]
                + [spec(lambda h, g=g: 2 * g * heads + h) for g in range(G)]
                + [spec(lambda h, g=g: (2 * g + 1) * heads + h) for g in range(G)])
    return pl.pallas_call(
        functools.partial(_attn_kernel, dilations=dil, seq=S),
        out_shape=jax.ShapeDtypeStruct((T, heads * D), BF16),
        grid=(B, heads),
        in_specs=in_specs,
        out_specs=pl.BlockSpec((S, D), lambda b, h: (b, h)),
        scratch_shapes=[pltpu.VMEM((G, S, D), F32), pltpu.VMEM((G, S, D), F32),
                        pltpu.VMEM((nblk, ATTN_BLOCK, 2 * ATTN_BLOCK), F32),
                        pltpu.VMEM((nblk, ATTN_BLOCK, 2 * ATTN_BLOCK), BF16)],
        compiler_params=_params("parallel", "parallel"),
        name="dilated_attention",
    )(*([q] * G), *([kv] * (2 * G)))


def _kv_rows_kernel(x_ref, o_ref, *, heads):
    o_ref[0] = x_ref[...].reshape(x_ref.shape[0], 2, heads, ATTN_HEAD_DIM)


def _kv_rows(kv, g, keep, B, S, heads):
    W2 = 2 * heads * ATTN_HEAD_DIM
    tm = min(keep, 256)
    first = (S - keep) // tm
    per_seq = S // tm
    return pl.pallas_call(
        functools.partial(_kv_rows_kernel, heads=heads),
        out_shape=jax.ShapeDtypeStruct((B, keep, 2, heads, ATTN_HEAD_DIM), F32),
        grid=(B, keep // tm),
        in_specs=[pl.BlockSpec((tm, W2), lambda b, i: (b * per_seq + first + i, g))],
        out_specs=pl.BlockSpec((1, tm, 2, heads, ATTN_HEAD_DIM), lambda b, i: (b, i, 0, 0, 0)),
        compiler_params=_params("parallel", "parallel"),
        name=f"kv_rows_g{g}",
    )(kv)


def _decode_attn_kernel(q_ref, kvn_ref, *refs, heads):
    D = ATTN_HEAD_DIM
    W = heads * D
    n_groups = len(refs) - 1
    buf_refs = refs[:n_groups]
    out_ref = refs[n_groups]
    scale = D ** -0.5
    for h in range(heads):
        outs, lses = [], []
        for g in range(n_groups):
            qh = q_ref[0, :, g * W + h * D:g * W + (h + 1) * D]
            k_new = kvn_ref[0, :, g * 2 * W + h * D:g * 2 * W + (h + 1) * D]
            v_new = kvn_ref[0, :, g * 2 * W + W + h * D:g * 2 * W + W + (h + 1) * D]
            k_buf = buf_refs[g][0, :, h * D:(h + 1) * D]
            v_buf = buf_refs[g][0, :, W + h * D:W + (h + 1) * D]
            s_buf = jnp.sum(k_buf * qh, axis=1, keepdims=True) * scale
            s_new = jnp.sum(k_new * qh, axis=1, keepdims=True) * scale
            mx = jnp.maximum(jnp.max(s_buf, axis=0, keepdims=True), s_new)
            e_buf = jnp.exp(s_buf - mx)
            e_new = jnp.exp(s_new - mx)
            l = jnp.sum(e_buf, axis=0, keepdims=True) + e_new
            outs.append(jnp.sum((e_buf / l) * v_buf, axis=0, keepdims=True) + (e_new / l) * v_new)
            lses.append(mx + jnp.log(l))
        mx = functools.reduce(jnp.maximum, lses)
        es = [jnp.exp(l - mx) for l in lses]
        tot = functools.reduce(lambda a, b: a + b, es)
        acc = (es[0] / tot) * outs[0]
        for g in range(1, n_groups):
            acc = acc + (es[g] / tot) * outs[g]
        out_ref[0, :, h * D:(h + 1) * D] = acc


def _decode_attention(q, kv_new, caches, heads):
    D = ATTN_HEAD_DIM
    W = heads * D
    Bd = caches[0].shape[0]
    steps = ATTN_BLOCK
    views = []
    for cache, (window, dilation) in zip(caches, DILATED_GROUPS):
        Lb = cache.shape[1]
        assert window // dilation == steps and Lb == steps * dilation
        views.append(cache.reshape(Bd, steps, dilation * 2 * W))
    return pl.pallas_call(
        functools.partial(_decode_attn_kernel, heads=heads),
        out_shape=jax.ShapeDtypeStruct((Bd, 1, W), F32),
        grid=(Bd,),
        in_specs=[pl.BlockSpec((1, 1, q.shape[-1]), lambda b: (b, 0, 0)),
                  pl.BlockSpec((1, 1, kv_new.shape[-1]), lambda b: (b, 0, 0))]
                 + [pl.BlockSpec((1, steps, 2 * W), lambda b: (b, 0, 0)) for _ in views],
        out_specs=pl.BlockSpec((1, 1, W), lambda b: (b, 0, 0)),
        compiler_params=_params("parallel"),
        name="decode_attention",
    )(q[:Bd].reshape(Bd, 1, -1), kv_new[:Bd].reshape(Bd, 1, -1), *views)


def _router_kernel(x_ref, wt_ref, b_ref, o_ref):
    logits = _dot_nt_3pass(x_ref[...], wt_ref[...]) + b_ref[...]
    e = jnp.exp(logits - jnp.max(logits, axis=1, keepdims=True))
    p = e / jnp.sum(e, axis=1, keepdims=True)
    E = p.shape[1]
    idx = lax.broadcasted_iota(I32, p.shape, 1)
    p1 = jnp.max(p, axis=1, keepdims=True)
    i1 = jnp.min(jnp.where(p == p1, idx, E), axis=1, keepdims=True)
    rest = jnp.where(idx == i1, -1.0, p)
    p2 = jnp.max(rest, axis=1, keepdims=True)
    i2 = jnp.min(jnp.where(rest == p2, idx, E), axis=1, keepdims=True)
    tot = p1 + p2
    o_ref[...] = jnp.where(idx == 0, i1.astype(F32),
                 jnp.where(idx == 1, i2.astype(F32),
                 jnp.where(idx == 2, p1 / tot,
                 jnp.where(idx == 3, p2 / tot, 0.0))))


def _router(xn, w_router_t, b_router):
    T, D = xn.shape
    E = w_router_t.shape[0]
    tm = min(T, 512)
    return pl.pallas_call(
        _router_kernel,
        out_shape=jax.ShapeDtypeStruct((T, E), F32),
        grid=(T // tm,),
        in_specs=[pl.BlockSpec((tm, D), lambda i: (i, 0)),
                  pl.BlockSpec((E, D), lambda i: (0, 0)),
                  pl.BlockSpec((1, E), lambda i: (0, 0))],
        out_specs=pl.BlockSpec((tm, E), lambda i: (i, 0)),
        compiler_params=_params("parallel"),
        name="router",
    )(xn, w_router_t, b_router.reshape(1, E))


def _route_plan(top_i, tm, n_tiles):
    T = top_i.shape[0]
    e_flat = top_i.reshape(-1)
    onehot = (e_flat[:, None] == jnp.arange(N_EXPERTS, dtype=I32)[None, :]).astype(I32)
    csum = jnp.cumsum(onehot, axis=0)
    rank = jnp.sum(onehot * csum, axis=1) - 1
    counts = csum[-1]
    gsz = ((counts + tm - 1) // tm) * tm
    gend = jnp.cumsum(gsz)
    gstart = gend - gsz
    pos = (jnp.sum(onehot * gstart[None, :], axis=1) + rank).astype(I32)
    src = jnp.zeros((n_tiles * tm,), I32).at[pos].set(jnp.arange(TOP_K * T, dtype=I32) // TOP_K)
    tile_start = jnp.arange(n_tiles, dtype=I32) * tm
    tile_valid = (tile_start < gend[-1]).astype(I32)
    tile_e = jnp.sum((tile_start[:, None] >= gend[None, :]).astype(I32), axis=1)
    last_e = jnp.sum((gend[-1] - 1 >= gend).astype(I32))
    tile_e = jnp.where(tile_valid > 0, tile_e, last_e).astype(I32)
    return pos, src, tile_e, tile_valid


def _experts_kernel(te_ref, tv_ref, src_ref, x_hbm, wg_ref, wu_ref, wd_ref, o_ref, xf_ref, xb_ref, sem,
                    *, tm, per_step):
    i = pl.program_id(0)
    c = pl.program_id(1)
    n_tiles = pl.num_programs(0)
    n_rows = per_step * pl.num_programs(1)
    valid = tv_ref[i] > 0
    nxt = jnp.minimum(i + 1, n_tiles - 1)
    prefetch = jnp.logical_and(i + 1 < n_tiles, tv_ref[nxt] > 0)
    D = o_ref.shape[1]
    dn = min(D, 512)

    def row_copy(tile, r):
        src = src_ref[tile * tm + jnp.minimum(r, tm - 1)]
        return pltpu.make_async_copy(x_hbm.at[pl.ds(src, 1)], xf_ref.at[pl.ds(r, 1)], sem.at[0])

    @pl.when(jnp.logical_and(i == 0, c == 0))
    def _():
        def start(r, carry):
            row_copy(0, r).start()
            return carry

        lax.fori_loop(0, n_rows, start, 0)

    @pl.when(c == 0)
    def _():
        o_ref[...] = jnp.zeros_like(o_ref)

    @pl.when(jnp.logical_and(valid, c == 0))
    def _():
        def wait(r, carry):
            row_copy(i, r).wait()
            return carry

        lax.fori_loop(0, n_rows, wait, 0)
        xb_ref[...] = xf_ref[pl.ds(0, tm), :].astype(BF16)

    def chunk(with_prefetch):
        if with_prefetch:
            for j in range(per_step):
                row_copy(i + 1, c * per_step + j).start()
        x = xb_ref[...]
        g = _dot(x, wg_ref[0].astype(BF16))
        u = _dot(x, wu_ref[0].astype(BF16))
        act = (g * _sigmoid(g) * u).astype(BF16)
        for n0 in range(0, D, dn):
            o_ref[:, n0:n0 + dn] += _dot(act, wd_ref[0, :, n0:n0 + dn].astype(BF16))

    pl.when(jnp.logical_and(valid, prefetch))(functools.partial(chunk, True))
    pl.when(jnp.logical_and(valid, jnp.logical_not(prefetch)))(functools.partial(chunk, False))


def _experts(xn, pos_plan, w_gate, w_up, w_down, tm, tf):
    _, src, tile_e, tile_valid = pos_plan
    T, D = xn.shape
    Fe = w_gate.shape[-1]
    nf = Fe // tf
    n_tiles = tile_e.shape[0]
    per_step = -(-tm // nf)

    def chunk(i, c, tv):
        return jnp.where(tv[i] > 0, c, nf - 1)

    return pl.pallas_call(
        functools.partial(_experts_kernel, tm=tm, per_step=per_step),
        out_shape=jax.ShapeDtypeStruct((n_tiles * tm, D), F32),
        grid_spec=pltpu.PrefetchScalarGridSpec(
            num_scalar_prefetch=3,
            grid=(n_tiles, nf),
            in_specs=[pl.BlockSpec(memory_space=pl.ANY),
                      pl.BlockSpec((1, D, tf), lambda i, c, te, tv, src: (te[i], 0, chunk(i, c, tv))),
                      pl.BlockSpec((1, D, tf), lambda i, c, te, tv, src: (te[i], 0, chunk(i, c, tv))),
                      pl.BlockSpec((1, tf, D), lambda i, c, te, tv, src: (te[i], chunk(i, c, tv), 0))],
            out_specs=pl.BlockSpec((tm, D), lambda i, c, te, tv, src: (i, 0)),
            scratch_shapes=[pltpu.VMEM((per_step * nf, D), F32), pltpu.VMEM((tm, D), BF16),
                            pltpu.SemaphoreType.DMA((1,))]),
        compiler_params=_params("arbitrary", "arbitrary"),
        name="experts",
    )(tile_e, tile_valid, src, xn, w_gate, w_up, w_down)


def _combine_kernel(pos_ref, h_ref, gate_ref, y_hbm, fg_ref, o_ref, ya_ref, yb_ref, sem, *, tm):
    i = pl.program_id(0)

    def copies(r):
        t = i * tm + r
        return (pltpu.make_async_copy(y_hbm.at[pl.ds(pos_ref[TOP_K * t], 1)],
                                      ya_ref.at[pl.ds(r, 1)], sem.at[0]),
                pltpu.make_async_copy(y_hbm.at[pl.ds(pos_ref[TOP_K * t + 1], 1)],
                                      yb_ref.at[pl.ds(r, 1)], sem.at[1]))

    def start(r, carry):
        a, b = copies(r)
        a.start()
        b.start()
        return carry

    def wait(r, carry):
        a, b = copies(r)
        a.wait()
        b.wait()
        return carry

    lax.fori_loop(0, tm, start, 0)
    lax.fori_loop(0, tm, wait, 0)
    hsum = h_ref[...] + (gate_ref[:, 0:1] * ya_ref[...] + gate_ref[:, 1:2] * yb_ref[...])
    o_ref[...] = _rms(hsum) * fg_ref[...]


def _combine(h, gates, y_sorted, pos, final_gain):
    T, D = h.shape
    tm = min(T, 256)
    return pl.pallas_call(
        functools.partial(_combine_kernel, tm=tm),
        out_shape=jax.ShapeDtypeStruct((T, D), F32),
        grid_spec=pltpu.PrefetchScalarGridSpec(
            num_scalar_prefetch=1,
            grid=(T // tm,),
            in_specs=[pl.BlockSpec((tm, D), lambda i, pos: (i, 0)),
                      pl.BlockSpec((tm, TOP_K), lambda i, pos: (i, 0)),
                      pl.BlockSpec(memory_space=pl.ANY),
                      pl.BlockSpec((1, D), lambda i, pos: (0, 0))],
            out_specs=pl.BlockSpec((tm, D), lambda i, pos: (i, 0)),
            scratch_shapes=[pltpu.VMEM((tm, D), F32), pltpu.VMEM((tm, D), F32),
                            pltpu.SemaphoreType.DMA((2,))]),
        compiler_params=_params("arbitrary"),
        name="moe_combine",
    )(pos, h, gates, y_sorted, final_gain)


def _trunk(x, mixer_state, kv_caches, B, S, p):
    T, D = x.shape
    H = MLSTM_HEADS
    hv = p["mlstm_out_norm"].shape[-1]
    hk = hv // 2
    heads = p["attn_w_out"].shape[1] // ATTN_HEAD_DIM
    W = heads * ATTN_HEAD_DIM
    n_groups = len(DILATED_GROUPS)

    w_in = p["mlstm_w_in"][0]
    (xn,) = _norm(x, p["norm_mix"][0:1], [BF16])
    proj = _matmul(xn, w_in, 2 * hk + 2 * hv, F32)
    gc, gr = _gates(x, p["norm_mix"][0:1], w_in[:, 2 * hk + 2 * hv:].T, p["mlstm_b_gates"][0])
    if mixer_state is None:
        hg, C, n, m = _mlstm_prompt(proj, gc, gr, p["mlstm_out_norm"], B, S)
        m = m[:, :, 0, 0]
    else:
        hg, C, n, m = _mlstm_decode(proj, gc, p["mlstm_out_norm"], *mixer_state)
        hg = jnp.zeros((T, hv), BF16).at[:B].set(hg[:, 0].astype(BF16))
        m = m[:, :, 0, 0]
    n = n[:, :, 0]
    h1 = _matmul(hg, p["mlstm_w_out"][0], D, F32, res=x)

    (xn,) = _norm(h1, p["norm_ffn"][0:1], [BF16])
    act = _glu(xn, p["ffn_w_gate"][0], p["ffn_w_up"][0])
    h2 = _matmul(act, p["ffn_w_down"][0], D, F32, res=h1, tn=512)

    xkv, xq = _norm(h2, jnp.concatenate([p["kv_norm"][None], p["norm_mix"][1:2]]), [BF16, BF16])
    kv = _matmul(xkv, p["w_kv"], n_groups * 2 * W, F32)
    q = _matmul(xq, p["attn_w_q"][0], n_groups * W, F32)
    if kv_caches is None:
        att = _attention(q, kv, B, S, heads)
        kv_rows = [_kv_rows(kv, g, min(window, S), B, S, heads)
                   for g, (window, _) in enumerate(DILATED_GROUPS)]
    else:
        att = _decode_attention(q, kv, kv_caches, heads)
        att = jnp.zeros((T, W), BF16).at[:B].set(att[:, 0].astype(BF16))
        kv5 = kv[:B].reshape(B, 1, n_groups, 2, heads, ATTN_HEAD_DIM)
        kv_rows = [kv5[:, :, g] for g in range(n_groups)]
    h3 =_matmul(att, p["attn_w_out"][0], D, F32, res=h2)

    (xn,) = _norm(h3, p["norm_ffn"][1:2], [F32])
    route = _router(xn, p["moe_w_router"][0].T, p["moe_b_router"][0])
    top_i = route[:, :TOP_K].astype(I32)
    gates = route[:, TOP_K:2 * TOP_K]
    tm_e = min(1024, T)
    n_tiles = (TOP_K * T) // tm_e + N_EXPERTS
    plan = _route_plan(top_i, tm_e, n_tiles)
    y_sorted = _experts(xn, plan, p["moe_w_gate"][0], p["moe_w_up"][0], p["moe_w_down"][0], tm_e, 256)
    y = _combine(h3, gates, y_sorted, plan[0], p["final_norm"][None])

    return y, C[None], n[None], m[None], kv_rows


def kernel(x_prompt, x_sample, state_mlstm_C, state_mlstm_n, state_mlstm_m, cache_kv_w128, cache_kv_w512, cache_kv_w2048, norm_mix, norm_ffn, mlstm_w_in, mlstm_b_gates, mlstm_out_norm, mlstm_w_out, kv_norm, w_kv, attn_w_q, attn_w_out, ffn_w_gate, ffn_w_up, ffn_w_down, moe_w_router, moe_b_router, moe_w_gate, moe_w_up, moe_w_down, final_norm):
    assert norm_mix.shape[0] == 2 and mlstm_w_in.shape[0] == 1 and attn_w_q.shape[0] == 1
    p = dict(norm_mix=norm_mix, norm_ffn=norm_ffn, mlstm_w_in=mlstm_w_in, mlstm_b_gates=mlstm_b_gates,
             mlstm_out_norm=mlstm_out_norm, mlstm_w_out=mlstm_w_out, kv_norm=kv_norm, w_kv=w_kv,
             attn_w_q=attn_w_q, attn_w_out=attn_w_out, ffn_w_gate=ffn_w_gate, ffn_w_up=ffn_w_up,
             ffn_w_down=ffn_w_down, moe_w_router=moe_w_router, moe_b_router=moe_b_router,
             moe_w_gate=moe_w_gate, moe_w_up=moe_w_up, moe_w_down=moe_w_down, final_norm=final_norm)
    B, S, D = x_prompt.shape
    Bd = x_sample.shape[0]
    assert x_sample.shape[1] == 1 and Bd <= SAMPLE_ROWS

    y_p, C_p, n_p, m_p, kv_rows_p = _trunk(x_prompt.reshape(B * S, D), None, None, B, S, p)

    xs = jnp.zeros((SAMPLE_ROWS, D), F32).at[:Bd].set(x_sample[:, 0])
    state = (state_mlstm_C[0], state_mlstm_n[0], state_mlstm_m[0])
    y_s, C_s, n_s, m_s, kv_rows_s = _trunk(xs, state, (cache_kv_w128, cache_kv_w512, cache_kv_w2048),
                                           Bd, 1, p)

    return (y_p.reshape(B, S, D), y_s[:Bd, None], C_p, n_p, m_p, *kv_rows_p,
            C_s, n_s, m_s, *kv_rows_s)
```

```python
import functools

import jax
import jax.numpy as jnp
from jax import lax
from jax.experimental import pallas as pl
from jax.experimental.pallas import tpu as pltpu

F32 = jnp.float32
BF16 = jnp.bfloat16
I32 = jnp.int32

EPS = 1e-6
GATE_SOFTCAP = 15.0
MLSTM_HEADS = 4
ATTN_HEAD_DIM = 128
DILATED_GROUPS = ((128, 1), (512, 4), (2048, 16))
N_EXPERTS = 8
TOP_K = 2
MLSTM_CHUNK = 256
ATTN_BLOCK = 128
SAMPLE_ROWS = 16

VMEM_LIMIT = 56 * 1024 * 1024


def _params(*sem):
    return pltpu.CompilerParams(dimension_semantics=sem, vmem_limit_bytes=VMEM_LIMIT)


def _dot(a, b):
    return jnp.dot(a, b, preferred_element_type=F32)


def _dot_nt(a, b):
    return lax.dot_general(a, b, (((1,), (1,)), ((), ())), preferred_element_type=F32)


def _dot_tn(a, b):
    return lax.dot_general(a, b, (((0,), (0,)), ((), ())), preferred_element_type=F32)


def _split(x):
    hi = x.astype(BF16)
    lo = (x - hi.astype(F32)).astype(BF16)
    return hi, lo


def _dot_nt_3pass(x, w):
    xh, xl = _split(x)
    wh, wl = _split(w)
    return _dot_nt(xh, wh) + (_dot_nt(xl, wh) + _dot_nt(xh, wl))


def _sigmoid(x):
    return 1.0 / (1.0 + jnp.exp(-x))


def _rms(x):
    return x * lax.rsqrt(jnp.mean(x * x, axis=-1, keepdims=True) + EPS)


def _norm_kernel(x_ref, g_ref, *o_refs):
    y = _rms(x_ref[...])
    for i, o_ref in enumerate(o_refs):
        o_ref[...] = (y * g_ref[i:i + 1, :]).astype(o_ref.dtype)


def _norm(x, gains, dtypes):
    T, D = x.shape
    tm = min(T, 512)
    return pl.pallas_call(
        _norm_kernel,
        out_shape=[jax.ShapeDtypeStruct((T, D), dt) for dt in dtypes],
        grid=(T // tm,),
        in_specs=[pl.BlockSpec((tm, D), lambda i: (i, 0)),
                  pl.BlockSpec(gains.shape, lambda i: (0, 0))],
        out_specs=[pl.BlockSpec((tm, D), lambda i: (i, 0)) for _ in dtypes],
        compiler_params=_params("parallel"),
        name="rmsnorm",
    )(x, gains)


def _row_tile(M, K):
    return min(M, 1024 if K <= 4096 else 512)


def _weight_spec(w, tn):
    if w.ndim == 3:
        return pl.BlockSpec((None, w.shape[1], tn), lambda j, i: (0, 0, j))
    return pl.BlockSpec((w.shape[0], tn), lambda j, i: (0, j))


def _mm_kernel(a_ref, w_ref, *rest, has_res):
    if has_res:
        r_ref, o_ref, wb_ref = rest
    else:
        o_ref, wb_ref = rest

    @pl.when(pl.program_id(1) == 0)
    def _():
        wb_ref[...] = w_ref[...].astype(BF16)

    acc = _dot(a_ref[...], wb_ref[...])
    if has_res:
        acc = r_ref[...] + acc
    o_ref[...] = acc.astype(o_ref.dtype)


def _matmul(a, w, n_cols, out_dtype, res=None, tn=1024):
    M, K = a.shape
    tm = _row_tile(M, K)
    tn = min(tn, n_cols)
    in_specs = [pl.BlockSpec((tm, K), lambda j, i: (i, 0)), _weight_spec(w, tn)]
    args = [a, w]
    if res is not None:
        in_specs.append(pl.BlockSpec((tm, tn), lambda j, i: (i, j)))
        args.append(res)
    return pl.pallas_call(
        functools.partial(_mm_kernel, has_res=res is not None),
        out_shape=jax.ShapeDtypeStruct((M, n_cols), out_dtype),
        grid=(n_cols // tn, M // tm),
        in_specs=in_specs,
        out_specs=pl.BlockSpec((tm, tn), lambda j, i: (i, j)),
        scratch_shapes=[pltpu.VMEM((K, tn), BF16)],
        compiler_params=_params("arbitrary", "arbitrary"),
        name="matmul",
    )(*args)


def _glu_kernel(a_ref, wg_ref, wu_ref, o_ref, wgb_ref, wub_ref):
    @pl.when(pl.program_id(1) == 0)
    def _():
        wgb_ref[...] = wg_ref[...].astype(BF16)
        wub_ref[...] = wu_ref[...].astype(BF16)

    a = a_ref[...]
    g = _dot(a, wgb_ref[...])
    u = _dot(a, wub_ref[...])
    o_ref[...] = (g * _sigmoid(g) * u).astype(o_ref.dtype)


def _glu(a, wg, wu, tf=512):
    M, K = a.shape
    F = wg.shape[-1]
    tm = _row_tile(M, K)
    return pl.pallas_call(
        _glu_kernel,
        out_shape=jax.ShapeDtypeStruct((M, F), BF16),
        grid=(F // tf, M // tm),
        in_specs=[pl.BlockSpec((tm, K), lambda j, i: (i, 0)), _weight_spec(wg, tf), _weight_spec(wu, tf)],
        out_specs=pl.BlockSpec((tm, tf), lambda j, i: (i, j)),
        scratch_shapes=[pltpu.VMEM((K, tf), BF16), pltpu.VMEM((K, tf), BF16)],
        compiler_params=_params("arbitrary", "arbitrary"),
        name="swiglu_up",
    )(a, wg, wu)


def _gates_kernel(x_ref, g_ref, wt_ref, bc_ref, br_ref, gc_ref, gr_ref):
    H = MLSTM_HEADS
    xn = _rms(x_ref[...]) * g_ref[...]
    wt = wt_ref[...]

    def finish(pre, head_axis):
        t = GATE_SOFTCAP * jnp.tanh(pre / GATE_SOFTCAP)
        ls = jnp.minimum(t, 0.0) - jnp.log1p(jnp.exp(-jnp.abs(t)))
        idx = lax.broadcasted_iota(I32, t.shape, head_axis)
        return jnp.where(idx < H, t, ls)

    gc_ref[...] = finish(_dot_nt_3pass(xn, wt) + bc_ref[...], 1)
    gr_ref[...] = finish(_dot_nt_3pass(wt, xn) + br_ref[...], 0)


def _gates(x, gain, w_gate_t, b_gates):
    T, D = x.shape
    G = w_gate_t.shape[0]
    tm = min(T, 512)
    return pl.pallas_call(
        _gates_kernel,
        out_shape=[jax.ShapeDtypeStruct((T, G), F32), jax.ShapeDtypeStruct((G, T), F32)],
        grid=(T // tm,),
        in_specs=[pl.BlockSpec((tm, D), lambda i: (i, 0)),
                  pl.BlockSpec((1, D), lambda i: (0, 0)),
                  pl.BlockSpec((G, D), lambda i: (0, 0)),
                  pl.BlockSpec((1, G), lambda i: (0, 0)),
                  pl.BlockSpec((G, 1), lambda i: (0, 0))],
        out_specs=[pl.BlockSpec((tm, G), lambda i: (i, 0)),
                   pl.BlockSpec((G, tm), lambda i: (0, i))],
        compiler_params=_params("parallel"),
        name="mlstm_gates",
    )(x, gain, w_gate_t, b_gates.reshape(1, G), b_gates.reshape(G, 1))


def _mlstm_kernel(q_ref, k_ref, v_ref, o_ref, gc_ref, gr_ref, on_ref,
                  h_ref, c_out, n_out, m_out, c_s, n_s, m_s, *, dk, dv):
    H = MLSTM_HEADS
    c = pl.program_id(1)
    L = q_ref.shape[0]

    @pl.when(c == 0)
    def _():
        c_s[...] = jnp.zeros_like(c_s)
        n_s[...] = jnp.zeros_like(n_s)
        m_s[...] = jnp.zeros_like(m_s)

    row = lax.broadcasted_iota(I32, (L, L), 0)
    col = lax.broadcasted_iota(I32, (L, L), 1)
    tril = row >= col
    for h in range(H):
        q = q_ref[:, h * dk:(h + 1) * dk]
        k = k_ref[:, h * dk:(h + 1) * dk] * (dk ** -0.5)
        v = v_ref[:, h * dv:(h + 1) * dv]
        ig_c = gc_ref[:, h:h + 1]
        lf_c = gc_ref[:, H + h:H + h + 1]
        ig_r = gr_ref[h:h + 1, :]
        lf_r = gr_ref[H + h:H + h + 1, :]
        b_c = jnp.sum(jnp.where(tril, lf_r, 0.0), axis=1, keepdims=True)
        b_r = jnp.sum(jnp.where(row <= col, lf_c, 0.0), axis=0, keepdims=True)
        m_prev = m_s[h][:, 0:1]
        C = c_s[h]
        n = n_s[h]
        dmat = jnp.where(tril, b_c - b_r + ig_r, -jnp.inf)
        inter = b_c + m_prev
        m_t = jnp.maximum(inter, jnp.max(dmat, axis=1, keepdims=True))
        w_intra = jnp.exp(dmat - m_t)
        w_inter = jnp.exp(inter - m_t)
        qb = q.astype(BF16)
        vb = v.astype(BF16)
        s = _dot_nt(qb, k.astype(BF16)) * w_intra
        num = w_inter * _dot(qb, C.astype(BF16)) + _dot(s.astype(BF16), vb)
        den = w_inter * jnp.sum(q * n, axis=1, keepdims=True) + jnp.sum(s, axis=1, keepdims=True)
        hh = num / jnp.maximum(jnp.abs(den), jnp.exp(-m_t))
        b_last = b_c[L - 1:L, :]
        g_r = b_last - b_r + ig_r
        g_c = b_last - b_c + ig_c
        m_new = jnp.maximum(b_last + m_prev, jnp.max(g_r, axis=1, keepdims=True))
        decay = jnp.exp(b_last + m_prev - m_new)
        kw = k * jnp.exp(g_c - m_new)
        c_s[h] = decay * C + _dot_tn(kw.astype(BF16), vb)
        n_s[h] = decay * n + jnp.sum(kw, axis=0, keepdims=True)
        m_s[h] = jnp.broadcast_to(m_new, m_s.shape[1:])
        sl = slice(h * dv, (h + 1) * dv)
        h_ref[:, sl] = (_rms(hh) * on_ref[:, sl] * _sigmoid(o_ref[:, sl])).astype(h_ref.dtype)

    @pl.when(c == pl.num_programs(1) - 1)
    def _():
        c_out[0] = c_s[...]
        n_out[0] = n_s[...]
        m_out[0] = m_s[...]


def _mlstm_prompt(proj, gc, gr, out_norm, B, S):
    H = MLSTM_HEADS
    T = B * S
    hv = out_norm.shape[1]
    dv = hv // H
    dk = dv // 2
    hk = H * dk
    L = MLSTM_CHUNK
    nc = S // L
    G = gc.shape[1]
    rows = lambda b, c: b * nc + c
    return pl.pallas_call(
        functools.partial(_mlstm_kernel, dk=dk, dv=dv),
        out_shape=[jax.ShapeDtypeStruct((T, hv), BF16),
                   jax.ShapeDtypeStruct((B, H, dk, dv), F32),
                   jax.ShapeDtypeStruct((B, H, 1, dk), F32),
                   jax.ShapeDtypeStruct((B, H, 1, 128), F32)],
        grid=(B, nc),
        in_specs=[pl.BlockSpec((L, hk), lambda b, c: (rows(b, c), 0)),
                  pl.BlockSpec((L, hk), lambda b, c: (rows(b, c), 1)),
                  pl.BlockSpec((L, hv), lambda b, c: (rows(b, c), 1)),
                  pl.BlockSpec((L, hv), lambda b, c: (rows(b, c), 2)),
                  pl.BlockSpec((L, G), lambda b, c: (rows(b, c), 0)),
                  pl.BlockSpec((G, L), lambda b, c: (0, rows(b, c))),
                  pl.BlockSpec((1, hv), lambda b, c: (0, 0))],
        out_specs=[pl.BlockSpec((L, hv), lambda b, c: (rows(b, c), 0)),
                   pl.BlockSpec((1, H, dk, dv), lambda b, c: (b, 0, 0, 0)),
                   pl.BlockSpec((1, H, 1, dk), lambda b, c: (b, 0, 0, 0)),
                   pl.BlockSpec((1, H, 1, 128), lambda b, c: (b, 0, 0, 0))],
        scratch_shapes=[pltpu.VMEM((H, dk, dv), F32), pltpu.VMEM((H, 1, dk), F32),
                        pltpu.VMEM((H, 1, 128), F32)],
        compiler_params=_params("arbitrary", "arbitrary"),
        name="mlstm_prompt",
    )(proj, proj, proj, proj, gc, gr, out_norm)


def _mlstm_step_kernel(qr_ref, qc_ref, kr_ref, kc_ref, v_ref, o_ref, ig_ref, lf_ref, on_ref,
                       c_ref, n_ref, m_ref, h_ref, c_out, n_out, m_out):
    H = MLSTM_HEADS
    dk = qr_ref.shape[-1]
    dv = v_ref.shape[-1]
    for h in range(H):
        q_r = qr_ref[0, h]
        q_c = qc_ref[0, h]
        k_r = kr_ref[0, h] * (dk ** -0.5)
        k_c = kc_ref[0, h] * (dk ** -0.5)
        v = v_ref[0, h]
        ig = ig_ref[0, h]
        lf = lf_ref[0, h]
        C = c_ref[0, h]
        n = n_ref[0, h]
        m = m_ref[0, h]
        inter = lf + m
        m_t = jnp.maximum(inter, ig)
        w_intra = jnp.exp(ig - m_t)
        w_inter = jnp.exp(inter - m_t)
        s = jnp.sum(q_r * k_r, axis=1, keepdims=True) * w_intra
        num = w_inter * jnp.sum(q_c * C, axis=0, keepdims=True) + s * v
        den = w_inter * jnp.sum(q_r * n, axis=1, keepdims=True) + s
        hh = num / jnp.maximum(jnp.abs(den), jnp.exp(-m_t))
        decay = jnp.exp(inter - m_t)
        w_k = jnp.exp(ig - m_t)
        c_out[0, h] = decay * C + (k_c * w_k) * v
        n_out[0, h] = decay * n + k_r * w_k
        m_out[0, h] = m_t
        sl = slice(h * dv, (h + 1) * dv)
        h_ref[0, :, sl] = _rms(hh) * on_ref[:, sl] * _sigmoid(o_ref[0, :, sl])


def _mlstm_decode(proj, gc, out_norm, C0, n0, m0):
    H = MLSTM_HEADS
    Bd, _, dk, dv = C0.shape
    hk, hv = H * dk, H * dv
    q = proj[:Bd, :hk].reshape(Bd, H, dk)
    k = proj[:Bd, hk:2 * hk].reshape(Bd, H, dk)
    v = proj[:Bd, 2 * hk:2 * hk + hv].reshape(Bd, H, 1, dv)
    o = proj[:Bd, 2 * hk + hv:2 * hk + 2 * hv].reshape(Bd, 1, hv)
    ig = gc[:Bd, :H].reshape(Bd, H, 1, 1)
    lf = gc[:Bd, H:].reshape(Bd, H, 1, 1)
    spec = lambda *tail: pl.BlockSpec((1, H) + tail, lambda b: (b, 0, 0, 0))
    return pl.pallas_call(
        _mlstm_step_kernel,
        out_shape=[jax.ShapeDtypeStruct((Bd, 1, hv), F32),
                   jax.ShapeDtypeStruct((Bd, H, dk, dv), F32),
                   jax.ShapeDtypeStruct((Bd, H, 1, dk), F32),
                   jax.ShapeDtypeStruct((Bd, H, 1, 1), F32)],
        grid=(Bd,),
        in_specs=[spec(1, dk), spec(dk, 1), spec(1, dk), spec(dk, 1), spec(1, dv),
                  pl.BlockSpec((1, 1, hv), lambda b: (b, 0, 0)),
                  spec(1, 1), spec(1, 1),
                  pl.BlockSpec((1, hv), lambda b: (0, 0)),
                  spec(dk, dv), spec(1, dk), spec(1, 1)],
        out_specs=[pl.BlockSpec((1, 1, hv), lambda b: (b, 0, 0)),
                   spec(dk, dv), spec(1, dk), spec(1, 1)],
        compiler_params=_params("parallel"),
        name="mlstm_decode",
    )(q.reshape(Bd, H, 1, dk), q.reshape(Bd, H, dk, 1), k.reshape(Bd, H, 1, dk),
      k.reshape(Bd, H, dk, 1), v, o, ig, lf, out_norm, C0, n0.reshape(Bd, H, 1, dk),
      m0.reshape(Bd, H, 1, 1))


def _rows(start, size, stride):
    return pl.ds(start, size) if stride == 1 else pl.ds(start, size, stride=stride)


def _attn_kernel(*refs, dilations, seq):
    G = len(dilations)
    q_refs, k_refs, v_refs = refs[:G], refs[G:2 * G], refs[2 * G:3 * G]
    out_ref = refs[3 * G]
    o_scr, lse_scr, s_scr, p_scr = refs[3 * G + 1:]
    bq = ATTN_BLOCK
    D = ATTN_HEAD_DIM
    scale = D ** -0.5
    row = lax.broadcasted_iota(I32, (bq, 2 * bq), 0)
    col = lax.broadcasted_iota(I32, (bq, 2 * bq), 1)
    bias_full = jnp.where(jnp.logical_and(col - row >= 0, col - row <= bq), 0.0, -jnp.inf)
    row1 = lax.broadcasted_iota(I32, (bq, bq), 0)
    col1 = lax.broadcasted_iota(I32, (bq, bq), 1)
    bias_first = jnp.where(row1 >= col1, 0.0, -jnp.inf)
    for g, r in enumerate(dilations):
        nb = seq // (r * bq)
        blocks = [(rho, u) for rho in range(r) for u in range(nb)]
        for i, (rho, u) in enumerate(blocks):
            q = q_refs[g][_rows(rho + u * bq * r, bq, r), :].astype(BF16)
            if u == 0:
                k = k_refs[g][_rows(rho, bq, r), :].astype(BF16)
                s_scr[i, :, :bq] = _dot_nt(q, k) * scale + bias_first
            else:
                k = k_refs[g][_rows(rho + (u - 1) * bq * r, 2 * bq, r), :].astype(BF16)
                s_scr[i] = _dot_nt(q, k) * scale + bias_full
        for i, (rho, u) in enumerate(blocks):
            s = s_scr[i, :, :bq] if u == 0 else s_scr[i]
            mx = jnp.max(s, axis=1, keepdims=True)
            e = jnp.exp(s - mx)
            l = jnp.sum(e, axis=1, keepdims=True)
            p = (e / l).astype(BF16)
            if u == 0:
                p_scr[i, :, :bq] = p
            else:
                p_scr[i] = p
            lse_scr[g, _rows(rho + u * bq * r, bq, r), :] = jnp.broadcast_to(mx + jnp.log(l), (bq, D))
        for i, (rho, u) in enumerate(blocks):
            if u == 0:
                v = v_refs[g][_rows(rho, bq, r), :].astype(BF16)
                o = _dot(p_scr[i, :, :bq], v)
            else:
                v = v_refs[g][_rows(rho + (u - 1) * bq * r, 2 * bq, r), :].astype(BF16)
                o = _dot(p_scr[i], v)
            o_scr[g, _rows(rho + u * bq * r, bq, r), :] = o
    lses = [lse_scr[g] for g in range(G)]
    mx = functools.reduce(jnp.maximum, lses)
    es = [jnp.exp(l - mx) for l in lses]
    tot = functools.reduce(lambda a, b: a + b, es)
    acc = (es[0] / tot) * o_scr[0]
    for g in range(1, G):
        acc = acc + (es[g] / tot) * o_scr[g]
    out_ref[...] = acc.astype(out_ref.dtype)


def _attention(q, kv, B, S, heads):
    D = ATTN_HEAD_DIM
    T = B * S
    G = len(DILATED_GROUPS)
    dil = tuple(d for _, d in DILATED_GROUPS)
    assert all(w // d == ATTN_BLOCK and S % (d * ATTN_BLOCK) == 0 for w, d in DILATED_GROUPS)
    nblk = S // ATTN_BLOCK
    spec = lambda col: pl.BlockSpec((S, D), lambda b, h: (b, col(h)))
    in_specs = ([spec(lambda h, g=g: g * heads + h) for g in range(G)]
                + [spec(lambda h, g=g: 2 * g * heads + h) for g in range(G)]
                + [spec(lambda h, g=g: (2 * g + 1) * heads + h) for g in range(G)])
    return pl.pallas_call(
        functools.partial(_attn_kernel, dilations=dil, seq=S),
        out_shape=jax.ShapeDtypeStruct((T, heads * D), BF16),
        grid=(B, heads),
        in_specs=in_specs,
        out_specs=pl.BlockSpec((S, D), lambda b, h: (b, h)),
        scratch_shapes=[pltpu.VMEM((G, S, D), F32), pltpu.VMEM((G, S, D), F32),
                        pltpu.VMEM((nblk, ATTN_BLOCK, 2 * ATTN_BLOCK), F32),
                        pltpu.VMEM((nblk, ATTN_BLOCK, 2 * ATTN_BLOCK), BF16)],
        compiler_params=_params("parallel", "parallel"),
        name="dilated_attention",
    )(*([q] * G), *([kv] * (2 * G)))


def _kv_rows_kernel(x_ref, o_ref, *, heads):
    o_ref[0] = x_ref[...].reshape(x_ref.shape[0], 2, heads, ATTN_HEAD_DIM)


def _kv_rows(kv, g, keep, B, S, heads):
    W2 = 2 * heads * ATTN_HEAD_DIM
    tm = min(keep, 256)
    first = (S - keep) // tm
    per_seq = S // tm
    return pl.pallas_call(
        functools.partial(_kv_rows_kernel, heads=heads),
        out_shape=jax.ShapeDtypeStruct((B, keep, 2, heads, ATTN_HEAD_DIM), F32),
        grid=(B, keep // tm),
        in_specs=[pl.BlockSpec((tm, W2), lambda b, i: (b * per_seq + first + i, g))],
        out_specs=pl.BlockSpec((1, tm, 2, heads, ATTN_HEAD_DIM), lambda b, i: (b, i, 0, 0, 0)),
        compiler_params=_params("parallel", "parallel"),
        name=f"kv_rows_g{g}",
    )(kv)


def _decode_attn_kernel(q_ref, kvn_ref, *refs):
    G = len(refs) - 1
    buf_refs = refs[:G]
    out_ref = refs[G]
    scale = ATTN_HEAD_DIM ** -0.5
    outs, lses = [], []
    for g in range(G):
        q = q_ref[g]
        k_new = kvn_ref[g, 0]
        v_new = kvn_ref[g, 1]
        k_buf = buf_refs[g][:, 0]
        v_buf = buf_refs[g][:, 1]
        s_buf = jnp.sum(k_buf * q[None], axis=-1, keepdims=True) * scale
        s_new = jnp.sum(k_new * q, axis=-1, keepdims=True) * scale
        mx = jnp.maximum(jnp.max(s_buf, axis=0), s_new)
        e_buf = jnp.exp(s_buf - mx[None])
        e_new = jnp.exp(s_new - mx)
        l = jnp.sum(e_buf, axis=0) + e_new
        outs.append(jnp.sum((e_buf / l[None]) * v_buf, axis=0) + (e_new / l) * v_new)
        lses.append(mx + jnp.log(l))
    mx = functools.reduce(jnp.maximum, lses)
    es = [jnp.exp(l - mx) for l in lses]
    tot = functools.reduce(lambda a, b: a + b, es)
    acc = (es[0] / tot) * outs[0]
    for g in range(1, G):
        acc = acc + (es[g] / tot) * outs[g]
    out_ref[...] = acc


def _decode_attention(q, kv_new, caches):
    Bd, G, H, D = q.shape
    steps = ATTN_BLOCK
    views = []
    for cache, (window, dilation) in zip(caches, DILATED_GROUPS):
        Lb = cache.shape[1]
        assert window // dilation == steps and Lb == steps * dilation
        views.append(cache.reshape(Bd, steps, dilation, 2, H, D))
    return pl.pallas_call(
        _decode_attn_kernel,
        out_shape=jax.ShapeDtypeStruct((Bd, H, D), F32),
        grid=(Bd,),
        in_specs=[pl.BlockSpec((None, G, H, D), lambda b: (b, 0, 0, 0)),
                  pl.BlockSpec((None, G, 2, H, D), lambda b: (b, 0, 0, 0, 0))]
                 + [pl.BlockSpec((None, steps, None, 2, H, D), lambda b: (b, 0, 0, 0, 0, 0)) for _ in views],
        out_specs=pl.BlockSpec((None, H, D), lambda b: (b, 0, 0)),
        compiler_params=_params("parallel"),
        name="decode_attention",
    )(q, kv_new, *views)


def _router_kernel(x_ref, wt_ref, b_ref, o_ref):
    logits = _dot_nt_3pass(x_ref[...], wt_ref[...]) + b_ref[...]
    e = jnp.exp(logits - jnp.max(logits, axis=1, keepdims=True))
    p = e / jnp.sum(e, axis=1, keepdims=True)
    E = p.shape[1]
    idx = lax.broadcasted_iota(I32, p.shape, 1)
    p1 = jnp.max(p, axis=1, keepdims=True)
    i1 = jnp.min(jnp.where(p == p1, idx, E), axis=1, keepdims=True)
    rest = jnp.where(idx == i1, -1.0, p)
    p2 = jnp.max(rest, axis=1, keepdims=True)
    i2 = jnp.min(jnp.where(rest == p2, idx, E), axis=1, keepdims=True)
    tot = p1 + p2
    o_ref[...] = jnp.where(idx == 0, i1.astype(F32),
                 jnp.where(idx == 1, i2.astype(F32),
                 jnp.where(idx == 2, p1 / tot,
                 jnp.where(idx == 3, p2 / tot, 0.0))))


def _router(xn, w_router_t, b_router):
    T, D = xn.shape
    E = w_router_t.shape[0]
    tm = min(T, 512)
    return pl.pallas_call(
        _router_kernel,
        out_shape=jax.ShapeDtypeStruct((T, E), F32),
        grid=(T // tm,),
        in_specs=[pl.BlockSpec((tm, D), lambda i: (i, 0)),
                  pl.BlockSpec((E, D), lambda i: (0, 0)),
                  pl.BlockSpec((1, E), lambda i: (0, 0))],
        out_specs=pl.BlockSpec((tm, E), lambda i: (i, 0)),
        compiler_params=_params("parallel"),
        name="router",
    )(xn, w_router_t, b_router.reshape(1, E))


def _route_plan(top_i, tm, n_tiles):
    T = top_i.shape[0]
    e_flat = top_i.reshape(-1)
    onehot = (e_flat[:, None] == jnp.arange(N_EXPERTS, dtype=I32)[None, :]).astype(I32)
    csum = jnp.cumsum(onehot, axis=0)
    rank = jnp.sum(onehot * csum, axis=1) - 1
    counts = csum[-1]
    gsz = ((counts + tm - 1) // tm) * tm
    gend = jnp.cumsum(gsz)
    gstart = gend - gsz
    pos = (jnp.sum(onehot * gstart[None, :], axis=1) + rank).astype(I32)
    src = jnp.zeros((n_tiles * tm,), I32).at[pos].set(jnp.arange(TOP_K * T, dtype=I32) // TOP_K)
    tile_start = jnp.arange(n_tiles, dtype=I32) * tm
    tile_valid = (tile_start < gend[-1]).astype(I32)
    tile_e = jnp.sum((tile_start[:, None] >= gend[None, :]).astype(I32), axis=1)
    last_e = jnp.sum((gend[-1] - 1 >= gend).astype(I32))
    tile_e = jnp.where(tile_valid > 0, tile_e, last_e).astype(I32)
    return pos, src, tile_e, tile_valid


def _experts_kernel(te_ref, tv_ref, src_ref, x_hbm, wg_ref, wu_ref, wd_ref, o_ref, xf_ref, xb_ref, sem,
                    *, tm, per_step):
    i = pl.program_id(0)
    c = pl.program_id(1)
    n_tiles = pl.num_programs(0)
    n_rows = per_step * pl.num_programs(1)
    valid = tv_ref[i] > 0
    nxt = jnp.minimum(i + 1, n_tiles - 1)
    prefetch = jnp.logical_and(i + 1 < n_tiles, tv_ref[nxt] > 0)
    D = o_ref.shape[1]
    dn = min(D, 512)

    def row_copy(tile, r):
        src = src_ref[tile * tm + jnp.minimum(r, tm - 1)]
        return pltpu.make_async_copy(x_hbm.at[pl.ds(src, 1)], xf_ref.at[pl.ds(r, 1)], sem.at[0])

    @pl.when(jnp.logical_and(i == 0, c == 0))
    def _():
        def start(r, carry):
            row_copy(0, r).start()
            return carry

        lax.fori_loop(0, n_rows, start, 0)

    @pl.when(c == 0)
    def _():
        o_ref[...] = jnp.zeros_like(o_ref)

    @pl.when(jnp.logical_and(valid, c == 0))
    def _():
        def wait(r, carry):
            row_copy(i, r).wait()
            return carry

        lax.fori_loop(0, n_rows, wait, 0)
        xb_ref[...] = xf_ref[pl.ds(0, tm), :].astype(BF16)

    def chunk(with_prefetch):
        if with_prefetch:
            for j in range(per_step):
                row_copy(i + 1, c * per_step + j).start()
        x = xb_ref[...]
        g = _dot(x, wg_ref[0].astype(BF16))
        u = _dot(x, wu_ref[0].astype(BF16))
        act = (g * _sigmoid(g) * u).astype(BF16)
        for n0 in range(0, D, dn):
            o_ref[:, n0:n0 + dn] += _dot(act, wd_ref[0, :, n0:n0 + dn].astype(BF16))

    pl.when(jnp.logical_and(valid, prefetch))(functools.partial(chunk, True))
    pl.when(jnp.logical_and(valid, jnp.logical_not(prefetch)))(functools.partial(chunk, False))


def _experts(xn, pos_plan, w_gate, w_up, w_down, tm, tf):
    _, src, tile_e, tile_valid = pos_plan
    T, D = xn.shape
    Fe = w_gate.shape[-1]
    nf = Fe // tf
    n_tiles = tile_e.shape[0]
    per_step = -(-tm // nf)

    def chunk(i, c, tv):
        return jnp.where(tv[i] > 0, c, nf - 1)

    return pl.pallas_call(
        functools.partial(_experts_kernel, tm=tm, per_step=per_step),
        out_shape=jax.ShapeDtypeStruct((n_tiles * tm, D), F32),
        grid_spec=pltpu.PrefetchScalarGridSpec(
            num_scalar_prefetch=3,
            grid=(n_tiles, nf),
            in_specs=[pl.BlockSpec(memory_space=pl.ANY),
                      pl.BlockSpec((None, 1, D, tf), lambda i, c, te, tv, src: (0, te[i], 0, chunk(i, c, tv))),
                      pl.BlockSpec((None, 1, D, tf), lambda i, c, te, tv, src: (0, te[i], 0, chunk(i, c, tv))),
                      pl.BlockSpec((None, 1, tf, D), lambda i, c, te, tv, src: (0, te[i], chunk(i, c, tv), 0))],
            out_specs=pl.BlockSpec((tm, D), lambda i, c, te, tv, src: (i, 0)),
            scratch_shapes=[pltpu.VMEM((per_step * nf, D), F32), pltpu.VMEM((tm, D), BF16),
                            pltpu.SemaphoreType.DMA((1,))]),
        compiler_params=_params("arbitrary", "arbitrary"),
        name="experts",
    )(tile_e, tile_valid, src, xn, w_gate, w_up, w_down)


def _combine_kernel(pos_ref, h_ref, gate_ref, y_hbm, fg_ref, o_ref, ya_ref, yb_ref, sem, *, tm):
    i = pl.program_id(0)

    def copies(r):
        t = i * tm + r
        return (pltpu.make_async_copy(y_hbm.at[pl.ds(pos_ref[TOP_K * t], 1)],
                                      ya_ref.at[pl.ds(r, 1)], sem.at[0]),
                pltpu.make_async_copy(y_hbm.at[pl.ds(pos_ref[TOP_K * t + 1], 1)],
                                      yb_ref.at[pl.ds(r, 1)], sem.at[1]))

    def start(r, carry):
        a, b = copies(r)
        a.start()
        b.start()
        return carry

    def wait(r, carry):
        a, b = copies(r)
        a.wait()
        b.wait()
        return carry

    lax.fori_loop(0, tm, start, 0)
    lax.fori_loop(0, tm, wait, 0)
    hsum = h_ref[...] + (gate_ref[:, 0:1] * ya_ref[...] + gate_ref[:, 1:2] * yb_ref[...])
    o_ref[...] = _rms(hsum) * fg_ref[...]


def _combine(h, gates, y_sorted, pos, final_gain):
    T, D = h.shape
    tm = min(T, 256)
    return pl.pallas_call(
        functools.partial(_combine_kernel, tm=tm),
        out_shape=jax.ShapeDtypeStruct((T, D), F32),
        grid_spec=pltpu.PrefetchScalarGridSpec(
            num_scalar_prefetch=1,
            grid=(T // tm,),
            in_specs=[pl.BlockSpec((tm, D), lambda i, pos: (i, 0)),
                      pl.BlockSpec((tm, TOP_K), lambda i, pos: (i, 0)),
                      pl.BlockSpec(memory_space=pl.ANY),
                      pl.BlockSpec((1, D), lambda i, pos: (0, 0))],
            out_specs=pl.BlockSpec((tm, D), lambda i, pos: (i, 0)),
            scratch_shapes=[pltpu.VMEM((tm, D), F32), pltpu.VMEM((tm, D), F32),
                            pltpu.SemaphoreType.DMA((2,))]),
        compiler_params=_params("arbitrary"),
        name="moe_combine",
    )(pos, h, gates, y_sorted, final_gain)


def _trunk(x, mixer_state, kv_caches, B, S, p):
    T, D = x.shape
    H = MLSTM_HEADS
    hv = p["mlstm_out_norm"].shape[-1]
    hk = hv // 2
    heads = p["attn_w_out"].shape[1] // ATTN_HEAD_DIM
    W = heads * ATTN_HEAD_DIM
    n_groups = len(DILATED_GROUPS)

    (xn,) = _norm(x, p["norm_mix"][0:1], [BF16])
    proj = _matmul(xn, p["mlstm_w_in"], 2 * hk + 2 * hv, F32)
    gc, gr = _gates(x, p["norm_mix"][0:1], p["mlstm_w_in"][0, :, 2 * hk + 2 * hv:].T, p["mlstm_b_gates"][0])
    if mixer_state is None:
        hg, C, n, m = _mlstm_prompt(proj, gc, gr, p["mlstm_out_norm"], B, S)
        m = m[:, :, 0, 0]
    else:
        hg, C, n, m = _mlstm_decode(proj, gc, p["mlstm_out_norm"], *mixer_state)
        hg = jnp.zeros((T, hv), BF16).at[:B].set(hg[:, 0].astype(BF16))
        m = m[:, :, 0, 0]
    n = n[:, :, 0]
    h1 = _matmul(hg, p["mlstm_w_out"], D, F32, res=x)

    (xn,) = _norm(h1, p["norm_ffn"][0:1], [BF16])
    act = _glu(xn, p["ffn_w_gate"], p["ffn_w_up"])
    h2 = _matmul(act, p["ffn_w_down"], D, F32, res=h1, tn=512)

    xkv, xq = _norm(h2, jnp.concatenate([p["kv_norm"][None], p["norm_mix"][1:2]]), [BF16, BF16])
    kv = _matmul(xkv, p["w_kv"], n_groups * 2 * W, F32)
    q = _matmul(xq, p["attn_w_q"], n_groups * W, F32)
    if kv_caches is None:
        att = _attention(q, kv, B, S, heads)
        kv_rows = [_kv_rows(kv, g, min(window, S), B, S, heads)
                   for g, (window, _) in enumerate(DILATED_GROUPS)]
    else:
        kv5 = kv[:B].reshape(B, n_groups, 2, heads, ATTN_HEAD_DIM)
        att = _decode_attention(q[:B].reshape(B, n_groups, heads, ATTN_HEAD_DIM), kv5, kv_caches)
        att = jnp.zeros((T, W), BF16).at[:B].set(att.reshape(B, W).astype(BF16))
        kv_rows = [kv5[:, None, g] for g in range(n_groups)]
    h3 = _matmul(att, p["attn_w_out"], D, F32, res=h2)

    (xn,) = _norm(h3, p["norm_ffn"][1:2], [F32])
    route = _router(xn, p["moe_w_router"][0].T, p["moe_b_router"][0])
    top_i = route[:, :TOP_K].astype(I32)
    gates = route[:, TOP_K:2 * TOP_K]
    tm_e = min(1024, T)
    n_tiles = (TOP_K * T) // tm_e + N_EXPERTS
    plan = _route_plan(top_i, tm_e, n_tiles)
    y_sorted = _experts(xn, plan, p["moe_w_gate"], p["moe_w_up"], p["moe_w_down"], tm_e, 256)
    y = _combine(h3, gates, y_sorted, plan[0], p["final_norm"][None])

    return y, C[None], n[None], m[None], kv_rows


def kernel(x_prompt, x_sample, state_mlstm_C, state_mlstm_n, state_mlstm_m, cache_kv_w128, cache_kv_w512, cache_kv_w2048, norm_mix, norm_ffn, mlstm_w_in, mlstm_b_gates, mlstm_out_norm, mlstm_w_out, kv_norm, w_kv, attn_w_q, attn_w_out, ffn_w_gate, ffn_w_up, ffn_w_down, moe_w_router, moe_b_router, moe_w_gate, moe_w_up, moe_w_down, final_norm):
    assert norm_mix.shape[0] == 2 and mlstm_w_in.shape[0] == 1 and attn_w_q.shape[0] == 1
    p = dict(norm_mix=norm_mix, norm_ffn=norm_ffn, mlstm_w_in=mlstm_w_in, mlstm_b_gates=mlstm_b_gates,
             mlstm_out_norm=mlstm_out_norm, mlstm_w_out=mlstm_w_out, kv_norm=kv_norm, w_kv=w_kv,
             attn_w_q=attn_w_q, attn_w_out=attn_w_out, ffn_w_gate=ffn_w_gate, ffn_w_up=ffn_w_up,
             ffn_w_down=ffn_w_down, moe_w_router=moe_w_router, moe_b_router=moe_b_router,
             moe_w_gate=moe_w_gate, moe_w_up=moe_w_up, moe_w_down=moe_w_down, final_norm=final_norm)
    B, S, D = x_prompt.shape
    Bd = x_sample.shape[0]
    assert x_sample.shape[1] == 1 and Bd <= SAMPLE_ROWS

    y_p, C_p, n_p, m_p, kv_rows_p = _trunk(x_prompt.reshape(B * S, D), None, None, B, S, p)

    xs = jnp.zeros((SAMPLE_ROWS, D), F32).at[:Bd].set(x_sample[:, 0])
    state = (state_mlstm_C[0], state_mlstm_n[0], state_mlstm_m[0])
    y_s, C_s, n_s, m_s, kv_rows_s = _trunk(xs, state, (cache_kv_w128, cache_kv_w512, cache_kv_w2048),
                                           Bd, 1, p)

    return (y_p.reshape(B, S, D), y_s[:Bd, None], C_p, n_p, m_p, *kv_rows_p,
            C_s, n_s, m_s, *kv_rows_s)
```

```python
import functools

import jax
import jax.numpy as jnp
from jax import lax
from jax.experimental import pallas as pl
from jax.experimental.pallas import tpu as pltpu

F32 = jnp.float32
BF16 = jnp.bfloat16
I32 = jnp.int32

EPS = 1e-6
GATE_SOFTCAP = 15.0
MLSTM_HEADS = 4
ATTN_HEAD_DIM = 128
DILATED_GROUPS = ((128, 1), (512, 4), (2048, 16))
N_EXPERTS = 8
TOP_K = 2
MLSTM_CHUNK = 256
ATTN_BLOCK = 128
SAMPLE_ROWS = 16

VMEM_LIMIT = 56 * 1024 * 1024


def _params(*sem):
    return pltpu.CompilerParams(dimension_semantics=sem, vmem_limit_bytes=VMEM_LIMIT)


def _dot(a, b):
    return jnp.dot(a, b, preferred_element_type=F32)


def _dot_nt(a, b):
    return lax.dot_general(a, b, (((1,), (1,)), ((), ())), preferred_element_type=F32)


def _dot_tn(a, b):
    return lax.dot_general(a, b, (((0,), (0,)), ((), ())), preferred_element_type=F32)


def _split(x):
    hi = x.astype(BF16)
    lo = (x - hi.astype(F32)).astype(BF16)
    return hi, lo


def _dot_nt_3pass(x, w):
    xh, xl = _split(x)
    wh, wl = _split(w)
    return _dot_nt(xh, wh) + (_dot_nt(xl, wh) + _dot_nt(xh, wl))


def _dot_3pass(x, w, wh, cols=256):
    n = x.shape[0]
    xh, xl = _split(x)
    xs = jnp.concatenate([xh, xl], axis=0)
    outs = []
    for c0 in range(0, w.shape[1], cols):
        wh_c = wh[:, c0:c0 + cols]
        wl_c = (w[:, c0:c0 + cols] - wh_c.astype(F32)).astype(BF16)
        r = _dot(xs, wh_c)
        outs.append(r[:n] + (r[n:] + _dot(xh, wl_c)))
    return outs[0] if len(outs) == 1 else jnp.concatenate(outs, axis=1)


def _sigmoid(x):
    return 1.0 / (1.0 + jnp.exp(-x))


def _rms(x):
    return x * lax.rsqrt(jnp.mean(x * x, axis=-1, keepdims=True) + EPS)


def _norm_kernel(x_ref, g_ref, *o_refs):
    y = _rms(x_ref[...])
    for i, o_ref in enumerate(o_refs):
        o_ref[...] = (y * g_ref[i:i + 1, :]).astype(o_ref.dtype)


def _row_tile(M, cap):
    t = (min(cap, M) // 16) * 16
    while M % t:
        t -= 16
    return t


def _norm(x, gains, dtypes):
    T, D = x.shape
    tm = _row_tile(T, 1024)
    return pl.pallas_call(
        _norm_kernel,
        out_shape=[jax.ShapeDtypeStruct((T, D), dt) for dt in dtypes],
        grid=(T // tm,),
        in_specs=[pl.BlockSpec((tm, D), lambda i: (i, 0)),
                  pl.BlockSpec(gains.shape, lambda i: (0, 0))],
        out_specs=[pl.BlockSpec((tm, D), lambda i: (i, 0)) for _ in dtypes],
        compiler_params=_params("parallel"),
        name="rmsnorm",
    )(x, gains)


def _mm_rows(M, K):
    return _row_tile(M, 1024 if K <= 4096 else 512)


def _weight_spec(w, tn):
    if w.ndim == 3:
        return pl.BlockSpec((None, w.shape[1], tn), lambda j, i: (0, 0, j))
    return pl.BlockSpec((w.shape[0], tn), lambda j, i: (0, j))


def _mm_kernel(a_ref, at_ref, w_ref, *rest, has_res):
    if has_res:
        r_ref, rt_ref, o_ref, ot_ref, wb_ref = rest
    else:
        o_ref, ot_ref, wb_ref = rest
    i = pl.program_id(1)
    n_main = pl.num_programs(1) - 1

    @pl.when(i == 0)
    def _():
        wb_ref[...] = w_ref[...].astype(BF16)

    @pl.when(i < n_main)
    def _():
        acc = _dot(a_ref[...], wb_ref[...])
        o_ref[...] = (r_ref[...] + acc if has_res else acc).astype(o_ref.dtype)

    @pl.when(i == n_main)
    def _():
        acc = _dot_3pass(at_ref[...], w_ref, wb_ref)
        ot_ref[...] = rt_ref[...] + acc if has_res else acc


def _matmul(a, a_tail, w, n_cols, out_dtype, res=None, res_tail=None, tn=1024):
    M, K = a.shape
    n = a_tail.shape[0]
    tm = _mm_rows(M, K)
    tn = min(tn, n_cols)
    n_main = M // tm
    main = lambda j, i: (jnp.minimum(i, n_main - 1), j)
    in_specs = [pl.BlockSpec((tm, K), lambda j, i: (jnp.minimum(i, n_main - 1), 0)),
                pl.BlockSpec((n, K), lambda j, i: (0, 0)), _weight_spec(w, tn)]
    args = [a, a_tail, w]
    if res is not None:
        in_specs += [pl.BlockSpec((tm, tn), main), pl.BlockSpec((n, tn), lambda j, i: (0, j))]
        args += [res, res_tail]
    return pl.pallas_call(
        functools.partial(_mm_kernel, has_res=res is not None),
        out_shape=[jax.ShapeDtypeStruct((M, n_cols), out_dtype), jax.ShapeDtypeStruct((n, n_cols), F32)],
        grid=(n_cols // tn, n_main + 1),
        in_specs=in_specs,
        out_specs=[pl.BlockSpec((tm, tn), main), pl.BlockSpec((n, tn), lambda j, i: (0, j))],
        scratch_shapes=[pltpu.VMEM((K, tn), BF16)],
        compiler_params=_params("arbitrary", "arbitrary"),
        name="matmul",
    )(*args)


def _glu_kernel(a_ref, at_ref, wg_ref, wu_ref, o_ref, ot_ref, wgb_ref, wub_ref):
    i = pl.program_id(1)
    n_main = pl.num_programs(1) - 1

    @pl.when(i == 0)
    def _():
        wgb_ref[...] = wg_ref[...].astype(BF16)
        wub_ref[...] = wu_ref[...].astype(BF16)

    @pl.when(i < n_main)
    def _():
        a = a_ref[...]
        g = _dot(a, wgb_ref[...])
        u = _dot(a, wub_ref[...])
        o_ref[...] = (g * _sigmoid(g) * u).astype(o_ref.dtype)

    @pl.when(i == n_main)
    def _():
        a = at_ref[...]
        g = _dot_3pass(a, wg_ref, wgb_ref)
        u = _dot_3pass(a, wu_ref, wub_ref)
        ot_ref[...] = g * _sigmoid(g) * u


def _glu(a, a_tail, wg, wu, tf=512):
    M, K = a.shape
    n = a_tail.shape[0]
    F = wg.shape[-1]
    tm = _mm_rows(M, K)
    n_main = M // tm
    return pl.pallas_call(
        _glu_kernel,
        out_shape=[jax.ShapeDtypeStruct((M, F), BF16), jax.ShapeDtypeStruct((n, F), F32)],
        grid=(F // tf, n_main + 1),
        in_specs=[pl.BlockSpec((tm, K), lambda j, i: (jnp.minimum(i, n_main - 1), 0)),
                  pl.BlockSpec((n, K), lambda j, i: (0, 0)), _weight_spec(wg, tf), _weight_spec(wu, tf)],
        out_specs=[pl.BlockSpec((tm, tf), lambda j, i: (jnp.minimum(i, n_main - 1), j)),
                   pl.BlockSpec((n, tf), lambda j, i: (0, j))],
        scratch_shapes=[pltpu.VMEM((K, tf), BF16), pltpu.VMEM((K, tf), BF16)],
        compiler_params=_params("arbitrary", "arbitrary"),
        name="swiglu_up",
    )(a, a_tail, wg, wu)


def _gates_kernel(x_ref, g_ref, wt_ref, bc_ref, br_ref, gc_ref, gr_ref):
    H = MLSTM_HEADS
    xn = _rms(x_ref[...]) * g_ref[...]
    wt = wt_ref[...]

    def finish(pre, head_axis):
        t = GATE_SOFTCAP * jnp.tanh(pre / GATE_SOFTCAP)
        ls = jnp.minimum(t, 0.0) - jnp.log1p(jnp.exp(-jnp.abs(t)))
        idx = lax.broadcasted_iota(I32, t.shape, head_axis)
        return jnp.where(idx < H, t, ls)

    gc_ref[...] = finish(_dot_nt_3pass(xn, wt) + bc_ref[...], 1)
    gr_ref[...] = finish(_dot_nt_3pass(wt, xn) + br_ref[...], 0)


def _gates(x, gain, w_gate_t, b_gates):
    T, D = x.shape
    G = w_gate_t.shape[0]
    tm = min(T, 512)
    return pl.pallas_call(
        _gates_kernel,
        out_shape=[jax.ShapeDtypeStruct((T, G), F32), jax.ShapeDtypeStruct((G, T), F32)],
        grid=(T // tm,),
        in_specs=[pl.BlockSpec((tm, D), lambda i: (i, 0)),
                  pl.BlockSpec((1, D), lambda i: (0, 0)),
                  pl.BlockSpec((G, D), lambda i: (0, 0)),
                  pl.BlockSpec((1, G), lambda i: (0, 0)),
                  pl.BlockSpec((G, 1), lambda i: (0, 0))],
        out_specs=[pl.BlockSpec((tm, G), lambda i: (i, 0)),
                   pl.BlockSpec((G, tm), lambda i: (0, i))],
        compiler_params=_params("parallel"),
        name="mlstm_gates",
    )(x, gain, w_gate_t, b_gates.reshape(1, G), b_gates.reshape(G, 1))


def _mlstm_kernel(q_ref, k_ref, v_ref, o_ref, gc_ref, gr_ref, on_ref,
                  h_ref, c_out, n_out, m_out, c_s, n_s, m_s, *, dk, dv):
    H = MLSTM_HEADS
    c = pl.program_id(1)
    L = q_ref.shape[0]

    @pl.when(c == 0)
    def _():
        c_s[...] = jnp.zeros_like(c_s)
        n_s[...] = jnp.zeros_like(n_s)
        m_s[...] = jnp.zeros_like(m_s)

    row = lax.broadcasted_iota(I32, (L, L), 0)
    col = lax.broadcasted_iota(I32, (L, L), 1)
    tril = row >= col
    for h in range(H):
        q = q_ref[:, h * dk:(h + 1) * dk]
        k = k_ref[:, h * dk:(h + 1) * dk] * (dk ** -0.5)
        v = v_ref[:, h * dv:(h + 1) * dv]
        ig_c = gc_ref[:, h:h + 1]
        lf_c = gc_ref[:, H + h:H + h + 1]
        ig_r = gr_ref[h:h + 1, :]
        lf_r = gr_ref[H + h:H + h + 1, :]
        b_c = jnp.sum(jnp.where(tril, lf_r, 0.0), axis=1, keepdims=True)
        b_r = jnp.sum(jnp.where(row <= col, lf_c, 0.0), axis=0, keepdims=True)
        m_prev = m_s[h][:, 0:1]
        C = c_s[h]
        n = n_s[h]
        dmat = jnp.where(tril, b_c - b_r + ig_r, -jnp.inf)
        inter = b_c + m_prev
        m_t = jnp.maximum(inter, jnp.max(dmat, axis=1, keepdims=True))
        w_intra = jnp.exp(dmat - m_t)
        w_inter = jnp.exp(inter - m_t)
        qb = q.astype(BF16)
        vb = v.astype(BF16)
        s = _dot_nt(qb, k.astype(BF16)) * w_intra
        num = w_inter * _dot(qb, C.astype(BF16)) + _dot(s.astype(BF16), vb)
        den = w_inter * jnp.sum(q * n, axis=1, keepdims=True) + jnp.sum(s, axis=1, keepdims=True)
        hh = num / jnp.maximum(jnp.abs(den), jnp.exp(-m_t))
        b_last = b_c[L - 1:L, :]
        g_r = b_last - b_r + ig_r
        g_c = b_last - b_c + ig_c
        m_new = jnp.maximum(b_last + m_prev, jnp.max(g_r, axis=1, keepdims=True))
        decay = jnp.exp(b_last + m_prev - m_new)
        kw = k * jnp.exp(g_c - m_new)
        c_s[h] = decay * C + _dot_tn(kw.astype(BF16), vb)
        n_s[h] = decay * n + jnp.sum(kw, axis=0, keepdims=True)
        m_s[h] = jnp.broadcast_to(m_new, m_s.shape[1:])
        sl = slice(h * dv, (h + 1) * dv)
        h_ref[:, sl] = (_rms(hh) * on_ref[:, sl] * _sigmoid(o_ref[:, sl])).astype(h_ref.dtype)

    @pl.when(c == pl.num_programs(1) - 1)
    def _():
        c_out[0] = c_s[...]
        n_out[0] = n_s[...]
        m_out[0] = m_s[...]


def _mlstm_prompt(proj, gc, gr, out_norm, B, S):
    H = MLSTM_HEADS
    T = B * S
    hv = out_norm.shape[1]
    dv = hv // H
    dk = dv // 2
    hk = H * dk
    L = MLSTM_CHUNK
    nc = S // L
    G = gc.shape[1]
    rows = lambda b, c: b * nc + c
    return pl.pallas_call(
        functools.partial(_mlstm_kernel, dk=dk, dv=dv),
        out_shape=[jax.ShapeDtypeStruct((T, hv), BF16),
                   jax.ShapeDtypeStruct((B, H, dk, dv), F32),
                   jax.ShapeDtypeStruct((B, H, 1, dk), F32),
                   jax.ShapeDtypeStruct((B, H, 1, 128), F32)],
        grid=(B, nc),
        in_specs=[pl.BlockSpec((L, hk), lambda b, c: (rows(b, c), 0)),
                  pl.BlockSpec((L, hk), lambda b, c: (rows(b, c), 1)),
                  pl.BlockSpec((L, hv), lambda b, c: (rows(b, c), 1)),
                  pl.BlockSpec((L, hv), lambda b, c: (rows(b, c), 2)),
                  pl.BlockSpec((L, G), lambda b, c: (rows(b, c), 0)),
                  pl.BlockSpec((G, L), lambda b, c: (0, rows(b, c))),
                  pl.BlockSpec((1, hv), lambda b, c: (0, 0))],
        out_specs=[pl.BlockSpec((L, hv), lambda b, c: (rows(b, c), 0)),
                   pl.BlockSpec((1, H, dk, dv), lambda b, c: (b, 0, 0, 0)),
                   pl.BlockSpec((1, H, 1, dk), lambda b, c: (b, 0, 0, 0)),
                   pl.BlockSpec((1, H, 1, 128), lambda b, c: (b, 0, 0, 0))],
        scratch_shapes=[pltpu.VMEM((H, dk, dv), F32), pltpu.VMEM((H, 1, dk), F32),
                        pltpu.VMEM((H, 1, 128), F32)],
        compiler_params=_params("arbitrary", "arbitrary"),
        name="mlstm_prompt",
    )(proj, proj, proj, proj, gc, gr, out_norm)


def _mlstm_step_kernel(qr_ref, qc_ref, kr_ref, kc_ref, v_ref, o_ref, ig_ref, lf_ref, on_ref,
                       c_ref, n_ref, m_ref, h_ref, c_out, n_out, m_out):
    H = MLSTM_HEADS
    dk = qr_ref.shape[-1]
    dv = v_ref.shape[-1]
    for h in range(H):
        q_r = qr_ref[0, h]
        q_c = qc_ref[0, h]
        k_r = kr_ref[0, h] * (dk ** -0.5)
        k_c = kc_ref[0, h] * (dk ** -0.5)
        v = v_ref[0, h]
        ig = ig_ref[0, h]
        lf = lf_ref[0, h]
        C = c_ref[0, h]
        n = n_ref[0, h]
        m = m_ref[0, h]
        inter = lf + m
        m_t = jnp.maximum(inter, ig)
        w_intra = jnp.exp(ig - m_t)
        w_inter = jnp.exp(inter - m_t)
        s = jnp.sum(q_r * k_r, axis=1, keepdims=True) * w_intra
        num = w_inter * jnp.sum(q_c * C, axis=0, keepdims=True) + s * v
        den = w_inter * jnp.sum(q_r * n, axis=1, keepdims=True) + s
        hh = num / jnp.maximum(jnp.abs(den), jnp.exp(-m_t))
        decay = jnp.exp(inter - m_t)
        w_k = jnp.exp(ig - m_t)
        c_out[0, h] = decay * C + (k_c * w_k) * v
        n_out[0, h] = decay * n + k_r * w_k
        m_out[0, h] = m_t
        sl = slice(h * dv, (h + 1) * dv)
        h_ref[0, :, sl] = _rms(hh) * on_ref[:, sl] * _sigmoid(o_ref[0, :, sl])


def _mlstm_decode(proj, gc, out_norm, C0, n0, m0):
    H = MLSTM_HEADS
    Bd, _, dk, dv = C0.shape
    hk, hv = H * dk, H * dv
    q = proj[:Bd, :hk].reshape(Bd, H, dk)
    k = proj[:Bd, hk:2 * hk].reshape(Bd, H, dk)
    v = proj[:Bd, 2 * hk:2 * hk + hv].reshape(Bd, H, 1, dv)
    o = proj[:Bd, 2 * hk + hv:2 * hk + 2 * hv].reshape(Bd, 1, hv)
    ig = gc[:Bd, :H].reshape(Bd, H, 1, 1)
    lf = gc[:Bd, H:].reshape(Bd, H, 1, 1)
    spec = lambda *tail: pl.BlockSpec((1, H) + tail, lambda b: (b, 0, 0, 0))
    return pl.pallas_call(
        _mlstm_step_kernel,
        out_shape=[jax.ShapeDtypeStruct((Bd, 1, hv), F32),
                   jax.ShapeDtypeStruct((Bd, H, dk, dv), F32),
                   jax.ShapeDtypeStruct((Bd, H, 1, dk), F32),
                   jax.ShapeDtypeStruct((Bd, H, 1, 1), F32)],
        grid=(Bd,),
        in_specs=[spec(1, dk), spec(dk, 1), spec(1, dk), spec(dk, 1), spec(1, dv),
                  pl.BlockSpec((1, 1, hv), lambda b: (b, 0, 0)),
                  spec(1, 1), spec(1, 1),
                  pl.BlockSpec((1, hv), lambda b: (0, 0)),
                  spec(dk, dv), spec(1, dk), spec(1, 1)],
        out_specs=[pl.BlockSpec((1, 1, hv), lambda b: (b, 0, 0)),
                   spec(dk, dv), spec(1, dk), spec(1, 1)],
        compiler_params=_params("parallel"),
        name="mlstm_decode",
    )(q.reshape(Bd, H, 1, dk), q.reshape(Bd, H, dk, 1), k.reshape(Bd, H, 1, dk),
      k.reshape(Bd, H, dk, 1), v, o, ig, lf, out_norm, C0, n0.reshape(Bd, H, 1, dk),
      m0.reshape(Bd, H, 1, 1))


def _rows(start, size, stride):
    return pl.ds(start, size) if stride == 1 else pl.ds(start, size, stride=stride)


def _attn_kernel(*refs, dilations, seq):
    G = len(dilations)
    q_refs, k_refs, v_refs = refs[:G], refs[G:2 * G], refs[2 * G:3 * G]
    out_ref = refs[3 * G]
    o_scr, lse_scr, s_scr, p_scr = refs[3 * G + 1:]
    bq = ATTN_BLOCK
    D = ATTN_HEAD_DIM
    scale = D ** -0.5
    row = lax.broadcasted_iota(I32, (bq, 2 * bq), 0)
    col = lax.broadcasted_iota(I32, (bq, 2 * bq), 1)
    bias_full = jnp.where(jnp.logical_and(col - row >= 0, col - row <= bq), 0.0, -jnp.inf)
    row1 = lax.broadcasted_iota(I32, (bq, bq), 0)
    col1 = lax.broadcasted_iota(I32, (bq, bq), 1)
    bias_first = jnp.where(row1 >= col1, 0.0, -jnp.inf)
    for g, r in enumerate(dilations):
        nb = seq // (r * bq)
        blocks = [(rho, u) for rho in range(r) for u in range(nb)]
        for i, (rho, u) in enumerate(blocks):
            q = q_refs[g][_rows(rho + u * bq * r, bq, r), :].astype(BF16)
            if u == 0:
                k = k_refs[g][_rows(rho, bq, r), :].astype(BF16)
                s_scr[i, :, :bq] = _dot_nt(q, k) * scale + bias_first
            else:
                k = k_refs[g][_rows(rho + (u - 1) * bq * r, 2 * bq, r), :].astype(BF16)
                s_scr[i] = _dot_nt(q, k) * scale + bias_full
        for i, (rho, u) in enumerate(blocks):
            s = s_scr[i, :, :bq] if u == 0 else s_scr[i]
            mx = jnp.max(s, axis=1, keepdims=True)
            e = jnp.exp(s - mx)
            l = jnp.sum(e, axis=1, keepdims=True)
            p = (e / l).astype(BF16)
            if u == 0:
                p_scr[i, :, :bq] = p
            else:
                p_scr[i] = p
            lse_scr[g, _rows(rho + u * bq * r, bq, r), :] = jnp.broadcast_to(mx + jnp.log(l), (bq, D))
        for i, (rho, u) in enumerate(blocks):
            if u == 0:
                v = v_refs[g][_rows(rho, bq, r), :].astype(BF16)
                o = _dot(p_scr[i, :, :bq], v)
            else:
                v = v_refs[g][_rows(rho + (u - 1) * bq * r, 2 * bq, r), :].astype(BF16)
                o = _dot(p_scr[i], v)
            o_scr[g, _rows(rho + u * bq * r, bq, r), :] = o
    lses = [lse_scr[g] for g in range(G)]
    mx = functools.reduce(jnp.maximum, lses)
    es = [jnp.exp(l - mx) for l in lses]
    tot = functools.reduce(lambda a, b: a + b, es)
    acc = (es[0] / tot) * o_scr[0]
    for g in range(1, G):
        acc = acc + (es[g] / tot) * o_scr[g]
    out_ref[...] = acc.astype(out_ref.dtype)


def _attention(q, kv, B, S, heads):
    D = ATTN_HEAD_DIM
    T = B * S
    G = len(DILATED_GROUPS)
    dil = tuple(d for _, d in DILATED_GROUPS)
    assert all(w // d == ATTN_BLOCK and S % (d * ATTN_BLOCK) == 0 for w, d in DILATED_GROUPS)
    nblk = S // ATTN_BLOCK
    spec = lambda col: pl.BlockSpec((S, D), lambda b, h: (b, col(h)))
    in_specs = ([spec(lambda h, g=g: g * heads + h) for g in range(G)]
                + [spec(lambda h, g=g: 2 * g * heads + h) for g in range(G)]
                + [spec(lambda h, g=g: (2 * g + 1) * heads + h) for g in range(G)])
    return pl.pallas_call(
        functools.partial(_attn_kernel, dilations=dil, seq=S),
        out_shape=jax.ShapeDtypeStruct((T, heads * D), BF16),
        grid=(B, heads),
        in_specs=in_specs,
        out_specs=pl.BlockSpec((S, D), lambda b, h: (b, h)),
        scratch_shapes=[pltpu.VMEM((G, S, D), F32), pltpu.VMEM((G, S, D), F32),
                        pltpu.VMEM((nblk, ATTN_BLOCK, 2 * ATTN_BLOCK), F32),
                        pltpu.VMEM((nblk, ATTN_BLOCK, 2 * ATTN_BLOCK), BF16)],
        compiler_params=_params("parallel", "parallel"),
        name="dilated_attention",
    )(*([q] * G), *([kv] * (2 * G)))


def _kv_rows_kernel(x_ref, o_ref, *, heads):
    o_ref[0] = x_ref[...].reshape(x_ref.shape[0], 2, heads, ATTN_HEAD_DIM)


def _kv_rows(kv, g, keep, B, S, heads):
    W2 = 2 * heads * ATTN_HEAD_DIM
    tm = min(keep, 256)
    first = (S - keep) // tm
    per_seq = S // tm
    return pl.pallas_call(
        functools.partial(_kv_rows_kernel, heads=heads),
        out_shape=jax.ShapeDtypeStruct((B, keep, 2, heads, ATTN_HEAD_DIM), F32),
        grid=(B, keep // tm),
        in_specs=[pl.BlockSpec((tm, W2), lambda b, i: (b * per_seq + first + i, g))],
        out_specs=pl.BlockSpec((1, tm, 2, heads, ATTN_HEAD_DIM), lambda b, i: (b, i, 0, 0, 0)),
        compiler_params=_params("parallel", "parallel"),
        name=f"kv_rows_g{g}",
    )(kv)


def _decode_attn_kernel(q_ref, kvn_ref, *refs):
    G = len(refs) - 1
    buf_refs = refs[:G]
    out_ref = refs[G]
    scale = ATTN_HEAD_DIM ** -0.5
    outs, lses = [], []
    for g in range(G):
        q = q_ref[g]
        k_new = kvn_ref[g, 0]
        v_new = kvn_ref[g, 1]
        k_buf = buf_refs[g][:, 0]
        v_buf = buf_refs[g][:, 1]
        s_buf = jnp.sum(k_buf * q[None], axis=-1, keepdims=True) * scale
        s_new = jnp.sum(k_new * q, axis=-1, keepdims=True) * scale
        mx = jnp.maximum(jnp.max(s_buf, axis=0), s_new)
        e_buf = jnp.exp(s_buf - mx[None])
        e_new = jnp.exp(s_new - mx)
        l = jnp.sum(e_buf, axis=0) + e_new
        outs.append(jnp.sum((e_buf / l[None]) * v_buf, axis=0) + (e_new / l) * v_new)
        lses.append(mx + jnp.log(l))
    mx = functools.reduce(jnp.maximum, lses)
    es = [jnp.exp(l - mx) for l in lses]
    tot = functools.reduce(lambda a, b: a + b, es)
    acc = (es[0] / tot) * outs[0]
    for g in range(1, G):
        acc = acc + (es[g] / tot) * outs[g]
    out_ref[...] = acc


def _decode_attention(q, kv_new, caches):
    Bd, G, H, D = q.shape
    steps = ATTN_BLOCK
    views = []
    for cache, (window, dilation) in zip(caches, DILATED_GROUPS):
        Lb = cache.shape[1]
        assert window // dilation == steps and Lb == steps * dilation
        views.append(cache.reshape(Bd, steps, dilation, 2, H, D))
    return pl.pallas_call(
        _decode_attn_kernel,
        out_shape=jax.ShapeDtypeStruct((Bd, H, D), F32),
        grid=(Bd,),
        in_specs=[pl.BlockSpec((None, G, H, D), lambda b: (b, 0, 0, 0)),
                  pl.BlockSpec((None, G, 2, H, D), lambda b: (b, 0, 0, 0, 0))]
                 + [pl.BlockSpec((None, steps, None, 2, H, D), lambda b: (b, 0, 0, 0, 0, 0)) for _ in views],
        out_specs=pl.BlockSpec((None, H, D), lambda b: (b, 0, 0)),
        compiler_params=_params("parallel"),
        name="decode_attention",
    )(q, kv_new, *views)


def _router_kernel(x_ref, wt_ref, b_ref, o_ref):
    logits = _dot_nt_3pass(x_ref[...], wt_ref[...]) + b_ref[...]
    e = jnp.exp(logits - jnp.max(logits, axis=1, keepdims=True))
    p = e / jnp.sum(e, axis=1, keepdims=True)
    E = p.shape[1]
    idx = lax.broadcasted_iota(I32, p.shape, 1)
    p1 = jnp.max(p, axis=1, keepdims=True)
    i1 = jnp.min(jnp.where(p == p1, idx, E), axis=1, keepdims=True)
    rest = jnp.where(idx == i1, -1.0, p)
    p2 = jnp.max(rest, axis=1, keepdims=True)
    i2 = jnp.min(jnp.where(rest == p2, idx, E), axis=1, keepdims=True)
    tot = p1 + p2
    o_ref[...] = jnp.where(idx == 0, i1.astype(F32),
                 jnp.where(idx == 1, i2.astype(F32),
                 jnp.where(idx == 2, p1 / tot,
                 jnp.where(idx == 3, p2 / tot, 0.0))))


def _router(xn, w_router_t, b_router):
    T, D = xn.shape
    E = w_router_t.shape[0]
    tm = _row_tile(T, 1024)
    return pl.pallas_call(
        _router_kernel,
        out_shape=jax.ShapeDtypeStruct((T, E), F32),
        grid=(T // tm,),
        in_specs=[pl.BlockSpec((tm, D), lambda i: (i, 0)),
                  pl.BlockSpec((E, D), lambda i: (0, 0)),
                  pl.BlockSpec((1, E), lambda i: (0, 0))],
        out_specs=pl.BlockSpec((tm, E), lambda i: (i, 0)),
        compiler_params=_params("parallel"),
        name="router",
    )(xn, w_router_t, b_router.reshape(1, E))


def _route_plan(top_i, tm, n_tiles):
    T = top_i.shape[0]
    e_flat = top_i.reshape(-1)
    onehot = (e_flat[:, None] == jnp.arange(N_EXPERTS, dtype=I32)[None, :]).astype(I32)
    csum = jnp.cumsum(onehot, axis=0)
    rank = jnp.sum(onehot * csum, axis=1) - 1
    counts = csum[-1]
    gsz = ((counts + tm - 1) // tm) * tm
    gend = jnp.cumsum(gsz)
    gstart = gend - gsz
    pos = (jnp.sum(onehot * gstart[None, :], axis=1) + rank).astype(I32)
    src = jnp.zeros((n_tiles * tm,), I32).at[pos].set(jnp.arange(TOP_K * T, dtype=I32) // TOP_K)
    tile_start = jnp.arange(n_tiles, dtype=I32) * tm
    tile_valid = (tile_start < gend[-1]).astype(I32)
    tile_e = jnp.sum((tile_start[:, None] >= gend[None, :]).astype(I32), axis=1)
    last_e = jnp.sum((gend[-1] - 1 >= gend).astype(I32))
    tile_e = jnp.where(tile_valid > 0, tile_e, last_e).astype(I32)
    return pos, src, tile_e, tile_valid


def _experts_kernel(te_ref, tv_ref, src_ref, x_hbm, wg_ref, wu_ref, wd_ref, o_ref, xf_ref, xb_ref, sem,
                    *, tm, per_step, accurate):
    i = pl.program_id(0)
    c = pl.program_id(1)
    n_tiles = pl.num_programs(0)
    n_rows = per_step * pl.num_programs(1)
    valid = tv_ref[i] > 0
    nxt = jnp.minimum(i + 1, n_tiles - 1)
    prefetch = jnp.logical_and(i + 1 < n_tiles, tv_ref[nxt] > 0)
    D = o_ref.shape[1]
    dn = min(D, 512)

    def row_copy(tile, r):
        src = src_ref[tile * tm + jnp.minimum(r, tm - 1)]
        return pltpu.make_async_copy(x_hbm.at[pl.ds(src, 1)], xf_ref.at[pl.ds(r, 1)], sem.at[0])

    @pl.when(jnp.logical_and(i == 0, c == 0))
    def _():
        def start(r, carry):
            row_copy(0, r).start()
            return carry

        lax.fori_loop(0, n_rows, start, 0)

    @pl.when(c == 0)
    def _():
        o_ref[...] = jnp.zeros_like(o_ref)

    @pl.when(jnp.logical_and(valid, c == 0))
    def _():
        def wait(r, carry):
            row_copy(i, r).wait()
            return carry

        lax.fori_loop(0, n_rows, wait, 0)
        xb_ref[...] = xf_ref[pl.ds(0, tm), :].astype(xb_ref.dtype)

    def mm(x, w):
        wh = w.astype(BF16)
        return _dot_3pass(x, w, wh) if accurate else _dot(x, wh)

    def chunk(with_prefetch):
        if with_prefetch:
            for j in range(per_step):
                row_copy(i + 1, c * per_step + j).start()
        x = xb_ref[...]
        g = mm(x, wg_ref[0])
        u = mm(x, wu_ref[0])
        act = (g * _sigmoid(g) * u).astype(xb_ref.dtype)
        for n0 in range(0, D, dn):
            o_ref[:, n0:n0 + dn] += mm(act, wd_ref[0, :, n0:n0 + dn])

    pl.when(jnp.logical_and(valid, prefetch))(functools.partial(chunk, True))
    pl.when(jnp.logical_and(valid, jnp.logical_not(prefetch)))(functools.partial(chunk, False))


def _experts(xn, pos_plan, w_gate, w_up, w_down, tm, tf, accurate=False):
    _, src, tile_e, tile_valid = pos_plan
    T, D = xn.shape
    Fe = w_gate.shape[-1]
    nf = Fe // tf
    n_tiles = tile_e.shape[0]
    per_step = -(-tm // nf)

    def chunk(i, c, tv):
        return jnp.where(tv[i] > 0, c, nf - 1)

    return pl.pallas_call(
        functools.partial(_experts_kernel, tm=tm, per_step=per_step, accurate=accurate),
        out_shape=jax.ShapeDtypeStruct((n_tiles * tm, D), F32),
        grid_spec=pltpu.PrefetchScalarGridSpec(
            num_scalar_prefetch=3,
            grid=(n_tiles, nf),
            in_specs=[pl.BlockSpec(memory_space=pl.ANY),
                      pl.BlockSpec((None, 1, D, tf), lambda i, c, te, tv, src: (0, te[i], 0, chunk(i, c, tv))),
                      pl.BlockSpec((None, 1, D, tf), lambda i, c, te, tv, src: (0, te[i], 0, chunk(i, c, tv))),
                      pl.BlockSpec((None, 1, tf, D), lambda i, c, te, tv, src: (0, te[i], chunk(i, c, tv), 0))],
            out_specs=pl.BlockSpec((tm, D), lambda i, c, te, tv, src: (i, 0)),
            scratch_shapes=[pltpu.VMEM((per_step * nf, D), F32), pltpu.VMEM((tm, D), F32 if accurate else BF16),
                            pltpu.SemaphoreType.DMA((1,))]),
        compiler_params=_params("arbitrary", "arbitrary"),
        name="experts",
    )(tile_e, tile_valid, src, xn, w_gate, w_up, w_down)


def _combine_kernel(pos_ref, h_ref, gate_ref, y_hbm, fg_ref, o_ref, ya_ref, yb_ref, sem, *, tm):
    i = pl.program_id(0)

    def copies(r):
        t = i * tm + r
        return (pltpu.make_async_copy(y_hbm.at[pl.ds(pos_ref[TOP_K * t], 1)],
                                      ya_ref.at[pl.ds(r, 1)], sem.at[0]),
                pltpu.make_async_copy(y_hbm.at[pl.ds(pos_ref[TOP_K * t + 1], 1)],
                                      yb_ref.at[pl.ds(r, 1)], sem.at[1]))

    def start(r, carry):
        a, b = copies(r)
        a.start()
        b.start()
        return carry

    def wait(r, carry):
        a, b = copies(r)
        a.wait()
        b.wait()
        return carry

    lax.fori_loop(0, tm, start, 0)
    lax.fori_loop(0, tm, wait, 0)
    hsum = h_ref[...] + (gate_ref[:, 0:1] * ya_ref[...] + gate_ref[:, 1:2] * yb_ref[...])
    o_ref[...] = _rms(hsum) * fg_ref[...]


def _combine(h, gates, y_sorted, pos, final_gain):
    T, D = h.shape
    tm = _row_tile(T, 256)
    return pl.pallas_call(
        functools.partial(_combine_kernel, tm=tm),
        out_shape=jax.ShapeDtypeStruct((T, D), F32),
        grid_spec=pltpu.PrefetchScalarGridSpec(
            num_scalar_prefetch=1,
            grid=(T // tm,),
            in_specs=[pl.BlockSpec((tm, D), lambda i, pos: (i, 0)),
                      pl.BlockSpec((tm, TOP_K), lambda i, pos: (i, 0)),
                      pl.BlockSpec(memory_space=pl.ANY),
                      pl.BlockSpec((1, D), lambda i, pos: (0, 0))],
            out_specs=pl.BlockSpec((tm, D), lambda i, pos: (i, 0)),
            scratch_shapes=[pltpu.VMEM((tm, D), F32), pltpu.VMEM((tm, D), F32),
                            pltpu.SemaphoreType.DMA((2,))]),
        compiler_params=_params("arbitrary"),
        name="moe_combine",
    )(pos, h, gates, y_sorted, final_gain)


def _forward(xp, xs, state, kv_caches, B, S, Bd, p):
    D = xp.shape[1]
    hv = p["mlstm_out_norm"].shape[-1]
    hk = hv // 2
    heads = p["attn_w_out"].shape[1] // ATTN_HEAD_DIM
    W = heads * ATTN_HEAD_DIM
    n_groups = len(DILATED_GROUPS)
    pad_rows = lambda a: jnp.zeros((SAMPLE_ROWS, a.shape[1]), F32).at[:Bd].set(a)

    gain0 = p["norm_mix"][0:1]
    w_gate_t = p["mlstm_w_in"][0, :, 2 * hk + 2 * hv:].T
    (xn,) = _norm(xp, gain0, [BF16])
    (xn_s,) = _norm(xs, gain0, [F32])
    proj, proj_s = _matmul(xn, xn_s, p["mlstm_w_in"], 2 * hk + 2 * hv, F32)
    gc, gr = _gates(xp, gain0, w_gate_t, p["mlstm_b_gates"][0])
    gc_s, _ = _gates(xs, gain0, w_gate_t, p["mlstm_b_gates"][0])
    hg, C_p, n_p, m_p = _mlstm_prompt(proj, gc, gr, p["mlstm_out_norm"], B, S)
    hg_s, C_s, n_s, m_s = _mlstm_decode(proj_s, gc_s, p["mlstm_out_norm"], *state)
    h1, h1_s = _matmul(hg, pad_rows(hg_s[:, 0]), p["mlstm_w_out"], D, F32, res=xp, res_tail=xs)

    (xn,) = _norm(h1, p["norm_ffn"][0:1], [BF16])
    (xn_s,) = _norm(h1_s, p["norm_ffn"][0:1], [F32])
    act, act_s = _glu(xn, xn_s, p["ffn_w_gate"], p["ffn_w_up"])
    h2, h2_s = _matmul(act, act_s, p["ffn_w_down"], D, F32, res=h1, res_tail=h1_s, tn=512)

    kvq_gains = jnp.concatenate([p["kv_norm"][None], p["norm_mix"][1:2]])
    xkv, xq = _norm(h2, kvq_gains, [BF16, BF16])
    xkv_s, xq_s = _norm(h2_s, kvq_gains, [F32, F32])
    kv, kv_s = _matmul(xkv, xkv_s, p["w_kv"], n_groups * 2 * W, F32)
    q, q_s = _matmul(xq, xq_s, p["attn_w_q"], n_groups * W, F32)
    att = _attention(q, kv, B, S, heads)
    kv_rows_p = [_kv_rows(kv, g, min(window, S), B, S, heads)
                 for g, (window, _) in enumerate(DILATED_GROUPS)]
    kv5 = kv_s[:Bd].reshape(Bd, n_groups, 2, heads, ATTN_HEAD_DIM)
    att_s = _decode_attention(q_s[:Bd].reshape(Bd, n_groups, heads, ATTN_HEAD_DIM), kv5, kv_caches)
    kv_rows_s = [kv5[:, None, g] for g in range(n_groups)]
    h3, h3_s = _matmul(att, pad_rows(att_s.reshape(Bd, W)), p["attn_w_out"], D, F32, res=h2, res_tail=h2_s)

    y_p = _moe_and_final_norm(h3, p, 1024, False)
    y_s = _moe_and_final_norm(h3_s, p, SAMPLE_ROWS, True)

    states_p = (C_p[None], n_p[:, :, 0][None], m_p[:, :, 0, 0][None])
    states_s = (C_s[None], n_s[:, :, 0][None], m_s[:, :, 0, 0][None])
    return y_p, y_s[:Bd], states_p, kv_rows_p, states_s, kv_rows_s


def _moe_and_final_norm(h, p, tm_e, accurate):
    T = h.shape[0]
    (xn,) = _norm(h, p["norm_ffn"][1:2], [F32])
    route = _router(xn, p["moe_w_router"][0].T, p["moe_b_router"][0])
    top_i = route[:, :TOP_K].astype(I32)
    gates = route[:, TOP_K:2 * TOP_K]
    n_tiles = -(-(TOP_K * T) // tm_e) + N_EXPERTS
    plan = _route_plan(top_i, tm_e, n_tiles)
    y_sorted = _experts(xn, plan, p["moe_w_gate"], p["moe_w_up"], p["moe_w_down"], tm_e, 256, accurate)
    return _combine(h, gates, y_sorted, plan[0], p["final_norm"][None])


def kernel(x_prompt, x_sample, state_mlstm_C, state_mlstm_n, state_mlstm_m, cache_kv_w128, cache_kv_w512, cache_kv_w2048, norm_mix, norm_ffn, mlstm_w_in, mlstm_b_gates, mlstm_out_norm, mlstm_w_out, kv_norm, w_kv, attn_w_q, attn_w_out, ffn_w_gate, ffn_w_up, ffn_w_down, moe_w_router, moe_b_router, moe_w_gate, moe_w_up, moe_w_down, final_norm):
    assert norm_mix.shape[0] == 2 and mlstm_w_in.shape[0] == 1 and attn_w_q.shape[0] == 1
    p = dict(norm_mix=norm_mix, norm_ffn=norm_ffn, mlstm_w_in=mlstm_w_in, mlstm_b_gates=mlstm_b_gates,
             mlstm_out_norm=mlstm_out_norm, mlstm_w_out=mlstm_w_out, kv_norm=kv_norm, w_kv=w_kv,
             attn_w_q=attn_w_q, attn_w_out=attn_w_out, ffn_w_gate=ffn_w_gate, ffn_w_up=ffn_w_up,
             ffn_w_down=ffn_w_down, moe_w_router=moe_w_router, moe_b_router=moe_b_router,
             moe_w_gate=moe_w_gate, moe_w_up=moe_w_up, moe_w_down=moe_w_down, final_norm=final_norm)
    B, S, D = x_prompt.shape
    Bd = x_sample.shape[0]
    assert x_sample.shape[1] == 1 and Bd <= SAMPLE_ROWS

    xs = jnp.zeros((SAMPLE_ROWS, D), F32).at[:Bd].set(x_sample[:, 0])
    state = (state_mlstm_C[0], state_mlstm_n[0], state_mlstm_m[0])
    y_p, y_s, states_p, kv_rows_p, states_s, kv_rows_s = _forward(
        x_prompt.reshape(B * S, D), xs, state, (cache_kv_w128, cache_kv_w512, cache_kv_w2048), B, S, Bd, p)
    return (y_p.reshape(B, S, D), y_s[:, None], *states_p, *kv_rows_p, *states_s, *kv_rows_s)
```

```python
import functools
import math

import jax
import jax.numpy as jnp
from jax import lax
from jax.experimental import pallas as pl
from jax.experimental.pallas import tpu as pltpu

F32 = jnp.float32
BF16 = jnp.bfloat16
I32 = jnp.int32

EPS = 1e-6
GATE_SOFTCAP = 15.0
MLSTM_HEADS = 4
ATTN_HEAD_DIM = 128
DILATED_GROUPS = ((128, 1), (512, 4), (2048, 16))
N_EXPERTS = 8
TOP_K = 2
MLSTM_CHUNK = 256
ATTN_BLOCK = 128
SAMPLE_ROWS = 16

VMEM_LIMIT = 56 * 1024 * 1024


def _params(*sem):
    return pltpu.CompilerParams(dimension_semantics=sem, vmem_limit_bytes=VMEM_LIMIT)


def _dot(a, b):
    return jnp.dot(a, b, preferred_element_type=F32)


def _dot_nt(a, b):
    return lax.dot_general(a, b, (((1,), (1,)), ((), ())), preferred_element_type=F32)


def _dot_tn(a, b):
    return lax.dot_general(a, b, (((0,), (0,)), ((), ())), preferred_element_type=F32)


def _split(x):
    hi = x.astype(BF16)
    lo = (x - hi.astype(F32)).astype(BF16)
    return hi, lo


def _dot_nt_3pass(x, w):
    xh, xl = _split(x)
    wh, wl = _split(w)
    return _dot_nt(xh, wh) + (_dot_nt(xl, wh) + _dot_nt(xh, wl))


def _dot_3pass(x, w, wh, cols=256):
    n = x.shape[0]
    xh, xl = _split(x)
    xs = jnp.concatenate([xh, xl], axis=0)
    outs = []
    for c0 in range(0, w.shape[1], cols):
        wh_c = wh[:, c0:c0 + cols]
        wl_c = (w[:, c0:c0 + cols] - wh_c.astype(F32)).astype(BF16)
        r = _dot(xs, wh_c)
        outs.append(r[:n] + (r[n:] + _dot(xh, wl_c)))
    return outs[0] if len(outs) == 1 else jnp.concatenate(outs, axis=1)


def _stack_tail(a_ref, at_ref, as_ref):
    tm = a_ref.shape[0]
    n = at_ref.shape[0]
    xh, xl = _split(at_ref[...])
    as_ref[0:tm, :] = a_ref[...]
    as_ref[tm:tm + n, :] = xh
    as_ref[tm + n:, :] = xl
    return xh


def _dot_stacked(as_ref, xh, w_ref, wh_ref, cols=256):
    n = xh.shape[0]
    tm = as_ref.shape[0] - 2 * n
    acc = _dot(as_ref[...], wh_ref[...])
    los = []
    for c0 in range(0, w_ref.shape[1], cols):
        wl_c = (w_ref[:, c0:c0 + cols] - wh_ref[:, c0:c0 + cols].astype(F32)).astype(BF16)
        los.append(_dot(xh, wl_c))
    lo = los[0] if len(los) == 1 else jnp.concatenate(los, axis=1)
    return acc[:tm], acc[tm:tm + n] + (acc[tm + n:] + lo)


def _sigmoid(x):
    return 1.0 / (1.0 + jnp.exp(-x))


def _rms(x):
    return x * lax.rsqrt(jnp.mean(x * x, axis=-1, keepdims=True) + EPS)


def _norm_kernel(x_ref, g_ref, *o_refs):
    y = _rms(x_ref[...])
    for i, o_ref in enumerate(o_refs):
        o_ref[...] = (y * g_ref[i:i + 1, :]).astype(o_ref.dtype)


def _row_tile(M, cap):
    t = (min(cap, M) // 16) * 16
    while M % t:
        t -= 16
    return t


def _norm(x, gains, dtypes):
    T, D = x.shape
    tm = _row_tile(T, 1024)
    return pl.pallas_call(
        _norm_kernel,
        out_shape=[jax.ShapeDtypeStruct((T, D), dt) for dt in dtypes],
        grid=(T // tm,),
        in_specs=[pl.BlockSpec((tm, D), lambda i: (i, 0)),
                  pl.BlockSpec(gains.shape, lambda i: (0, 0))],
        out_specs=[pl.BlockSpec((tm, D), lambda i: (i, 0)) for _ in dtypes],
        compiler_params=_params("parallel"),
        name="rmsnorm",
    )(x, gains)


def _mm_rows(M, K):
    return _row_tile(M, 1024 if K <= 4096 else 512)


def _weight_spec(w, tn):
    if w.ndim == 3:
        return pl.BlockSpec((None, w.shape[1], tn), lambda j, i: (0, 0, j))
    return pl.BlockSpec((w.shape[0], tn), lambda j, i: (0, j))


def _mm_kernel(a_ref, at_ref, w_ref, *rest, has_res):
    if has_res:
        r_ref, rt_ref, o_ref, ot_ref, wb_ref, as_ref = rest
    else:
        o_ref, ot_ref, wb_ref, as_ref = rest
    i = pl.program_id(1)
    last = pl.num_programs(1) - 1

    @pl.when(i == 0)
    def _():
        wb_ref[...] = w_ref[...].astype(BF16)

    @pl.when(i < last)
    def _():
        acc = _dot(a_ref[...], wb_ref[...])
        o_ref[...] = (r_ref[...] + acc if has_res else acc).astype(o_ref.dtype)

    @pl.when(i == last)
    def _():
        xh = _stack_tail(a_ref, at_ref, as_ref)
        acc, acc_t = _dot_stacked(as_ref, xh, w_ref, wb_ref)
        o_ref[...] = (r_ref[...] + acc if has_res else acc).astype(o_ref.dtype)
        ot_ref[...] = rt_ref[...] + acc_t if has_res else acc_t


def _matmul(a, a_tail, w, n_cols, out_dtype, res=None, res_tail=None, tn=1024):
    M, K = a.shape
    n = a_tail.shape[0]
    tm = _mm_rows(M, K)
    tn = min(tn, n_cols)
    in_specs = [pl.BlockSpec((tm, K), lambda j, i: (i, 0)),
                pl.BlockSpec((n, K), lambda j, i: (0, 0)), _weight_spec(w, tn)]
    args = [a, a_tail, w]
    if res is not None:
        in_specs += [pl.BlockSpec((tm, tn), lambda j, i: (i, j)), pl.BlockSpec((n, tn), lambda j, i: (0, j))]
        args += [res, res_tail]
    return pl.pallas_call(
        functools.partial(_mm_kernel, has_res=res is not None),
        out_shape=[jax.ShapeDtypeStruct((M, n_cols), out_dtype), jax.ShapeDtypeStruct((n, n_cols), F32)],
        grid=(n_cols // tn, M // tm),
        in_specs=in_specs,
        out_specs=[pl.BlockSpec((tm, tn), lambda j, i: (i, j)), pl.BlockSpec((n, tn), lambda j, i: (0, j))],
        scratch_shapes=[pltpu.VMEM((K, tn), BF16), pltpu.VMEM((tm + 2 * n, K), BF16)],
        compiler_params=_params("arbitrary", "arbitrary"),
        name="matmul",
    )(*args)


def _glu_kernel(a_ref, at_ref, wg_ref, wu_ref, o_ref, ot_ref, wgb_ref, wub_ref, as_ref):
    i = pl.program_id(1)
    last = pl.num_programs(1) - 1

    @pl.when(i == 0)
    def _():
        wgb_ref[...] = wg_ref[...].astype(BF16)
        wub_ref[...] = wu_ref[...].astype(BF16)

    @pl.when(i < last)
    def _():
        a = a_ref[...]
        g = _dot(a, wgb_ref[...])
        u = _dot(a, wub_ref[...])
        o_ref[...] = (g * _sigmoid(g) * u).astype(o_ref.dtype)

    @pl.when(i == last)
    def _():
        xh = _stack_tail(a_ref, at_ref, as_ref)
        g, g_t = _dot_stacked(as_ref, xh, wg_ref, wgb_ref)
        u, u_t = _dot_stacked(as_ref, xh, wu_ref, wub_ref)
        o_ref[...] = (g * _sigmoid(g) * u).astype(o_ref.dtype)
        ot_ref[...] = g_t * _sigmoid(g_t) * u_t


def _glu(a, a_tail, wg, wu, tf=512):
    M, K = a.shape
    n = a_tail.shape[0]
    F = wg.shape[-1]
    tm = _mm_rows(M, K)
    return pl.pallas_call(
        _glu_kernel,
        out_shape=[jax.ShapeDtypeStruct((M, F), BF16), jax.ShapeDtypeStruct((n, F), F32)],
        grid=(F // tf, M // tm),
        in_specs=[pl.BlockSpec((tm, K), lambda j, i: (i, 0)),
                  pl.BlockSpec((n, K), lambda j, i: (0, 0)), _weight_spec(wg, tf), _weight_spec(wu, tf)],
        out_specs=[pl.BlockSpec((tm, tf), lambda j, i: (i, j)), pl.BlockSpec((n, tf), lambda j, i: (0, j))],
        scratch_shapes=[pltpu.VMEM((K, tf), BF16), pltpu.VMEM((K, tf), BF16),
                        pltpu.VMEM((tm + 2 * n, K), BF16)],
        compiler_params=_params("arbitrary", "arbitrary"),
        name="swiglu_up",
    )(a, a_tail, wg, wu)


def _gates_kernel(x_ref, g_ref, wt_ref, bc_ref, br_ref, gc_ref, gr_ref):
    H = MLSTM_HEADS
    xn = _rms(x_ref[...]) * g_ref[...]
    wt = wt_ref[...]

    def finish(pre, head_axis):
        t = GATE_SOFTCAP * jnp.tanh(pre / GATE_SOFTCAP)
        ls = jnp.minimum(t, 0.0) - jnp.log1p(jnp.exp(-jnp.abs(t)))
        idx = lax.broadcasted_iota(I32, t.shape, head_axis)
        return jnp.where(idx < H, t, ls)

    gc_ref[...] = finish(_dot_nt_3pass(xn, wt) + bc_ref[...], 1)
    gr_ref[...] = finish(_dot_nt_3pass(wt, xn) + br_ref[...], 0)


def _gates(x, gain, w_gate_t, b_gates):
    T, D = x.shape
    G = w_gate_t.shape[0]
    tm = min(T, 512)
    return pl.pallas_call(
        _gates_kernel,
        out_shape=[jax.ShapeDtypeStruct((T, G), F32), jax.ShapeDtypeStruct((G, T), F32)],
        grid=(T // tm,),
        in_specs=[pl.BlockSpec((tm, D), lambda i: (i, 0)),
                  pl.BlockSpec((1, D), lambda i: (0, 0)),
                  pl.BlockSpec((G, D), lambda i: (0, 0)),
                  pl.BlockSpec((1, G), lambda i: (0, 0)),
                  pl.BlockSpec((G, 1), lambda i: (0, 0))],
        out_specs=[pl.BlockSpec((tm, G), lambda i: (i, 0)),
                   pl.BlockSpec((G, tm), lambda i: (0, i))],
        compiler_params=_params("parallel"),
        name="mlstm_gates",
    )(x, gain, w_gate_t, b_gates.reshape(1, G), b_gates.reshape(G, 1))


def _mlstm_kernel(q_ref, k_ref, v_ref, o_ref, gc_ref, gr_ref, on_ref,
                  h_ref, c_out, n_out, m_out, c_s, n_s, m_s, *, dk, dv):
    H = MLSTM_HEADS
    c = pl.program_id(1)
    L = q_ref.shape[0]

    @pl.when(c == 0)
    def _():
        c_s[...] = jnp.zeros_like(c_s)
        n_s[...] = jnp.zeros_like(n_s)
        m_s[...] = jnp.zeros_like(m_s)

    row = lax.broadcasted_iota(I32, (L, L), 0)
    col = lax.broadcasted_iota(I32, (L, L), 1)
    tril = row >= col
    for h in range(H):
        q = q_ref[:, h * dk:(h + 1) * dk]
        k = k_ref[:, h * dk:(h + 1) * dk] * (dk ** -0.5)
        v = v_ref[:, h * dv:(h + 1) * dv]
        ig_c = gc_ref[:, h:h + 1]
        lf_c = gc_ref[:, H + h:H + h + 1]
        ig_r = gr_ref[h:h + 1, :]
        lf_r = gr_ref[H + h:H + h + 1, :]
        b_c = jnp.sum(jnp.where(tril, lf_r, 0.0), axis=1, keepdims=True)
        b_r = jnp.sum(jnp.where(row <= col, lf_c, 0.0), axis=0, keepdims=True)
        m_prev = m_s[h][:, 0:1]
        C = c_s[h]
        n = n_s[h]
        dmat = jnp.where(tril, b_c - b_r + ig_r, -jnp.inf)
        inter = b_c + m_prev
        m_t = jnp.maximum(inter, jnp.max(dmat, axis=1, keepdims=True))
        w_intra = jnp.exp(dmat - m_t)
        w_inter = jnp.exp(inter - m_t)
        qb = q.astype(BF16)
        vb = v.astype(BF16)
        s = _dot_nt(qb, k.astype(BF16)) * w_intra
        num = w_inter * _dot(qb, C.astype(BF16)) + _dot(s.astype(BF16), vb)
        den = w_inter * jnp.sum(q * n, axis=1, keepdims=True) + jnp.sum(s, axis=1, keepdims=True)
        hh = num / jnp.maximum(jnp.abs(den), jnp.exp(-m_t))
        b_last = b_c[L - 1:L, :]
        g_r = b_last - b_r + ig_r
        g_c = b_last - b_c + ig_c
        m_new = jnp.maximum(b_last + m_prev, jnp.max(g_r, axis=1, keepdims=True))
        decay = jnp.exp(b_last + m_prev - m_new)
        kw = k * jnp.exp(g_c - m_new)
        c_s[h] = decay * C + _dot_tn(kw.astype(BF16), vb)
        n_s[h] = decay * n + jnp.sum(kw, axis=0, keepdims=True)
        m_s[h] = jnp.broadcast_to(m_new, m_s.shape[1:])
        sl = slice(h * dv, (h + 1) * dv)
        h_ref[:, sl] = (_rms(hh) * on_ref[:, sl] * _sigmoid(o_ref[:, sl])).astype(h_ref.dtype)

    @pl.when(c == pl.num_programs(1) - 1)
    def _():
        c_out[0] = c_s[...]
        n_out[0] = n_s[...]
        m_out[0] = m_s[...]


def _mlstm_prompt(proj, gc, gr, out_norm, B, S):
    H = MLSTM_HEADS
    T = B * S
    hv = out_norm.shape[1]
    dv = hv // H
    dk = dv // 2
    hk = H * dk
    L = MLSTM_CHUNK
    nc = S // L
    G = gc.shape[1]
    rows = lambda b, c: b * nc + c
    return pl.pallas_call(
        functools.partial(_mlstm_kernel, dk=dk, dv=dv),
        out_shape=[jax.ShapeDtypeStruct((T, hv), BF16),
                   jax.ShapeDtypeStruct((B, H, dk, dv), F32),
                   jax.ShapeDtypeStruct((B, H, 1, dk), F32),
                   jax.ShapeDtypeStruct((B, H, 1, 128), F32)],
        grid=(B, nc),
        in_specs=[pl.BlockSpec((L, hk), lambda b, c: (rows(b, c), 0)),
                  pl.BlockSpec((L, hk), lambda b, c: (rows(b, c), 1)),
                  pl.BlockSpec((L, hv), lambda b, c: (rows(b, c), 1)),
                  pl.BlockSpec((L, hv), lambda b, c: (rows(b, c), 2)),
                  pl.BlockSpec((L, G), lambda b, c: (rows(b, c), 0)),
                  pl.BlockSpec((G, L), lambda b, c: (0, rows(b, c))),
                  pl.BlockSpec((1, hv), lambda b, c: (0, 0))],
        out_specs=[pl.BlockSpec((L, hv), lambda b, c: (rows(b, c), 0)),
                   pl.BlockSpec((1, H, dk, dv), lambda b, c: (b, 0, 0, 0)),
                   pl.BlockSpec((1, H, 1, dk), lambda b, c: (b, 0, 0, 0)),
                   pl.BlockSpec((1, H, 1, 128), lambda b, c: (b, 0, 0, 0))],
        scratch_shapes=[pltpu.VMEM((H, dk, dv), F32), pltpu.VMEM((H, 1, dk), F32),
                        pltpu.VMEM((H, 1, 128), F32)],
        compiler_params=_params("arbitrary", "arbitrary"),
        name="mlstm_prompt",
    )(proj, proj, proj, proj, gc, gr, out_norm)


def _mlstm_step_kernel(qr_ref, qc_ref, kr_ref, kc_ref, v_ref, o_ref, ig_ref, lf_ref, on_ref,
                       c_ref, n_ref, m_ref, h_ref, c_out, n_out, m_out):
    H = MLSTM_HEADS
    dk = qr_ref.shape[-1]
    dv = v_ref.shape[-1]
    for h in range(H):
        q_r = qr_ref[0, h]
        q_c = qc_ref[0, h]
        k_r = kr_ref[0, h] * (dk ** -0.5)
        k_c = kc_ref[0, h] * (dk ** -0.5)
        v = v_ref[0, h]
        ig = ig_ref[0, h]
        lf = lf_ref[0, h]
        C = c_ref[0, h]
        n = n_ref[0, h]
        m = m_ref[0, h]
        inter = lf + m
        m_t = jnp.maximum(inter, ig)
        w_intra = jnp.exp(ig - m_t)
        w_inter = jnp.exp(inter - m_t)
        s = jnp.sum(q_r * k_r, axis=1, keepdims=True) * w_intra
        num = w_inter * jnp.sum(q_c * C, axis=0, keepdims=True) + s * v
        den = w_inter * jnp.sum(q_r * n, axis=1, keepdims=True) + s
        hh = num / jnp.maximum(jnp.abs(den), jnp.exp(-m_t))
        decay = jnp.exp(inter - m_t)
        w_k = jnp.exp(ig - m_t)
        c_out[0, h] = decay * C + (k_c * w_k) * v
        n_out[0, h] = decay * n + k_r * w_k
        m_out[0, h] = m_t
        sl = slice(h * dv, (h + 1) * dv)
        h_ref[0, :, sl] = _rms(hh) * on_ref[:, sl] * _sigmoid(o_ref[0, :, sl])


def _mlstm_decode(proj, gc, out_norm, C0, n0, m0):
    H = MLSTM_HEADS
    Bd, _, dk, dv = C0.shape
    hk, hv = H * dk, H * dv
    q = proj[:Bd, :hk].reshape(Bd, H, dk)
    k = proj[:Bd, hk:2 * hk].reshape(Bd, H, dk)
    v = proj[:Bd, 2 * hk:2 * hk + hv].reshape(Bd, H, 1, dv)
    o = proj[:Bd, 2 * hk + hv:2 * hk + 2 * hv].reshape(Bd, 1, hv)
    ig = gc[:Bd, :H].reshape(Bd, H, 1, 1)
    lf = gc[:Bd, H:].reshape(Bd, H, 1, 1)
    spec = lambda *tail: pl.BlockSpec((1, H) + tail, lambda b: (b, 0, 0, 0))
    return pl.pallas_call(
        _mlstm_step_kernel,
        out_shape=[jax.ShapeDtypeStruct((Bd, 1, hv), F32),
                   jax.ShapeDtypeStruct((Bd, H, dk, dv), F32),
                   jax.ShapeDtypeStruct((Bd, H, 1, dk), F32),
                   jax.ShapeDtypeStruct((Bd, H, 1, 1), F32)],
        grid=(Bd,),
        in_specs=[spec(1, dk), spec(dk, 1), spec(1, dk), spec(dk, 1), spec(1, dv),
                  pl.BlockSpec((1, 1, hv), lambda b: (b, 0, 0)),
                  spec(1, 1), spec(1, 1),
                  pl.BlockSpec((1, hv), lambda b: (0, 0)),
                  spec(dk, dv), spec(1, dk), spec(1, 1)],
        out_specs=[pl.BlockSpec((1, 1, hv), lambda b: (b, 0, 0)),
                   spec(dk, dv), spec(1, dk), spec(1, 1)],
        compiler_params=_params("parallel"),
        name="mlstm_decode",
    )(q.reshape(Bd, H, 1, dk), q.reshape(Bd, H, dk, 1), k.reshape(Bd, H, 1, dk),
      k.reshape(Bd, H, dk, 1), v, o, ig, lf, out_norm, C0, n0.reshape(Bd, H, 1, dk),
      m0.reshape(Bd, H, 1, 1))


def _rows(start, size, stride):
    return pl.ds(start, size) if stride == 1 else pl.ds(start, size, stride=stride)


def _attn_kernel(*refs, dilations, seq):
    G = len(dilations)
    q_refs, k_refs, v_refs = refs[:G], refs[G:2 * G], refs[2 * G:3 * G]
    out_ref = refs[3 * G]
    o_scr, lse_scr, s_scr, p_scr = refs[3 * G + 1:]
    bq = ATTN_BLOCK
    D = ATTN_HEAD_DIM
    scale = D ** -0.5 * math.log2(math.e)
    row = lax.broadcasted_iota(I32, (bq, 2 * bq), 0)
    col = lax.broadcasted_iota(I32, (bq, 2 * bq), 1)
    bias_full = jnp.where(jnp.logical_and(col - row >= 0, col - row <= bq), 0.0, -jnp.inf)
    row1 = lax.broadcasted_iota(I32, (bq, bq), 0)
    col1 = lax.broadcasted_iota(I32, (bq, bq), 1)
    bias_first = jnp.where(row1 >= col1, 0.0, -jnp.inf)
    for g, r in enumerate(dilations):
        nb = seq // (r * bq)
        blocks = [(rho, u) for rho in range(r) for u in range(nb)]
        for i, (rho, u) in enumerate(blocks):
            q = q_refs[g][_rows(rho + u * bq * r, bq, r), :].astype(BF16)
            if u == 0:
                k = k_refs[g][_rows(rho, bq, r), :].astype(BF16)
                s_scr[i, :, :bq] = _dot_nt(q, k) * scale + bias_first
            else:
                k = k_refs[g][_rows(rho + (u - 1) * bq * r, 2 * bq, r), :].astype(BF16)
                s_scr[i] = _dot_nt(q, k) * scale + bias_full
        for i, (rho, u) in enumerate(blocks):
            s = s_scr[i, :, :bq] if u == 0 else s_scr[i]
            mx = jnp.max(s, axis=1, keepdims=True)
            e = jnp.exp2(s - mx)
            l = jnp.sum(e, axis=1, keepdims=True)
            p = (e / l).astype(BF16)
            if u == 0:
                p_scr[i, :, :bq] = p
            else:
                p_scr[i] = p
            lse_scr[g, _rows(rho + u * bq * r, bq, r), :] = jnp.broadcast_to(mx + jnp.log2(l), (bq, D))
        for i, (rho, u) in enumerate(blocks):
            if u == 0:
                v = v_refs[g][_rows(rho, bq, r), :].astype(BF16)
                o = _dot(p_scr[i, :, :bq], v)
            else:
                v = v_refs[g][_rows(rho + (u - 1) * bq * r, 2 * bq, r), :].astype(BF16)
                o = _dot(p_scr[i], v)
            o_scr[g, _rows(rho + u * bq * r, bq, r), :] = o
    lses = [lse_scr[g] for g in range(G)]
    mx = functools.reduce(jnp.maximum, lses)
    es = [jnp.exp2(l - mx) for l in lses]
    tot = functools.reduce(lambda a, b: a + b, es)
    acc = (es[0] / tot) * o_scr[0]
    for g in range(1, G):
        acc = acc + (es[g] / tot) * o_scr[g]
    out_ref[...] = acc.astype(out_ref.dtype)


def _attention(q, kv, B, S, heads):
    D = ATTN_HEAD_DIM
    T = B * S
    G = len(DILATED_GROUPS)
    dil = tuple(d for _, d in DILATED_GROUPS)
    assert all(w // d == ATTN_BLOCK and S % (d * ATTN_BLOCK) == 0 for w, d in DILATED_GROUPS)
    nblk = S // ATTN_BLOCK
    spec = lambda col: pl.BlockSpec((S, D), lambda b, h: (b, col(h)))
    in_specs = ([spec(lambda h, g=g: g * heads + h) for g in range(G)]
                + [spec(lambda h, g=g: 2 * g * heads + h) for g in range(G)]
                + [spec(lambda h, g=g: (2 * g + 1) * heads + h) for g in range(G)])
    return pl.pallas_call(
        functools.partial(_attn_kernel, dilations=dil, seq=S),
        out_shape=jax.ShapeDtypeStruct((T, heads * D), BF16),
        grid=(B, heads),
        in_specs=in_specs,
        out_specs=pl.BlockSpec((S, D), lambda b, h: (b, h)),
        scratch_shapes=[pltpu.VMEM((G, S, D), F32), pltpu.VMEM((G, S, D), F32),
                        pltpu.VMEM((nblk, ATTN_BLOCK, 2 * ATTN_BLOCK), F32),
                        pltpu.VMEM((nblk, ATTN_BLOCK, 2 * ATTN_BLOCK), BF16)],
        compiler_params=_params("parallel", "parallel"),
        name="dilated_attention",
    )(*([q] * G), *([kv] * (2 * G)))


def _kv_rows_kernel(x_ref, o_ref, *, heads):
    o_ref[0] = x_ref[...].reshape(x_ref.shape[0], 2, heads, ATTN_HEAD_DIM)


def _kv_rows(kv, g, keep, B, S, heads):
    W2 = 2 * heads * ATTN_HEAD_DIM
    tm = min(keep, 256)
    first = (S - keep) // tm
    per_seq = S // tm
    return pl.pallas_call(
        functools.partial(_kv_rows_kernel, heads=heads),
        out_shape=jax.ShapeDtypeStruct((B, keep, 2, heads, ATTN_HEAD_DIM), F32),
        grid=(B, keep // tm),
        in_specs=[pl.BlockSpec((tm, W2), lambda b, i: (b * per_seq + first + i, g))],
        out_specs=pl.BlockSpec((1, tm, 2, heads, ATTN_HEAD_DIM), lambda b, i: (b, i, 0, 0, 0)),
        compiler_params=_params("parallel", "parallel"),
        name=f"kv_rows_g{g}",
    )(kv)


def _decode_attn_kernel(q_ref, kvn_ref, *refs):
    G = len(refs) - 1
    buf_refs = refs[:G]
    out_ref = refs[G]
    scale = ATTN_HEAD_DIM ** -0.5
    outs, lses = [], []
    for g in range(G):
        q = q_ref[g]
        k_new = kvn_ref[g, 0]
        v_new = kvn_ref[g, 1]
        k_buf = buf_refs[g][:, 0]
        v_buf = buf_refs[g][:, 1]
        s_buf = jnp.sum(k_buf * q[None], axis=-1, keepdims=True) * scale
        s_new = jnp.sum(k_new * q, axis=-1, keepdims=True) * scale
        mx = jnp.maximum(jnp.max(s_buf, axis=0), s_new)
        e_buf = jnp.exp(s_buf - mx[None])
        e_new = jnp.exp(s_new - mx)
        l = jnp.sum(e_buf, axis=0) + e_new
        outs.append(jnp.sum((e_buf / l[None]) * v_buf, axis=0) + (e_new / l) * v_new)
        lses.append(mx + jnp.log(l))
    mx = functools.reduce(jnp.maximum, lses)
    es = [jnp.exp(l - mx) for l in lses]
    tot = functools.reduce(lambda a, b: a + b, es)
    acc = (es[0] / tot) * outs[0]
    for g in range(1, G):
        acc = acc + (es[g] / tot) * outs[g]
    out_ref[...] = acc


def _decode_attention(q, kv_new, caches):
    Bd, G, H, D = q.shape
    steps = ATTN_BLOCK
    views = []
    for cache, (window, dilation) in zip(caches, DILATED_GROUPS):
        Lb = cache.shape[1]
        assert window // dilation == steps and Lb == steps * dilation
        views.append(cache.reshape(Bd, steps, dilation, 2, H, D))
    return pl.pallas_call(
        _decode_attn_kernel,
        out_shape=jax.ShapeDtypeStruct((Bd, H, D), F32),
        grid=(Bd,),
        in_specs=[pl.BlockSpec((None, G, H, D), lambda b: (b, 0, 0, 0)),
                  pl.BlockSpec((None, G, 2, H, D), lambda b: (b, 0, 0, 0, 0))]
                 + [pl.BlockSpec((None, steps, None, 2, H, D), lambda b: (b, 0, 0, 0, 0, 0)) for _ in views],
        out_specs=pl.BlockSpec((None, H, D), lambda b: (b, 0, 0)),
        compiler_params=_params("parallel"),
        name="decode_attention",
    )(q, kv_new, *views)


def _router_kernel(x_ref, wt_ref, b_ref, o_ref):
    logits = _dot_nt_3pass(x_ref[...], wt_ref[...]) + b_ref[...]
    e = jnp.exp(logits - jnp.max(logits, axis=1, keepdims=True))
    p = e / jnp.sum(e, axis=1, keepdims=True)
    E = p.shape[1]
    idx = lax.broadcasted_iota(I32, p.shape, 1)
    p1 = jnp.max(p, axis=1, keepdims=True)
    i1 = jnp.min(jnp.where(p == p1, idx, E), axis=1, keepdims=True)
    rest = jnp.where(idx == i1, -1.0, p)
    p2 = jnp.max(rest, axis=1, keepdims=True)
    i2 = jnp.min(jnp.where(rest == p2, idx, E), axis=1, keepdims=True)
    tot = p1 + p2
    o_ref[...] = jnp.where(idx == 0, i1.astype(F32),
                 jnp.where(idx == 1, i2.astype(F32),
                 jnp.where(idx == 2, p1 / tot,
                 jnp.where(idx == 3, p2 / tot, 0.0))))


def _router(xn, w_router_t, b_router):
    T, D = xn.shape
    E = w_router_t.shape[0]
    tm = _row_tile(T, 1024)
    return pl.pallas_call(
        _router_kernel,
        out_shape=jax.ShapeDtypeStruct((T, E), F32),
        grid=(T // tm,),
        in_specs=[pl.BlockSpec((tm, D), lambda i: (i, 0)),
                  pl.BlockSpec((E, D), lambda i: (0, 0)),
                  pl.BlockSpec((1, E), lambda i: (0, 0))],
        out_specs=pl.BlockSpec((tm, E), lambda i: (i, 0)),
        compiler_params=_params("parallel"),
        name="router",
    )(xn, w_router_t, b_router.reshape(1, E))


def _route_plan(top_i, tm, n_tiles):
    T = top_i.shape[0]
    e_flat = top_i.reshape(-1)
    onehot = (e_flat[:, None] == jnp.arange(N_EXPERTS, dtype=I32)[None, :]).astype(I32)
    csum = jnp.cumsum(onehot, axis=0)
    rank = jnp.sum(onehot * csum, axis=1) - 1
    counts = csum[-1]
    gsz = ((counts + tm - 1) // tm) * tm
    gend = jnp.cumsum(gsz)
    gstart = gend - gsz
    pos = (jnp.sum(onehot * gstart[None, :], axis=1) + rank).astype(I32)
    src = jnp.zeros((n_tiles * tm,), I32).at[pos].set(jnp.arange(TOP_K * T, dtype=I32) // TOP_K)
    tile_start = jnp.arange(n_tiles, dtype=I32) * tm
    tile_valid = (tile_start < gend[-1]).astype(I32)
    tile_e = jnp.sum((tile_start[:, None] >= gend[None, :]).astype(I32), axis=1)
    last_e = jnp.sum((gend[-1] - 1 >= gend).astype(I32))
    tile_e = jnp.where(tile_valid > 0, tile_e, last_e).astype(I32)
    return pos, src, tile_e, tile_valid


def _experts_kernel(te_ref, tv_ref, src_ref, x_hbm, wg_ref, wu_ref, wd_ref, o_ref, xf_ref, xb_ref, sem,
                    *, tm, per_step, accurate):
    i = pl.program_id(0)
    c = pl.program_id(1)
    n_tiles = pl.num_programs(0)
    n_rows = per_step * pl.num_programs(1)
    valid = tv_ref[i] > 0
    nxt = jnp.minimum(i + 1, n_tiles - 1)
    prefetch = jnp.logical_and(i + 1 < n_tiles, tv_ref[nxt] > 0)
    D = o_ref.shape[1]
    dn = min(D, 512)

    def row_copy(tile, r):
        src = src_ref[tile * tm + jnp.minimum(r, tm - 1)]
        return pltpu.make_async_copy(x_hbm.at[pl.ds(src, 1)], xf_ref.at[pl.ds(r, 1)], sem.at[0])

    @pl.when(jnp.logical_and(i == 0, c == 0))
    def _():
        def start(r, carry):
            row_copy(0, r).start()
            return carry

        lax.fori_loop(0, n_rows, start, 0)

    @pl.when(c == 0)
    def _():
        o_ref[...] = jnp.zeros_like(o_ref)

    @pl.when(jnp.logical_and(valid, c == 0))
    def _():
        def wait(r, carry):
            row_copy(i, r).wait()
            return carry

        lax.fori_loop(0, n_rows, wait, 0)
        xb_ref[...] = xf_ref[pl.ds(0, tm), :].astype(xb_ref.dtype)

    def mm(x, w):
        wh = w.astype(BF16)
        return _dot_3pass(x, w, wh) if accurate else _dot(x, wh)

    def chunk(with_prefetch):
        if with_prefetch:
            for j in range(per_step):
                row_copy(i + 1, c * per_step + j).start()
        x = xb_ref[...]
        g = mm(x, wg_ref[0])
        u = mm(x, wu_ref[0])
        act = (g * _sigmoid(g) * u).astype(xb_ref.dtype)
        for n0 in range(0, D, dn):
            o_ref[:, n0:n0 + dn] += mm(act, wd_ref[0, :, n0:n0 + dn])

    pl.when(jnp.logical_and(valid, prefetch))(functools.partial(chunk, True))
    pl.when(jnp.logical_and(valid, jnp.logical_not(prefetch)))(functools.partial(chunk, False))


def _experts(xn, pos_plan, w_gate, w_up, w_down, tm, tf, accurate=False):
    _, src, tile_e, tile_valid = pos_plan
    T, D = xn.shape
    Fe = w_gate.shape[-1]
    nf = Fe // tf
    n_tiles = tile_e.shape[0]
    per_step = -(-tm // nf)

    def chunk(i, c, tv):
        return jnp.where(tv[i] > 0, c, nf - 1)

    return pl.pallas_call(
        functools.partial(_experts_kernel, tm=tm, per_step=per_step, accurate=accurate),
        out_shape=jax.ShapeDtypeStruct((n_tiles * tm, D), F32),
        grid_spec=pltpu.PrefetchScalarGridSpec(
            num_scalar_prefetch=3,
            grid=(n_tiles, nf),
            in_specs=[pl.BlockSpec(memory_space=pl.ANY),
                      pl.BlockSpec((None, 1, D, tf), lambda i, c, te, tv, src: (0, te[i], 0, chunk(i, c, tv))),
                      pl.BlockSpec((None, 1, D, tf), lambda i, c, te, tv, src: (0, te[i], 0, chunk(i, c, tv))),
                      pl.BlockSpec((None, 1, tf, D), lambda i, c, te, tv, src: (0, te[i], chunk(i, c, tv), 0))],
            out_specs=pl.BlockSpec((tm, D), lambda i, c, te, tv, src: (i, 0)),
            scratch_shapes=[pltpu.VMEM((per_step * nf, D), F32), pltpu.VMEM((tm, D), F32 if accurate else BF16),
                            pltpu.SemaphoreType.DMA((1,))]),
        compiler_params=_params("arbitrary", "arbitrary"),
        name="experts",
    )(tile_e, tile_valid, src, xn, w_gate, w_up, w_down)


def _combine_kernel(pos_ref, h_ref, gate_ref, y_hbm, fg_ref, o_ref, ya_ref, yb_ref, sem, *, tm):
    i = pl.program_id(0)

    def copies(r):
        t = i * tm + r
        return (pltpu.make_async_copy(y_hbm.at[pl.ds(pos_ref[TOP_K * t], 1)],
                                      ya_ref.at[pl.ds(r, 1)], sem.at[0]),
                pltpu.make_async_copy(y_hbm.at[pl.ds(pos_ref[TOP_K * t + 1], 1)],
                                      yb_ref.at[pl.ds(r, 1)], sem.at[1]))

    def start(r, carry):
        a, b = copies(r)
        a.start()
        b.start()
        return carry

    def wait(r, carry):
        a, b = copies(r)
        a.wait()
        b.wait()
        return carry

    lax.fori_loop(0, tm, start, 0)
    lax.fori_loop(0, tm, wait, 0)
    hsum = h_ref[...] + (gate_ref[:, 0:1] * ya_ref[...] + gate_ref[:, 1:2] * yb_ref[...])
    o_ref[...] = _rms(hsum) * fg_ref[...]


def _combine(h, gates, y_sorted, pos, final_gain):
    T, D = h.shape
    tm = _row_tile(T, 256)
    return pl.pallas_call(
        functools.partial(_combine_kernel, tm=tm),
        out_shape=jax.ShapeDtypeStruct((T, D), F32),
        grid_spec=pltpu.PrefetchScalarGridSpec(
            num_scalar_prefetch=1,
            grid=(T // tm,),
            in_specs=[pl.BlockSpec((tm, D), lambda i, pos: (i, 0)),
                      pl.BlockSpec((tm, TOP_K), lambda i, pos: (i, 0)),
                      pl.BlockSpec(memory_space=pl.ANY),
                      pl.BlockSpec((1, D), lambda i, pos: (0, 0))],
            out_specs=pl.BlockSpec((tm, D), lambda i, pos: (i, 0)),
            scratch_shapes=[pltpu.VMEM((tm, D), F32), pltpu.VMEM((tm, D), F32),
                            pltpu.SemaphoreType.DMA((2,))]),
        compiler_params=_params("arbitrary"),
        name="moe_combine",
    )(pos, h, gates, y_sorted, final_gain)


def _forward(xp, xs, state, kv_caches, B, S, Bd, p):
    D = xp.shape[1]
    hv = p["mlstm_out_norm"].shape[-1]
    hk = hv // 2
    heads = p["attn_w_out"].shape[1] // ATTN_HEAD_DIM
    W = heads * ATTN_HEAD_DIM
    n_groups = len(DILATED_GROUPS)
    pad_rows = lambda a: jnp.zeros((SAMPLE_ROWS, a.shape[1]), F32).at[:Bd].set(a)

    gain0 = p["norm_mix"][0:1]
    w_gate_t = p["mlstm_w_in"][0, :, 2 * hk + 2 * hv:].T
    (xn,) = _norm(xp, gain0, [BF16])
    (xn_s,) = _norm(xs, gain0, [F32])
    proj, proj_s = _matmul(xn, xn_s, p["mlstm_w_in"], 2 * hk + 2 * hv, F32)
    gc, gr = _gates(xp, gain0, w_gate_t, p["mlstm_b_gates"][0])
    gc_s, _ = _gates(xs, gain0, w_gate_t, p["mlstm_b_gates"][0])
    hg, C_p, n_p, m_p = _mlstm_prompt(proj, gc, gr, p["mlstm_out_norm"], B, S)
    hg_s, C_s, n_s, m_s = _mlstm_decode(proj_s, gc_s, p["mlstm_out_norm"], *state)
    h1, h1_s = _matmul(hg, pad_rows(hg_s[:, 0]), p["mlstm_w_out"], D, F32, res=xp, res_tail=xs)

    (xn,) = _norm(h1, p["norm_ffn"][0:1], [BF16])
    (xn_s,) = _norm(h1_s, p["norm_ffn"][0:1], [F32])
    act, act_s = _glu(xn, xn_s, p["ffn_w_gate"], p["ffn_w_up"])
    h2, h2_s = _matmul(act, act_s, p["ffn_w_down"], D, F32, res=h1, res_tail=h1_s, tn=512)

    kvq_gains = jnp.concatenate([p["kv_norm"][None], p["norm_mix"][1:2]])
    xkv, xq = _norm(h2, kvq_gains, [BF16, BF16])
    xkv_s, xq_s = _norm(h2_s, kvq_gains, [F32, F32])
    kv, kv_s = _matmul(xkv, xkv_s, p["w_kv"], n_groups * 2 * W, F32)
    q, q_s = _matmul(xq, xq_s, p["attn_w_q"], n_groups * W, F32)
    att = _attention(q, kv, B, S, heads)
    kv_rows_p = [_kv_rows(kv, g, min(window, S), B, S, heads)
                 for g, (window, _) in enumerate(DILATED_GROUPS)]
    kv5 = kv_s[:Bd].reshape(Bd, n_groups, 2, heads, ATTN_HEAD_DIM)
    att_s = _decode_attention(q_s[:Bd].reshape(Bd, n_groups, heads, ATTN_HEAD_DIM), kv5, kv_caches)
    kv_rows_s = [kv5[:, None, g] for g in range(n_groups)]
    h3, h3_s = _matmul(att, pad_rows(att_s.reshape(Bd, W)), p["attn_w_out"], D, F32, res=h2, res_tail=h2_s)

    y_p = _moe_and_final_norm(h3, p, 1024, False)
    y_s = _moe_and_final_norm(h3_s, p, SAMPLE_ROWS, True)

    states_p = (C_p[None], n_p[:, :, 0][None], m_p[:, :, 0, 0][None])
    states_s = (C_s[None], n_s[:, :, 0][None], m_s[:, :, 0, 0][None])
    return y_p, y_s[:Bd], states_p, kv_rows_p, states_s, kv_rows_s


def _moe_and_final_norm(h, p, tm_e, accurate):
    T = h.shape[0]
    (xn,) = _norm(h, p["norm_ffn"][1:2], [F32])
    route = _router(xn, p["moe_w_router"][0].T, p["moe_b_router"][0])
    top_i = route[:, :TOP_K].astype(I32)
    gates = route[:, TOP_K:2 * TOP_K]
    n_tiles = -(-(TOP_K * T) // tm_e) + N_EXPERTS
    plan = _route_plan(top_i, tm_e, n_tiles)
    y_sorted = _experts(xn, plan, p["moe_w_gate"], p["moe_w_up"], p["moe_w_down"], tm_e, 256, accurate)
    return _combine(h, gates, y_sorted, plan[0], p["final_norm"][None])


def kernel(x_prompt, x_sample, state_mlstm_C, state_mlstm_n, state_mlstm_m, cache_kv_w128, cache_kv_w512, cache_kv_w2048, norm_mix, norm_ffn, mlstm_w_in, mlstm_b_gates, mlstm_out_norm, mlstm_w_out, kv_norm, w_kv, attn_w_q, attn_w_out, ffn_w_gate, ffn_w_up, ffn_w_down, moe_w_router, moe_b_router, moe_w_gate, moe_w_up, moe_w_down, final_norm):
    assert norm_mix.shape[0] == 2 and mlstm_w_in.shape[0] == 1 and attn_w_q.shape[0] == 1
    p = dict(norm_mix=norm_mix, norm_ffn=norm_ffn, mlstm_w_in=mlstm_w_in, mlstm_b_gates=mlstm_b_gates,
             mlstm_out_norm=mlstm_out_norm, mlstm_w_out=mlstm_w_out, kv_norm=kv_norm, w_kv=w_kv,
             attn_w_q=attn_w_q, attn_w_out=attn_w_out, ffn_w_gate=ffn_w_gate, ffn_w_up=ffn_w_up,
             ffn_w_down=ffn_w_down, moe_w_router=moe_w_router, moe_b_router=moe_b_router,
             moe_w_gate=moe_w_gate, moe_w_up=moe_w_up, moe_w_down=moe_w_down, final_norm=final_norm)
    B, S, D = x_prompt.shape
    Bd = x_sample.shape[0]
    assert x_sample.shape[1] == 1 and Bd <= SAMPLE_ROWS

    xs = jnp.zeros((SAMPLE_ROWS, D), F32).at[:Bd].set(x_sample[:, 0])
    state = (state_mlstm_C[0], state_mlstm_n[0], state_mlstm_m[0])
    y_p, y_s, states_p, kv_rows_p, states_s, kv_rows_s = _forward(
        x_prompt.reshape(B * S, D), xs, state, (cache_kv_w128, cache_kv_w512, cache_kv_w2048), B, S, Bd, p)
    return (y_p.reshape(B, S, D), y_s[:, None], *states_p, *kv_rows_p, *states_s, *kv_rows_s)
```

```python
import functools
import math

import jax
import jax.numpy as jnp
from jax import lax
from jax.experimental import pallas as pl
from jax.experimental.pallas import tpu as pltpu

F32 = jnp.float32
BF16 = jnp.bfloat16
I32 = jnp.int32

EPS = 1e-6
GATE_SOFTCAP = 15.0
MLSTM_HEADS = 4
ATTN_HEAD_DIM = 128
DILATED_GROUPS = ((128, 1), (512, 4), (2048, 16))
N_EXPERTS = 8
TOP_K = 2
MLSTM_CHUNK = 256
ATTN_BLOCK = 128
SAMPLE_ROWS = 16
EXPERT_CHUNK = 256
PROMPT_EXPERT_ROWS = 1056

VMEM_LIMIT = 56 * 1024 * 1024


def _params(*sem):
    return pltpu.CompilerParams(dimension_semantics=sem, vmem_limit_bytes=VMEM_LIMIT)


def _dot(a, b):
    return jnp.dot(a, b, preferred_element_type=F32)


def _dot_nt(a, b):
    return lax.dot_general(a, b, (((1,), (1,)), ((), ())), preferred_element_type=F32)


def _dot_tn(a, b):
    return lax.dot_general(a, b, (((0,), (0,)), ((), ())), preferred_element_type=F32)


def _split(x):
    hi = x.astype(BF16)
    lo = (x - hi.astype(F32)).astype(BF16)
    return hi, lo


def _dot_nt_3pass(x, w):
    xh, xl = _split(x)
    wh, wl = _split(w)
    return _dot_nt(xh, wh) + (_dot_nt(xl, wh) + _dot_nt(xh, wl))


def _dot_3pass(x, w, wh, cols=256):
    n = x.shape[0]
    xh, xl = _split(x)
    xs = jnp.concatenate([xh, xl], axis=0)
    outs = []
    for c0 in range(0, w.shape[1], cols):
        wh_c = wh[:, c0:c0 + cols]
        wl_c = (w[:, c0:c0 + cols] - wh_c.astype(F32)).astype(BF16)
        r = _dot(xs, wh_c)
        outs.append(r[:n] + (r[n:] + _dot(xh, wl_c)))
    return outs[0] if len(outs) == 1 else jnp.concatenate(outs, axis=1)


def _stack_tail(a_ref, at_ref, as_ref):
    tm = a_ref.shape[0]
    n = at_ref.shape[0]
    xh, xl = _split(at_ref[...])
    as_ref[0:tm, :] = a_ref[...]
    as_ref[tm:tm + n, :] = xh
    as_ref[tm + n:, :] = xl
    return xh


def _dot_stacked(as_ref, xh, w_ref, wh_ref, cols=256):
    n = xh.shape[0]
    tm = as_ref.shape[0] - 2 * n
    acc = _dot(as_ref[...], wh_ref[...])
    los = []
    for c0 in range(0, w_ref.shape[1], cols):
        wl_c = (w_ref[:, c0:c0 + cols] - wh_ref[:, c0:c0 + cols].astype(F32)).astype(BF16)
        los.append(_dot(xh, wl_c))
    lo = los[0] if len(los) == 1 else jnp.concatenate(los, axis=1)
    return acc[:tm], acc[tm:tm + n] + (acc[tm + n:] + lo)


def _sigmoid(x):
    return 1.0 / (1.0 + jnp.exp(-x))


def _rms(x):
    return x * lax.rsqrt(jnp.mean(x * x, axis=-1, keepdims=True) + EPS)


def _norm_kernel(x_ref, g_ref, *o_refs):
    y = _rms(x_ref[...])
    for i, o_ref in enumerate(o_refs):
        o_ref[...] = (y * g_ref[i:i + 1, :]).astype(o_ref.dtype)


def _row_tile(M, cap):
    t = (min(cap, M) // 16) * 16
    while M % t:
        t -= 16
    return t


def _norm(x, gains, dtypes):
    T, D = x.shape
    tm = _row_tile(T, 1024)
    return pl.pallas_call(
        _norm_kernel,
        out_shape=[jax.ShapeDtypeStruct((T, D), dt) for dt in dtypes],
        grid=(T // tm,),
        in_specs=[pl.BlockSpec((tm, D), lambda i: (i, 0)),
                  pl.BlockSpec(gains.shape, lambda i: (0, 0))],
        out_specs=[pl.BlockSpec((tm, D), lambda i: (i, 0)) for _ in dtypes],
        compiler_params=_params("parallel"),
        name="rmsnorm",
    )(x, gains)


def _mm_rows(M, K):
    return _row_tile(M, 1024 if K <= 4096 else 512)


def _weight_spec(w, tn):
    if w.ndim == 3:
        return pl.BlockSpec((None, w.shape[1], tn), lambda j, i: (0, 0, j))
    return pl.BlockSpec((w.shape[0], tn), lambda j, i: (0, j))


def _mm_kernel(a_ref, at_ref, w_ref, *rest, has_res):
    if has_res:
        r_ref, rt_ref, o_ref, ot_ref, wb_ref, as_ref = rest
    else:
        o_ref, ot_ref, wb_ref, as_ref = rest
    i = pl.program_id(1)
    last = pl.num_programs(1) - 1

    @pl.when(i == 0)
    def _():
        wb_ref[...] = w_ref[...].astype(BF16)

    @pl.when(i < last)
    def _():
        acc = _dot(a_ref[...], wb_ref[...])
        o_ref[...] = (r_ref[...] + acc if has_res else acc).astype(o_ref.dtype)

    @pl.when(i == last)
    def _():
        xh = _stack_tail(a_ref, at_ref, as_ref)
        acc, acc_t = _dot_stacked(as_ref, xh, w_ref, wb_ref)
        o_ref[...] = (r_ref[...] + acc if has_res else acc).astype(o_ref.dtype)
        ot_ref[...] = rt_ref[...] + acc_t if has_res else acc_t


def _matmul(a, a_tail, w, n_cols, out_dtype, res=None, res_tail=None, tn=1024):
    M, K = a.shape
    n = a_tail.shape[0]
    tm = _mm_rows(M, K)
    tn = min(tn, n_cols)
    in_specs = [pl.BlockSpec((tm, K), lambda j, i: (i, 0)),
                pl.BlockSpec((n, K), lambda j, i: (0, 0)), _weight_spec(w, tn)]
    args = [a, a_tail, w]
    if res is not None:
        in_specs += [pl.BlockSpec((tm, tn), lambda j, i: (i, j)), pl.BlockSpec((n, tn), lambda j, i: (0, j))]
        args += [res, res_tail]
    return pl.pallas_call(
        functools.partial(_mm_kernel, has_res=res is not None),
        out_shape=[jax.ShapeDtypeStruct((M, n_cols), out_dtype), jax.ShapeDtypeStruct((n, n_cols), F32)],
        grid=(n_cols // tn, M // tm),
        in_specs=in_specs,
        out_specs=[pl.BlockSpec((tm, tn), lambda j, i: (i, j)), pl.BlockSpec((n, tn), lambda j, i: (0, j))],
        scratch_shapes=[pltpu.VMEM((K, tn), BF16), pltpu.VMEM((tm + 2 * n, K), BF16)],
        compiler_params=_params("arbitrary", "arbitrary"),
        name="matmul",
    )(*args)


def _glu_kernel(a_ref, at_ref, wg_ref, wu_ref, o_ref, ot_ref, wgb_ref, wub_ref, as_ref):
    i = pl.program_id(1)
    last = pl.num_programs(1) - 1

    @pl.when(i == 0)
    def _():
        wgb_ref[...] = wg_ref[...].astype(BF16)
        wub_ref[...] = wu_ref[...].astype(BF16)

    @pl.when(i < last)
    def _():
        a = a_ref[...]
        g = _dot(a, wgb_ref[...])
        u = _dot(a, wub_ref[...])
        o_ref[...] = (g * _sigmoid(g) * u).astype(o_ref.dtype)

    @pl.when(i == last)
    def _():
        xh = _stack_tail(a_ref, at_ref, as_ref)
        g, g_t = _dot_stacked(as_ref, xh, wg_ref, wgb_ref)
        u, u_t = _dot_stacked(as_ref, xh, wu_ref, wub_ref)
        o_ref[...] = (g * _sigmoid(g) * u).astype(o_ref.dtype)
        ot_ref[...] = g_t * _sigmoid(g_t) * u_t


def _glu(a, a_tail, wg, wu, tf=512):
    M, K = a.shape
    n = a_tail.shape[0]
    F = wg.shape[-1]
    tm = _mm_rows(M, K)
    return pl.pallas_call(
        _glu_kernel,
        out_shape=[jax.ShapeDtypeStruct((M, F), BF16), jax.ShapeDtypeStruct((n, F), F32)],
        grid=(F // tf, M // tm),
        in_specs=[pl.BlockSpec((tm, K), lambda j, i: (i, 0)),
                  pl.BlockSpec((n, K), lambda j, i: (0, 0)), _weight_spec(wg, tf), _weight_spec(wu, tf)],
        out_specs=[pl.BlockSpec((tm, tf), lambda j, i: (i, j)), pl.BlockSpec((n, tf), lambda j, i: (0, j))],
        scratch_shapes=[pltpu.VMEM((K, tf), BF16), pltpu.VMEM((K, tf), BF16),
                        pltpu.VMEM((tm + 2 * n, K), BF16)],
        compiler_params=_params("arbitrary", "arbitrary"),
        name="swiglu_up",
    )(a, a_tail, wg, wu)


def _gates_kernel(x_ref, g_ref, wt_ref, bc_ref, br_ref, gc_ref, gr_ref):
    H = MLSTM_HEADS
    xn = _rms(x_ref[...]) * g_ref[...]
    wt = wt_ref[...]

    def finish(pre, head_axis):
        t = GATE_SOFTCAP * jnp.tanh(pre / GATE_SOFTCAP)
        ls = jnp.minimum(t, 0.0) - jnp.log1p(jnp.exp(-jnp.abs(t)))
        idx = lax.broadcasted_iota(I32, t.shape, head_axis)
        return jnp.where(idx < H, t, ls)

    gc_ref[...] = finish(_dot_nt_3pass(xn, wt) + bc_ref[...], 1)
    gr_ref[...] = finish(_dot_nt_3pass(wt, xn) + br_ref[...], 0)


def _gates(x, gain, w_gate_t, b_gates):
    T, D = x.shape
    G = w_gate_t.shape[0]
    tm = min(T, 512)
    return pl.pallas_call(
        _gates_kernel,
        out_shape=[jax.ShapeDtypeStruct((T, G), F32), jax.ShapeDtypeStruct((G, T), F32)],
        grid=(T // tm,),
        in_specs=[pl.BlockSpec((tm, D), lambda i: (i, 0)),
                  pl.BlockSpec((1, D), lambda i: (0, 0)),
                  pl.BlockSpec((G, D), lambda i: (0, 0)),
                  pl.BlockSpec((1, G), lambda i: (0, 0)),
                  pl.BlockSpec((G, 1), lambda i: (0, 0))],
        out_specs=[pl.BlockSpec((tm, G), lambda i: (i, 0)),
                   pl.BlockSpec((G, tm), lambda i: (0, i))],
        compiler_params=_params("parallel"),
        name="mlstm_gates",
    )(x, gain, w_gate_t, b_gates.reshape(1, G), b_gates.reshape(G, 1))


def _mlstm_kernel(q_ref, k_ref, v_ref, o_ref, gc_ref, gr_ref, on_ref,
                  h_ref, c_out, n_out, m_out, c_s, n_s, m_s, *, dk, dv):
    H = MLSTM_HEADS
    c = pl.program_id(1)
    L = q_ref.shape[0]

    @pl.when(c == 0)
    def _():
        c_s[...] = jnp.zeros_like(c_s)
        n_s[...] = jnp.zeros_like(n_s)
        m_s[...] = jnp.zeros_like(m_s)

    row = lax.broadcasted_iota(I32, (L, L), 0)
    col = lax.broadcasted_iota(I32, (L, L), 1)
    tril = row >= col
    for h in range(H):
        q = q_ref[:, h * dk:(h + 1) * dk]
        k = k_ref[:, h * dk:(h + 1) * dk] * (dk ** -0.5)
        v = v_ref[:, h * dv:(h + 1) * dv]
        ig_c = gc_ref[:, h:h + 1]
        lf_c = gc_ref[:, H + h:H + h + 1]
        ig_r = gr_ref[h:h + 1, :]
        lf_r = gr_ref[H + h:H + h + 1, :]
        b_c = jnp.sum(jnp.where(tril, lf_r, 0.0), axis=1, keepdims=True)
        b_r = jnp.sum(jnp.where(row <= col, lf_c, 0.0), axis=0, keepdims=True)
        m_prev = m_s[h][:, 0:1]
        C = c_s[h]
        n = n_s[h]
        dmat = jnp.where(tril, b_c - b_r + ig_r, -jnp.inf)
        inter = b_c + m_prev
        m_t = jnp.maximum(inter, jnp.max(dmat, axis=1, keepdims=True))
        w_intra = jnp.exp(dmat - m_t)
        w_inter = jnp.exp(inter - m_t)
        qb = q.astype(BF16)
        vb = v.astype(BF16)
        s = _dot_nt(qb, k.astype(BF16)) * w_intra
        num = w_inter * _dot(qb, C.astype(BF16)) + _dot(s.astype(BF16), vb)
        den = w_inter * jnp.sum(q * n, axis=1, keepdims=True) + jnp.sum(s, axis=1, keepdims=True)
        hh = num / jnp.maximum(jnp.abs(den), jnp.exp(-m_t))
        b_last = b_c[L - 1:L, :]
        g_r = b_last - b_r + ig_r
        g_c = b_last - b_c + ig_c
        m_new = jnp.maximum(b_last + m_prev, jnp.max(g_r, axis=1, keepdims=True))
        decay = jnp.exp(b_last + m_prev - m_new)
        kw = k * jnp.exp(g_c - m_new)
        c_s[h] = decay * C + _dot_tn(kw.astype(BF16), vb)
        n_s[h] = decay * n + jnp.sum(kw, axis=0, keepdims=True)
        m_s[h] = jnp.broadcast_to(m_new, m_s.shape[1:])
        sl = slice(h * dv, (h + 1) * dv)
        h_ref[:, sl] = (_rms(hh) * on_ref[:, sl] * _sigmoid(o_ref[:, sl])).astype(h_ref.dtype)

    @pl.when(c == pl.num_programs(1) - 1)
    def _():
        c_out[0] = c_s[...]
        n_out[0] = n_s[...]
        m_out[0] = m_s[...]


def _mlstm_prompt(proj, gc, gr, out_norm, B, S):
    H = MLSTM_HEADS
    T = B * S
    hv = out_norm.shape[1]
    dv = hv // H
    dk = dv // 2
    hk = H * dk
    L = MLSTM_CHUNK
    nc = S // L
    G = gc.shape[1]
    rows = lambda b, c: b * nc + c
    return pl.pallas_call(
        functools.partial(_mlstm_kernel, dk=dk, dv=dv),
        out_shape=[jax.ShapeDtypeStruct((T, hv), BF16),
                   jax.ShapeDtypeStruct((B, H, dk, dv), F32),
                   jax.ShapeDtypeStruct((B, H, 1, dk), F32),
                   jax.ShapeDtypeStruct((B, H, 1, 128), F32)],
        grid=(B, nc),
        in_specs=[pl.BlockSpec((L, hk), lambda b, c: (rows(b, c), 0)),
                  pl.BlockSpec((L, hk), lambda b, c: (rows(b, c), 1)),
                  pl.BlockSpec((L, hv), lambda b, c: (rows(b, c), 1)),
                  pl.BlockSpec((L, hv), lambda b, c: (rows(b, c), 2)),
                  pl.BlockSpec((L, G), lambda b, c: (rows(b, c), 0)),
                  pl.BlockSpec((G, L), lambda b, c: (0, rows(b, c))),
                  pl.BlockSpec((1, hv), lambda b, c: (0, 0))],
        out_specs=[pl.BlockSpec((L, hv), lambda b, c: (rows(b, c), 0)),
                   pl.BlockSpec((1, H, dk, dv), lambda b, c: (b, 0, 0, 0)),
                   pl.BlockSpec((1, H, 1, dk), lambda b, c: (b, 0, 0, 0)),
                   pl.BlockSpec((1, H, 1, 128), lambda b, c: (b, 0, 0, 0))],
        scratch_shapes=[pltpu.VMEM((H, dk, dv), F32), pltpu.VMEM((H, 1, dk), F32),
                        pltpu.VMEM((H, 1, 128), F32)],
        compiler_params=_params("arbitrary", "arbitrary"),
        name="mlstm_prompt",
    )(proj, proj, proj, proj, gc, gr, out_norm)


def _mlstm_step_kernel(qr_ref, qc_ref, kr_ref, kc_ref, v_ref, o_ref, ig_ref, lf_ref, on_ref,
                       c_ref, n_ref, m_ref, h_ref, c_out, n_out, m_out):
    H = MLSTM_HEADS
    dk = qr_ref.shape[-1]
    dv = v_ref.shape[-1]
    for h in range(H):
        q_r = qr_ref[0, h]
        q_c = qc_ref[0, h]
        k_r = kr_ref[0, h] * (dk ** -0.5)
        k_c = kc_ref[0, h] * (dk ** -0.5)
        v = v_ref[0, h]
        ig = ig_ref[0, h]
        lf = lf_ref[0, h]
        C = c_ref[0, h]
        n = n_ref[0, h]
        m = m_ref[0, h]
        inter = lf + m
        m_t = jnp.maximum(inter, ig)
        w_intra = jnp.exp(ig - m_t)
        w_inter = jnp.exp(inter - m_t)
        s = jnp.sum(q_r * k_r, axis=1, keepdims=True) * w_intra
        num = w_inter * jnp.sum(q_c * C, axis=0, keepdims=True) + s * v
        den = w_inter * jnp.sum(q_r * n, axis=1, keepdims=True) + s
        hh = num / jnp.maximum(jnp.abs(den), jnp.exp(-m_t))
        decay = jnp.exp(inter - m_t)
        w_k = jnp.exp(ig - m_t)
        c_out[0, h] = decay * C + (k_c * w_k) * v
        n_out[0, h] = decay * n + k_r * w_k
        m_out[0, h] = m_t
        sl = slice(h * dv, (h + 1) * dv)
        h_ref[0, :, sl] = _rms(hh) * on_ref[:, sl] * _sigmoid(o_ref[0, :, sl])


def _mlstm_decode(proj, gc, out_norm, C0, n0, m0):
    H = MLSTM_HEADS
    Bd, _, dk, dv = C0.shape
    hk, hv = H * dk, H * dv
    q = proj[:Bd, :hk].reshape(Bd, H, dk)
    k = proj[:Bd, hk:2 * hk].reshape(Bd, H, dk)
    v = proj[:Bd, 2 * hk:2 * hk + hv].reshape(Bd, H, 1, dv)
    o = proj[:Bd, 2 * hk + hv:2 * hk + 2 * hv].reshape(Bd, 1, hv)
    ig = gc[:Bd, :H].reshape(Bd, H, 1, 1)
    lf = gc[:Bd, H:].reshape(Bd, H, 1, 1)
    spec = lambda *tail: pl.BlockSpec((1, H) + tail, lambda b: (b, 0, 0, 0))
    return pl.pallas_call(
        _mlstm_step_kernel,
        out_shape=[jax.ShapeDtypeStruct((Bd, 1, hv), F32),
                   jax.ShapeDtypeStruct((Bd, H, dk, dv), F32),
                   jax.ShapeDtypeStruct((Bd, H, 1, dk), F32),
                   jax.ShapeDtypeStruct((Bd, H, 1, 1), F32)],
        grid=(Bd,),
        in_specs=[spec(1, dk), spec(dk, 1), spec(1, dk), spec(dk, 1), spec(1, dv),
                  pl.BlockSpec((1, 1, hv), lambda b: (b, 0, 0)),
                  spec(1, 1), spec(1, 1),
                  pl.BlockSpec((1, hv), lambda b: (0, 0)),
                  spec(dk, dv), spec(1, dk), spec(1, 1)],
        out_specs=[pl.BlockSpec((1, 1, hv), lambda b: (b, 0, 0)),
                   spec(dk, dv), spec(1, dk), spec(1, 1)],
        compiler_params=_params("parallel"),
        name="mlstm_decode",
    )(q.reshape(Bd, H, 1, dk), q.reshape(Bd, H, dk, 1), k.reshape(Bd, H, 1, dk),
      k.reshape(Bd, H, dk, 1), v, o, ig, lf, out_norm, C0, n0.reshape(Bd, H, 1, dk),
      m0.reshape(Bd, H, 1, 1))


def _rows(start, size, stride):
    return pl.ds(start, size) if stride == 1 else pl.ds(start, size, stride=stride)


def _attn_kernel(*refs, dilations, seq):
    G = len(dilations)
    q_refs, k_refs, v_refs = refs[:G], refs[G:2 * G], refs[2 * G:3 * G]
    out_ref = refs[3 * G]
    o_scr, lse_scr, s_scr, p_scr = refs[3 * G + 1:]
    bq = ATTN_BLOCK
    D = ATTN_HEAD_DIM
    scale = D ** -0.5 * math.log2(math.e)
    row = lax.broadcasted_iota(I32, (bq, 2 * bq), 0)
    col = lax.broadcasted_iota(I32, (bq, 2 * bq), 1)
    bias_full = jnp.where(jnp.logical_and(col - row >= 0, col - row <= bq), 0.0, -jnp.inf)
    row1 = lax.broadcasted_iota(I32, (bq, bq), 0)
    col1 = lax.broadcasted_iota(I32, (bq, bq), 1)
    bias_first = jnp.where(row1 >= col1, 0.0, -jnp.inf)
    for g, r in enumerate(dilations):
        nb = seq // (r * bq)
        blocks = [(rho, u) for rho in range(r) for u in range(nb)]
        for i, (rho, u) in enumerate(blocks):
            q = q_refs[g][_rows(rho + u * bq * r, bq, r), :].astype(BF16)
            if u == 0:
                k = k_refs[g][_rows(rho, bq, r), :].astype(BF16)
                s_scr[i, :, :bq] = _dot_nt(q, k) * scale + bias_first
            else:
                k = k_refs[g][_rows(rho + (u - 1) * bq * r, 2 * bq, r), :].astype(BF16)
                s_scr[i] = _dot_nt(q, k) * scale + bias_full
        for i, (rho, u) in enumerate(blocks):
            s = s_scr[i, :, :bq] if u == 0 else s_scr[i]
            mx = jnp.max(s, axis=1, keepdims=True)
            e = jnp.exp2(s - mx)
            l = jnp.sum(e, axis=1, keepdims=True)
            p = (e / l).astype(BF16)
            if u == 0:
                p_scr[i, :, :bq] = p
            else:
                p_scr[i] = p
            lse_scr[g, _rows(rho + u * bq * r, bq, r), :] = jnp.broadcast_to(mx + jnp.log2(l), (bq, D))
        for i, (rho, u) in enumerate(blocks):
            if u == 0:
                v = v_refs[g][_rows(rho, bq, r), :].astype(BF16)
                o = _dot(p_scr[i, :, :bq], v)
            else:
                v = v_refs[g][_rows(rho + (u - 1) * bq * r, 2 * bq, r), :].astype(BF16)
                o = _dot(p_scr[i], v)
            o_scr[g, _rows(rho + u * bq * r, bq, r), :] = o
    lses = [lse_scr[g] for g in range(G)]
    mx = functools.reduce(jnp.maximum, lses)
    es = [jnp.exp2(l - mx) for l in lses]
    tot = functools.reduce(lambda a, b: a + b, es)
    acc = (es[0] / tot) * o_scr[0]
    for g in range(1, G):
        acc = acc + (es[g] / tot) * o_scr[g]
    out_ref[...] = acc.astype(out_ref.dtype)


def _attention(q, kv, B, S, heads):
    D = ATTN_HEAD_DIM
    T = B * S
    G = len(DILATED_GROUPS)
    dil = tuple(d for _, d in DILATED_GROUPS)
    assert all(w // d == ATTN_BLOCK and S % (d * ATTN_BLOCK) == 0 for w, d in DILATED_GROUPS)
    nblk = S // ATTN_BLOCK
    spec = lambda col: pl.BlockSpec((S, D), lambda b, h: (b, col(h)))
    in_specs = ([spec(lambda h, g=g: g * heads + h) for g in range(G)]
                + [spec(lambda h, g=g: 2 * g * heads + h) for g in range(G)]
                + [spec(lambda h, g=g: (2 * g + 1) * heads + h) for g in range(G)])
    return pl.pallas_call(
        functools.partial(_attn_kernel, dilations=dil, seq=S),
        out_shape=jax.ShapeDtypeStruct((T, heads * D), BF16),
        grid=(B, heads),
        in_specs=in_specs,
        out_specs=pl.BlockSpec((S, D), lambda b, h: (b, h)),
        scratch_shapes=[pltpu.VMEM((G, S, D), F32), pltpu.VMEM((G, S, D), F32),
                        pltpu.VMEM((nblk, ATTN_BLOCK, 2 * ATTN_BLOCK), F32),
                        pltpu.VMEM((nblk, ATTN_BLOCK, 2 * ATTN_BLOCK), BF16)],
        compiler_params=_params("parallel", "parallel"),
        name="dilated_attention",
    )(*([q] * G), *([kv] * (2 * G)))


def _kv_rows_kernel(x_ref, o_ref, *, heads):
    o_ref[0] = x_ref[...].reshape(x_ref.shape[0], 2, heads, ATTN_HEAD_DIM)


def _kv_rows(kv, g, keep, B, S, heads):
    W2 = 2 * heads * ATTN_HEAD_DIM
    tm = min(keep, 256)
    first = (S - keep) // tm
    per_seq = S // tm
    return pl.pallas_call(
        functools.partial(_kv_rows_kernel, heads=heads),
        out_shape=jax.ShapeDtypeStruct((B, keep, 2, heads, ATTN_HEAD_DIM), F32),
        grid=(B, keep // tm),
        in_specs=[pl.BlockSpec((tm, W2), lambda b, i: (b * per_seq + first + i, g))],
        out_specs=pl.BlockSpec((1, tm, 2, heads, ATTN_HEAD_DIM), lambda b, i: (b, i, 0, 0, 0)),
        compiler_params=_params("parallel", "parallel"),
        name=f"kv_rows_g{g}",
    )(kv)


def _decode_attn_kernel(q_ref, kvn_ref, *refs):
    G = len(refs) - 1
    buf_refs = refs[:G]
    out_ref = refs[G]
    scale = ATTN_HEAD_DIM ** -0.5
    outs, lses = [], []
    for g in range(G):
        q = q_ref[g]
        k_new = kvn_ref[g, 0]
        v_new = kvn_ref[g, 1]
        k_buf = buf_refs[g][:, 0]
        v_buf = buf_refs[g][:, 1]
        s_buf = jnp.sum(k_buf * q[None], axis=-1, keepdims=True) * scale
        s_new = jnp.sum(k_new * q, axis=-1, keepdims=True) * scale
        mx = jnp.maximum(jnp.max(s_buf, axis=0), s_new)
        e_buf = jnp.exp(s_buf - mx[None])
        e_new = jnp.exp(s_new - mx)
        l = jnp.sum(e_buf, axis=0) + e_new
        outs.append(jnp.sum((e_buf / l[None]) * v_buf, axis=0) + (e_new / l) * v_new)
        lses.append(mx + jnp.log(l))
    mx = functools.reduce(jnp.maximum, lses)
    es = [jnp.exp(l - mx) for l in lses]
    tot = functools.reduce(lambda a, b: a + b, es)
    acc = (es[0] / tot) * outs[0]
    for g in range(1, G):
        acc = acc + (es[g] / tot) * outs[g]
    out_ref[...] = acc


def _decode_attention(q, kv_new, caches):
    Bd, G, H, D = q.shape
    steps = ATTN_BLOCK
    views = []
    for cache, (window, dilation) in zip(caches, DILATED_GROUPS):
        Lb = cache.shape[1]
        assert window // dilation == steps and Lb == steps * dilation
        views.append(cache.reshape(Bd, steps, dilation, 2, H, D))
    return pl.pallas_call(
        _decode_attn_kernel,
        out_shape=jax.ShapeDtypeStruct((Bd, H, D), F32),
        grid=(Bd,),
        in_specs=[pl.BlockSpec((None, G, H, D), lambda b: (b, 0, 0, 0)),
                  pl.BlockSpec((None, G, 2, H, D), lambda b: (b, 0, 0, 0, 0))]
                 + [pl.BlockSpec((None, steps, None, 2, H, D), lambda b: (b, 0, 0, 0, 0, 0)) for _ in views],
        out_specs=pl.BlockSpec((None, H, D), lambda b: (b, 0, 0)),
        compiler_params=_params("parallel"),
        name="decode_attention",
    )(q, kv_new, *views)


def _router_kernel(x_ref, wt_ref, b_ref, o_ref):
    logits = _dot_nt_3pass(x_ref[...], wt_ref[...]) + b_ref[...]
    e = jnp.exp(logits - jnp.max(logits, axis=1, keepdims=True))
    p = e / jnp.sum(e, axis=1, keepdims=True)
    E = p.shape[1]
    idx = lax.broadcasted_iota(I32, p.shape, 1)
    p1 = jnp.max(p, axis=1, keepdims=True)
    i1 = jnp.min(jnp.where(p == p1, idx, E), axis=1, keepdims=True)
    rest = jnp.where(idx == i1, -1.0, p)
    p2 = jnp.max(rest, axis=1, keepdims=True)
    i2 = jnp.min(jnp.where(rest == p2, idx, E), axis=1, keepdims=True)
    tot = p1 + p2
    o_ref[...] = jnp.where(idx == 0, i1.astype(F32),
                 jnp.where(idx == 1, i2.astype(F32),
                 jnp.where(idx == 2, p1 / tot,
                 jnp.where(idx == 3, p2 / tot, 0.0))))


def _router(xn, w_router_t, b_router):
    T, D = xn.shape
    E = w_router_t.shape[0]
    tm = _row_tile(T, 1024)
    return pl.pallas_call(
        _router_kernel,
        out_shape=jax.ShapeDtypeStruct((T, E), F32),
        grid=(T // tm,),
        in_specs=[pl.BlockSpec((tm, D), lambda i: (i, 0)),
                  pl.BlockSpec((E, D), lambda i: (0, 0)),
                  pl.BlockSpec((1, E), lambda i: (0, 0))],
        out_specs=pl.BlockSpec((tm, E), lambda i: (i, 0)),
        compiler_params=_params("parallel"),
        name="router",
    )(xn, w_router_t, b_router.reshape(1, E))


def _route_plan(top_i, tm, n_tiles):
    T = top_i.shape[0]
    e_flat = top_i.reshape(-1)
    onehot = (e_flat[:, None] == jnp.arange(N_EXPERTS, dtype=I32)[None, :]).astype(I32)
    csum = jnp.cumsum(onehot, axis=0)
    rank = jnp.sum(onehot * csum, axis=1) - 1
    counts = csum[-1]
    gsz = ((counts + tm - 1) // tm) * tm
    gend = jnp.cumsum(gsz)
    gstart = gend - gsz
    pos = (jnp.sum(onehot * gstart[None, :], axis=1) + rank).astype(I32)
    src = jnp.zeros((n_tiles * tm,), I32).at[pos].set(jnp.arange(TOP_K * T, dtype=I32) // TOP_K)
    tile_start = jnp.arange(n_tiles, dtype=I32) * tm
    tile_valid = (tile_start < gend[-1]).astype(I32)
    tile_e = jnp.sum((tile_start[:, None] >= gend[None, :]).astype(I32), axis=1)
    last_e = jnp.sum((gend[-1] - 1 >= gend).astype(I32))
    tile_e = jnp.where(tile_valid > 0, tile_e, last_e).astype(I32)
    return pos, src, tile_e, tile_valid


def _scatter_plan(top_i, tm, n_tiles):
    T = top_i.shape[0]
    R = n_tiles * tm
    e_flat = top_i.reshape(-1)
    onehot = (e_flat[:, None] == jnp.arange(N_EXPERTS, dtype=I32)[None, :]).astype(I32)
    csum = jnp.cumsum(onehot, axis=0)
    rank = jnp.sum(onehot * csum, axis=1) - 1
    counts = csum[-1]
    gsz = ((counts + tm - 1) // tm) * tm
    gend = jnp.cumsum(gsz)
    gstart = gend - gsz
    pos = (jnp.sum(onehot * gstart[None, :], axis=1) + rank).astype(I32)
    a = jnp.arange(TOP_K * T, dtype=I32)
    dst_assigned = jnp.zeros((R,), I32).at[pos].set((a % TOP_K) * T + a // TOP_K)
    r = jnp.arange(R, dtype=I32)
    e_row = jnp.minimum(jnp.sum((r[:, None] >= gend[None, :]).astype(I32), axis=1), N_EXPERTS - 1)
    row_onehot = (e_row[:, None] == jnp.arange(N_EXPERTS, dtype=I32)[None, :]).astype(I32)
    in_group = r - jnp.sum(row_onehot * gstart[None, :], axis=1)
    assigned = jnp.logical_and(r < gend[-1], in_group < jnp.sum(row_onehot * counts[None, :], axis=1))
    pad_rank = jnp.cumsum(jnp.logical_not(assigned).astype(I32)) - 1
    dst = jnp.where(assigned, dst_assigned, TOP_K * T + pad_rank).astype(I32)
    tile_start = jnp.arange(n_tiles, dtype=I32) * tm
    tile_valid = (tile_start < gend[-1]).astype(I32)
    tile_e = jnp.sum((tile_start[:, None] >= gend[None, :]).astype(I32), axis=1)
    last_e = jnp.sum((gend[-1] - 1 >= gend).astype(I32))
    tile_e = jnp.where(tile_valid > 0, tile_e, last_e).astype(I32)
    return dst, tile_e, tile_valid


def _scatter_experts_kernel(te_ref, tv_ref, tok_ref, dst_ref, x_hbm, wg_ref, wu_ref, wd_ref, y_hbm,
                            xf_ref, xb_ref, acc_ref, out_ref, gsem, ssem, zsem, *, tm, per_step):
    i = pl.program_id(0)
    c = pl.program_id(1)
    n_tiles = pl.num_programs(0)
    nf = pl.num_programs(1)
    valid = tv_ref[i] > 0
    prefetch = jnp.logical_and(i + 1 < n_tiles, tv_ref[jnp.minimum(i + 1, n_tiles - 1)] > 0)
    drain = jnp.logical_and(i > 0, tv_ref[jnp.maximum(i - 1, 0)] > 0)
    D = xb_ref.shape[1]
    dn = min(D, 512)

    def gather_copy(tile, r):
        return pltpu.make_async_copy(x_hbm.at[pl.ds(tok_ref[tile * tm + r], 1)], xf_ref.at[pl.ds(r, 1)],
                                     gsem.at[0])

    def scatter_copy(tile, r):
        return pltpu.make_async_copy(out_ref.at[pl.ds(r, 1)], y_hbm.at[pl.ds(dst_ref[tile * tm + r], 1)],
                                     ssem.at[0])

    @pl.when(jnp.logical_and(i == 0, c == 0))
    def _():
        def start(r, carry):
            gather_copy(0, r).start()
            return carry

        lax.fori_loop(0, tm, start, 0)

    @pl.when(c == 0)
    def _():
        acc_ref[...] = jnp.zeros_like(acc_ref)

    @pl.when(jnp.logical_and(jnp.logical_not(valid), c == 0))
    def _():
        fill = pltpu.make_async_copy(acc_ref, y_hbm.at[pl.ds(pl.multiple_of(i * tm, 8), tm)], zsem.at[0])
        fill.start()
        fill.wait()

    @pl.when(jnp.logical_and(valid, c == 0))
    def _():
        for _ in range(tm):
            pltpu.make_async_copy(x_hbm.at[pl.ds(0, 1)], xf_ref.at[pl.ds(0, 1)], gsem.at[0]).wait()
        xb_ref[...] = xf_ref[...].astype(BF16)

    def chunk(compute, with_prefetch, with_drain):
        if with_prefetch:
            for j in range(per_step):
                gather_copy(i + 1, c * per_step + j).start()
        if with_drain:
            for j in range(per_step):
                scatter_copy(i - 1, c * per_step + j).start()
        if compute:
            x = xb_ref[...]
            g = _dot(x, wg_ref[0].astype(BF16))
            u = _dot(x, wu_ref[0].astype(BF16))
            act = (g * _sigmoid(g) * u).astype(BF16)
            for n0 in range(0, D, dn):
                acc_ref[:, n0:n0 + dn] += _dot(act, wd_ref[0, :, n0:n0 + dn].astype(BF16))

    for compute, pf, dr in ((True, True, True), (True, True, False), (True, False, True),
                            (True, False, False), (False, False, True)):
        cond = valid if compute else jnp.logical_not(valid)
        cond = jnp.logical_and(cond, prefetch if pf else jnp.logical_not(prefetch))
        cond = jnp.logical_and(cond, drain if dr else jnp.logical_not(drain))
        pl.when(cond)(functools.partial(chunk, compute, pf, dr))

    @pl.when(jnp.logical_and(drain, c == nf - 1))
    def _():
        for _ in range(tm):
            pltpu.make_async_copy(out_ref.at[pl.ds(0, 1)], y_hbm.at[pl.ds(0, 1)], ssem.at[0]).wait()

    @pl.when(jnp.logical_and(valid, c == nf - 1))
    def _():
        out_ref[...] = acc_ref[...]


def _scatter_experts(xn, plan, w_gate, w_up, w_down, tm, tf):
    dst, tile_e, tile_valid = plan
    T, D = xn.shape
    Fe = w_gate.shape[-1]
    nf = Fe // tf
    n_tiles = tile_e.shape[0]
    per_step = tm // nf
    assert per_step * nf == tm
    tok = jnp.where(dst >= TOP_K * T, 0, dst % T).astype(I32)

    def chunk(i, c, tv):
        return jnp.where(tv[i] > 0, c, nf - 1)

    weight = lambda shape, index: pl.BlockSpec((None, 1) + shape,
                                               lambda i, c, te, tv, tok, dst: (0, te[i]) + index(chunk(i, c, tv)))
    return pl.pallas_call(
        functools.partial(_scatter_experts_kernel, tm=tm, per_step=per_step),
        out_shape=jax.ShapeDtypeStruct((n_tiles * tm, D), F32),
        grid_spec=pltpu.PrefetchScalarGridSpec(
            num_scalar_prefetch=4,
            grid=(n_tiles, nf),
            in_specs=[pl.BlockSpec(memory_space=pl.ANY),
                      weight((D, tf), lambda ch: (0, ch)),
                      weight((D, tf), lambda ch: (0, ch)),
                      weight((tf, D), lambda ch: (ch, 0))],
            out_specs=pl.BlockSpec(memory_space=pl.ANY),
            scratch_shapes=[pltpu.VMEM((tm, D), F32), pltpu.VMEM((tm, D), BF16),
                            pltpu.VMEM((tm, D), F32), pltpu.VMEM((tm, D), F32),
                            pltpu.SemaphoreType.DMA((1,)), pltpu.SemaphoreType.DMA((1,)),
                            pltpu.SemaphoreType.DMA((1,))]),
        compiler_params=_params("arbitrary", "arbitrary"),
        name="experts_scatter",
    )(tile_e, tile_valid, tok, dst, xn, w_gate, w_up, w_down)


def _mix_kernel(h_ref, gate_ref, y0_ref, y1_ref, fg_ref, o_ref):
    hsum = h_ref[...] + (gate_ref[:, 0:1] * y0_ref[...] + gate_ref[:, 1:2] * y1_ref[...])
    o_ref[...] = _rms(hsum) * fg_ref[...]


def _mix(h, gates, y_slots, final_gain):
    T, D = h.shape
    tm = _row_tile(T, 512)
    nt = T // tm
    return pl.pallas_call(
        _mix_kernel,
        out_shape=jax.ShapeDtypeStruct((T, D), F32),
        grid=(nt,),
        in_specs=[pl.BlockSpec((tm, D), lambda i: (i, 0)),
                  pl.BlockSpec((tm, TOP_K), lambda i: (i, 0)),
                  pl.BlockSpec((tm, D), lambda i: (i, 0)),
                  pl.BlockSpec((tm, D), lambda i: (i + nt, 0)),
                  pl.BlockSpec((1, D), lambda i: (0, 0))],
        out_specs=pl.BlockSpec((tm, D), lambda i: (i, 0)),
        compiler_params=_params("parallel"),
        name="moe_mix",
    )(h, gates, y_slots, y_slots, final_gain)


def _experts_kernel(te_ref, tv_ref, src_ref, x_hbm, wg_ref, wu_ref, wd_ref, o_ref, xf_ref, xb_ref, sem,
                    *, tm, per_step, accurate):
    i = pl.program_id(0)
    c = pl.program_id(1)
    n_tiles = pl.num_programs(0)
    n_rows = per_step * pl.num_programs(1)
    valid = tv_ref[i] > 0
    nxt = jnp.minimum(i + 1, n_tiles - 1)
    prefetch = jnp.logical_and(i + 1 < n_tiles, tv_ref[nxt] > 0)
    D = o_ref.shape[1]
    dn = min(D, 512)

    def row_copy(tile, r):
        src = src_ref[tile * tm + jnp.minimum(r, tm - 1)]
        return pltpu.make_async_copy(x_hbm.at[pl.ds(src, 1)], xf_ref.at[pl.ds(r, 1)], sem.at[0])

    @pl.when(jnp.logical_and(i == 0, c == 0))
    def _():
        def start(r, carry):
            row_copy(0, r).start()
            return carry

        lax.fori_loop(0, n_rows, start, 0)

    @pl.when(c == 0)
    def _():
        o_ref[...] = jnp.zeros_like(o_ref)

    @pl.when(jnp.logical_and(valid, c == 0))
    def _():
        def wait(r, carry):
            row_copy(i, r).wait()
            return carry

        lax.fori_loop(0, n_rows, wait, 0)
        xb_ref[...] = xf_ref[pl.ds(0, tm), :].astype(xb_ref.dtype)

    def mm(x, w):
        wh = w.astype(BF16)
        return _dot_3pass(x, w, wh) if accurate else _dot(x, wh)

    def chunk(with_prefetch):
        if with_prefetch:
            for j in range(per_step):
                row_copy(i + 1, c * per_step + j).start()
        x = xb_ref[...]
        g = mm(x, wg_ref[0])
        u = mm(x, wu_ref[0])
        act = (g * _sigmoid(g) * u).astype(xb_ref.dtype)
        for n0 in range(0, D, dn):
            o_ref[:, n0:n0 + dn] += mm(act, wd_ref[0, :, n0:n0 + dn])

    pl.when(jnp.logical_and(valid, prefetch))(functools.partial(chunk, True))
    pl.when(jnp.logical_and(valid, jnp.logical_not(prefetch)))(functools.partial(chunk, False))


def _experts(xn, pos_plan, w_gate, w_up, w_down, tm, tf, accurate=False):
    _, src, tile_e, tile_valid = pos_plan
    T, D = xn.shape
    Fe = w_gate.shape[-1]
    nf = Fe // tf
    n_tiles = tile_e.shape[0]
    per_step = -(-tm // nf)

    def chunk(i, c, tv):
        return jnp.where(tv[i] > 0, c, nf - 1)

    return pl.pallas_call(
        functools.partial(_experts_kernel, tm=tm, per_step=per_step, accurate=accurate),
        out_shape=jax.ShapeDtypeStruct((n_tiles * tm, D), F32),
        grid_spec=pltpu.PrefetchScalarGridSpec(
            num_scalar_prefetch=3,
            grid=(n_tiles, nf),
            in_specs=[pl.BlockSpec(memory_space=pl.ANY),
                      pl.BlockSpec((None, 1, D, tf), lambda i, c, te, tv, src: (0, te[i], 0, chunk(i, c, tv))),
                      pl.BlockSpec((None, 1, D, tf), lambda i, c, te, tv, src: (0, te[i], 0, chunk(i, c, tv))),
                      pl.BlockSpec((None, 1, tf, D), lambda i, c, te, tv, src: (0, te[i], chunk(i, c, tv), 0))],
            out_specs=pl.BlockSpec((tm, D), lambda i, c, te, tv, src: (i, 0)),
            scratch_shapes=[pltpu.VMEM((per_step * nf, D), F32), pltpu.VMEM((tm, D), F32 if accurate else BF16),
                            pltpu.SemaphoreType.DMA((1,))]),
        compiler_params=_params("arbitrary", "arbitrary"),
        name="experts",
    )(tile_e, tile_valid, src, xn, w_gate, w_up, w_down)


def _combine_kernel(pos_ref, h_ref, gate_ref, y_hbm, fg_ref, o_ref, ya_ref, yb_ref, sem, *, tm):
    i = pl.program_id(0)

    def copies(r):
        t = i * tm + r
        return (pltpu.make_async_copy(y_hbm.at[pl.ds(pos_ref[TOP_K * t], 1)],
                                      ya_ref.at[pl.ds(r, 1)], sem.at[0]),
                pltpu.make_async_copy(y_hbm.at[pl.ds(pos_ref[TOP_K * t + 1], 1)],
                                      yb_ref.at[pl.ds(r, 1)], sem.at[1]))

    def start(r, carry):
        a, b = copies(r)
        a.start()
        b.start()
        return carry

    def wait(r, carry):
        a, b = copies(r)
        a.wait()
        b.wait()
        return carry

    lax.fori_loop(0, tm, start, 0)
    lax.fori_loop(0, tm, wait, 0)
    hsum = h_ref[...] + (gate_ref[:, 0:1] * ya_ref[...] + gate_ref[:, 1:2] * yb_ref[...])
    o_ref[...] = _rms(hsum) * fg_ref[...]


def _combine(h, gates, y_sorted, pos, final_gain):
    T, D = h.shape
    tm = _row_tile(T, 256)
    return pl.pallas_call(
        functools.partial(_combine_kernel, tm=tm),
        out_shape=jax.ShapeDtypeStruct((T, D), F32),
        grid_spec=pltpu.PrefetchScalarGridSpec(
            num_scalar_prefetch=1,
            grid=(T // tm,),
            in_specs=[pl.BlockSpec((tm, D), lambda i, pos: (i, 0)),
                      pl.BlockSpec((tm, TOP_K), lambda i, pos: (i, 0)),
                      pl.BlockSpec(memory_space=pl.ANY),
                      pl.BlockSpec((1, D), lambda i, pos: (0, 0))],
            out_specs=pl.BlockSpec((tm, D), lambda i, pos: (i, 0)),
            scratch_shapes=[pltpu.VMEM((tm, D), F32), pltpu.VMEM((tm, D), F32),
                            pltpu.SemaphoreType.DMA((2,))]),
        compiler_params=_params("arbitrary"),
        name="moe_combine",
    )(pos, h, gates, y_sorted, final_gain)


def _forward(xp, xs, state, kv_caches, B, S, Bd, p):
    D = xp.shape[1]
    hv = p["mlstm_out_norm"].shape[-1]
    hk = hv // 2
    heads = p["attn_w_out"].shape[1] // ATTN_HEAD_DIM
    W = heads * ATTN_HEAD_DIM
    n_groups = len(DILATED_GROUPS)
    pad_rows = lambda a: jnp.zeros((SAMPLE_ROWS, a.shape[1]), F32).at[:Bd].set(a)

    gain0 = p["norm_mix"][0:1]
    w_gate_t = p["mlstm_w_in"][0, :, 2 * hk + 2 * hv:].T
    (xn,) = _norm(xp, gain0, [BF16])
    (xn_s,) = _norm(xs, gain0, [F32])
    proj, proj_s = _matmul(xn, xn_s, p["mlstm_w_in"], 2 * hk + 2 * hv, F32)
    gc, gr = _gates(xp, gain0, w_gate_t, p["mlstm_b_gates"][0])
    gc_s, _ = _gates(xs, gain0, w_gate_t, p["mlstm_b_gates"][0])
    hg, C_p, n_p, m_p = _mlstm_prompt(proj, gc, gr, p["mlstm_out_norm"], B, S)
    hg_s, C_s, n_s, m_s = _mlstm_decode(proj_s, gc_s, p["mlstm_out_norm"], *state)
    h1, h1_s = _matmul(hg, pad_rows(hg_s[:, 0]), p["mlstm_w_out"], D, F32, res=xp, res_tail=xs)

    (xn,) = _norm(h1, p["norm_ffn"][0:1], [BF16])
    (xn_s,) = _norm(h1_s, p["norm_ffn"][0:1], [F32])
    act, act_s = _glu(xn, xn_s, p["ffn_w_gate"], p["ffn_w_up"])
    h2, h2_s = _matmul(act, act_s, p["ffn_w_down"], D, F32, res=h1, res_tail=h1_s, tn=512)

    kvq_gains = jnp.concatenate([p["kv_norm"][None], p["norm_mix"][1:2]])
    xkv, xq = _norm(h2, kvq_gains, [BF16, BF16])
    xkv_s, xq_s = _norm(h2_s, kvq_gains, [F32, F32])
    kv, kv_s = _matmul(xkv, xkv_s, p["w_kv"], n_groups * 2 * W, F32)
    q, q_s = _matmul(xq, xq_s, p["attn_w_q"], n_groups * W, F32)
    att = _attention(q, kv, B, S, heads)
    kv_rows_p = [_kv_rows(kv, g, min(window, S), B, S, heads)
                 for g, (window, _) in enumerate(DILATED_GROUPS)]
    kv5 = kv_s[:Bd].reshape(Bd, n_groups, 2, heads, ATTN_HEAD_DIM)
    att_s = _decode_attention(q_s[:Bd].reshape(Bd, n_groups, heads, ATTN_HEAD_DIM), kv5, kv_caches)
    kv_rows_s = [kv5[:, None, g] for g in range(n_groups)]
    h3, h3_s = _matmul(att, pad_rows(att_s.reshape(Bd, W)), p["attn_w_out"], D, F32, res=h2, res_tail=h2_s)

    y_p = _moe_and_final_norm(h3, p, PROMPT_EXPERT_ROWS, False)
    y_s = _moe_and_final_norm(h3_s, p, SAMPLE_ROWS, True)

    states_p = (C_p[None], n_p[:, :, 0][None], m_p[:, :, 0, 0][None])
    states_s = (C_s[None], n_s[:, :, 0][None], m_s[:, :, 0, 0][None])
    return y_p, y_s[:Bd], states_p, kv_rows_p, states_s, kv_rows_s


def _moe_and_final_norm(h, p, tm_e, accurate):
    T = h.shape[0]
    (xn,) = _norm(h, p["norm_ffn"][1:2], [F32])
    route = _router(xn, p["moe_w_router"][0].T, p["moe_b_router"][0])
    top_i = route[:, :TOP_K].astype(I32)
    gates = route[:, TOP_K:2 * TOP_K]
    n_tiles = -(-(TOP_K * T) // tm_e) + N_EXPERTS
    weights = (p["moe_w_gate"], p["moe_w_up"], p["moe_w_down"])
    if accurate:
        plan = _route_plan(top_i, tm_e, n_tiles)
        y_sorted = _experts(xn, plan, *weights, tm_e, EXPERT_CHUNK, True)
        return _combine(h, gates, y_sorted, plan[0], p["final_norm"][None])
    y_slots = _scatter_experts(xn, _scatter_plan(top_i, tm_e, n_tiles), *weights, tm_e, EXPERT_CHUNK)
    return _mix(h, gates, y_slots, p["final_norm"][None])


def kernel(x_prompt, x_sample, state_mlstm_C, state_mlstm_n, state_mlstm_m, cache_kv_w128, cache_kv_w512, cache_kv_w2048, norm_mix, norm_ffn, mlstm_w_in, mlstm_b_gates, mlstm_out_norm, mlstm_w_out, kv_norm, w_kv, attn_w_q, attn_w_out, ffn_w_gate, ffn_w_up, ffn_w_down, moe_w_router, moe_b_router, moe_w_gate, moe_w_up, moe_w_down, final_norm):
    assert norm_mix.shape[0] == 2 and mlstm_w_in.shape[0] == 1 and attn_w_q.shape[0] == 1
    p = dict(norm_mix=norm_mix, norm_ffn=norm_ffn, mlstm_w_in=mlstm_w_in, mlstm_b_gates=mlstm_b_gates,
             mlstm_out_norm=mlstm_out_norm, mlstm_w_out=mlstm_w_out, kv_norm=kv_norm, w_kv=w_kv,
             attn_w_q=attn_w_q, attn_w_out=attn_w_out, ffn_w_gate=ffn_w_gate, ffn_w_up=ffn_w_up,
             ffn_w_down=ffn_w_down, moe_w_router=moe_w_router, moe_b_router=moe_b_router,
             moe_w_gate=moe_w_gate, moe_w_up=moe_w_up, moe_w_down=moe_w_down, final_norm=final_norm)
    B, S, D = x_prompt.shape
    Bd = x_sample.shape[0]
    assert x_sample.shape[1] == 1 and Bd <= SAMPLE_ROWS

    xs = jnp.zeros((SAMPLE_ROWS, D), F32).at[:Bd].set(x_sample[:, 0])
    state = (state_mlstm_C[0], state_mlstm_n[0], state_mlstm_m[0])
    y_p, y_s, states_p, kv_rows_p, states_s, kv_rows_s = _forward(
        x_prompt.reshape(B * S, D), xs, state, (cache_kv_w128, cache_kv_w512, cache_kv_w2048), B, S, Bd, p)
    return (y_p.reshape(B, S, D), y_s[:, None], *states_p, *kv_rows_p, *states_s, *kv_rows_s)
```

```python
import functools
import math

import jax
import jax.numpy as jnp
from jax import lax
from jax.experimental import pallas as pl
from jax.experimental.pallas import tpu as pltpu

F32 = jnp.float32
BF16 = jnp.bfloat16
I32 = jnp.int32

EPS = 1e-6
GATE_SOFTCAP = 15.0
MLSTM_HEADS = 4
ATTN_HEAD_DIM = 128
DILATED_GROUPS = ((128, 1), (512, 4), (2048, 16))
N_EXPERTS = 8
TOP_K = 2
MLSTM_CHUNK = 256
ATTN_BLOCK = 128
SAMPLE_ROWS = 16
EXPERT_CHUNK = 256
PROMPT_EXPERT_ROWS = 1056

VMEM_LIMIT = 56 * 1024 * 1024


def _params(*sem):
    return pltpu.CompilerParams(dimension_semantics=sem, vmem_limit_bytes=VMEM_LIMIT)


def _dot(a, b):
    return jnp.dot(a, b, preferred_element_type=F32)


def _dot_nt(a, b):
    return lax.dot_general(a, b, (((1,), (1,)), ((), ())), preferred_element_type=F32)


def _dot_tn(a, b):
    return lax.dot_general(a, b, (((0,), (0,)), ((), ())), preferred_element_type=F32)


def _split(x):
    hi = x.astype(BF16)
    lo = (x - hi.astype(F32)).astype(BF16)
    return hi, lo


def _dot_nt_3pass(x, w):
    xh, xl = _split(x)
    wh, wl = _split(w)
    return _dot_nt(xh, wh) + (_dot_nt(xl, wh) + _dot_nt(xh, wl))


def _dot_3pass(x, w, wh, cols=256):
    n = x.shape[0]
    xh, xl = _split(x)
    xs = jnp.concatenate([xh, xl], axis=0)
    outs = []
    for c0 in range(0, w.shape[1], cols):
        wh_c = wh[:, c0:c0 + cols]
        wl_c = (w[:, c0:c0 + cols] - wh_c.astype(F32)).astype(BF16)
        r = _dot(xs, wh_c)
        outs.append(r[:n] + (r[n:] + _dot(xh, wl_c)))
    return outs[0] if len(outs) == 1 else jnp.concatenate(outs, axis=1)


def _stack_tail(a_ref, at_ref, as_ref):
    tm = a_ref.shape[0]
    n = at_ref.shape[0]
    xh, xl = _split(at_ref[...])
    as_ref[0:tm, :] = a_ref[...]
    as_ref[tm:tm + n, :] = xh
    as_ref[tm + n:, :] = xl
    return xh


def _dot_stacked(as_ref, xh, w_ref, wh_ref, cols=256):
    n = xh.shape[0]
    tm = as_ref.shape[0] - 2 * n
    acc = _dot(as_ref[...], wh_ref[...])
    los = []
    for c0 in range(0, w_ref.shape[1], cols):
        wl_c = (w_ref[:, c0:c0 + cols] - wh_ref[:, c0:c0 + cols].astype(F32)).astype(BF16)
        los.append(_dot(xh, wl_c))
    lo = los[0] if len(los) == 1 else jnp.concatenate(los, axis=1)
    return acc[:tm], acc[tm:tm + n] + (acc[tm + n:] + lo)


def _sigmoid(x):
    return 1.0 / (1.0 + jnp.exp(-x))


def _rms(x):
    return x * lax.rsqrt(jnp.mean(x * x, axis=-1, keepdims=True) + EPS)


def _norm_kernel(x_ref, g_ref, *o_refs):
    y = _rms(x_ref[...])
    for i, o_ref in enumerate(o_refs):
        o_ref[...] = (y * g_ref[i:i + 1, :]).astype(o_ref.dtype)


def _row_tile(M, cap):
    t = (min(cap, M) // 16) * 16
    while M % t:
        t -= 16
    return t


def _norm(x, gains, dtypes):
    T, D = x.shape
    tm = _row_tile(T, 1024)
    return pl.pallas_call(
        _norm_kernel,
        out_shape=[jax.ShapeDtypeStruct((T, D), dt) for dt in dtypes],
        grid=(T // tm,),
        in_specs=[pl.BlockSpec((tm, D), lambda i: (i, 0)),
                  pl.BlockSpec(gains.shape, lambda i: (0, 0))],
        out_specs=[pl.BlockSpec((tm, D), lambda i: (i, 0)) for _ in dtypes],
        compiler_params=_params("parallel"),
        name="rmsnorm",
    )(x, gains)


def _mm_rows(M, K):
    return _row_tile(M, 1024 if K <= 4096 else 512)


def _weight_spec(w, tn):
    if w.ndim == 3:
        return pl.BlockSpec((None, w.shape[1], tn), lambda j, i: (0, 0, j))
    return pl.BlockSpec((w.shape[0], tn), lambda j, i: (0, j))


def _mm_kernel(a_ref, at_ref, w_ref, *rest, has_res):
    if has_res:
        r_ref, rt_ref, o_ref, ot_ref, wb_ref, as_ref = rest
    else:
        o_ref, ot_ref, wb_ref, as_ref = rest
    i = pl.program_id(1)
    last = pl.num_programs(1) - 1

    @pl.when(i == 0)
    def _():
        wb_ref[...] = w_ref[...].astype(BF16)

    @pl.when(i < last)
    def _():
        acc = _dot(a_ref[...], wb_ref[...])
        o_ref[...] = (r_ref[...] + acc if has_res else acc).astype(o_ref.dtype)

    @pl.when(i == last)
    def _():
        xh = _stack_tail(a_ref, at_ref, as_ref)
        acc, acc_t = _dot_stacked(as_ref, xh, w_ref, wb_ref)
        o_ref[...] = (r_ref[...] + acc if has_res else acc).astype(o_ref.dtype)
        ot_ref[...] = rt_ref[...] + acc_t if has_res else acc_t


def _matmul(a, a_tail, w, n_cols, out_dtype, res=None, res_tail=None, tn=1024):
    M, K = a.shape
    n = a_tail.shape[0]
    tm = _mm_rows(M, K)
    tn = min(tn, n_cols)
    in_specs = [pl.BlockSpec((tm, K), lambda j, i: (i, 0)),
                pl.BlockSpec((n, K), lambda j, i: (0, 0)), _weight_spec(w, tn)]
    args = [a, a_tail, w]
    if res is not None:
        in_specs += [pl.BlockSpec((tm, tn), lambda j, i: (i, j)), pl.BlockSpec((n, tn), lambda j, i: (0, j))]
        args += [res, res_tail]
    return pl.pallas_call(
        functools.partial(_mm_kernel, has_res=res is not None),
        out_shape=[jax.ShapeDtypeStruct((M, n_cols), out_dtype), jax.ShapeDtypeStruct((n, n_cols), F32)],
        grid=(n_cols // tn, M // tm),
        in_specs=in_specs,
        out_specs=[pl.BlockSpec((tm, tn), lambda j, i: (i, j)), pl.BlockSpec((n, tn), lambda j, i: (0, j))],
        scratch_shapes=[pltpu.VMEM((K, tn), BF16), pltpu.VMEM((tm + 2 * n, K), BF16)],
        compiler_params=_params("arbitrary", "arbitrary"),
        name="matmul",
    )(*args)


def _glu_kernel(a_ref, at_ref, wg_ref, wu_ref, o_ref, ot_ref, wgb_ref, wub_ref, as_ref):
    i = pl.program_id(1)
    last = pl.num_programs(1) - 1

    @pl.when(i == 0)
    def _():
        wgb_ref[...] = wg_ref[...].astype(BF16)
        wub_ref[...] = wu_ref[...].astype(BF16)

    @pl.when(i < last)
    def _():
        a = a_ref[...]
        g = _dot(a, wgb_ref[...])
        u = _dot(a, wub_ref[...])
        o_ref[...] = (g * _sigmoid(g) * u).astype(o_ref.dtype)

    @pl.when(i == last)
    def _():
        xh = _stack_tail(a_ref, at_ref, as_ref)
        g, g_t = _dot_stacked(as_ref, xh, wg_ref, wgb_ref)
        u, u_t = _dot_stacked(as_ref, xh, wu_ref, wub_ref)
        o_ref[...] = (g * _sigmoid(g) * u).astype(o_ref.dtype)
        ot_ref[...] = g_t * _sigmoid(g_t) * u_t


def _glu(a, a_tail, wg, wu, tf=512):
    M, K = a.shape
    n = a_tail.shape[0]
    F = wg.shape[-1]
    tm = _mm_rows(M, K)
    return pl.pallas_call(
        _glu_kernel,
        out_shape=[jax.ShapeDtypeStruct((M, F), BF16), jax.ShapeDtypeStruct((n, F), F32)],
        grid=(F // tf, M // tm),
        in_specs=[pl.BlockSpec((tm, K), lambda j, i: (i, 0)),
                  pl.BlockSpec((n, K), lambda j, i: (0, 0)), _weight_spec(wg, tf), _weight_spec(wu, tf)],
        out_specs=[pl.BlockSpec((tm, tf), lambda j, i: (i, j)), pl.BlockSpec((n, tf), lambda j, i: (0, j))],
        scratch_shapes=[pltpu.VMEM((K, tf), BF16), pltpu.VMEM((K, tf), BF16),
                        pltpu.VMEM((tm + 2 * n, K), BF16)],
        compiler_params=_params("arbitrary", "arbitrary"),
        name="swiglu_up",
    )(a, a_tail, wg, wu)


def _gates_kernel(x_ref, g_ref, wt_ref, b_ref, xn_ref, gc_ref, *gr_refs):
    H = MLSTM_HEADS
    xn = _rms(x_ref[...]) * g_ref[...]
    xn_ref[...] = xn.astype(xn_ref.dtype)
    pre = _dot_nt_3pass(xn, wt_ref[...]) + b_ref[...]
    t = GATE_SOFTCAP * jnp.tanh(pre / GATE_SOFTCAP)
    ls = jnp.minimum(t, 0.0) - jnp.log1p(jnp.exp(-jnp.abs(t)))
    lane = lax.broadcasted_iota(I32, t.shape, 1)
    fin = jnp.where(lane < H, t, ls)
    gc_ref[...] = fin[:, :2 * H]
    if gr_refs:
        gr_refs[0][...] = fin.T[:2 * H, :]


def _gates(x, gain, w_gate_t, b_gates, xn_dtype, head_major):
    T, D = x.shape
    G = w_gate_t.shape[0]
    tm = min(T, 512)
    wt = jnp.zeros((128, D), F32).at[:G].set(w_gate_t)
    b = jnp.zeros((1, 128), F32).at[0, :G].set(b_gates)
    out_shape = [jax.ShapeDtypeStruct((T, D), xn_dtype), jax.ShapeDtypeStruct((T, G), F32)]
    out_specs = [pl.BlockSpec((tm, D), lambda i: (i, 0)), pl.BlockSpec((tm, G), lambda i: (i, 0))]
    if head_major:
        out_shape.append(jax.ShapeDtypeStruct((G, T), F32))
        out_specs.append(pl.BlockSpec((G, tm), lambda i: (0, i)))
    return pl.pallas_call(
        _gates_kernel,
        out_shape=out_shape,
        grid=(T // tm,),
        in_specs=[pl.BlockSpec((tm, D), lambda i: (i, 0)),
                  pl.BlockSpec((1, D), lambda i: (0, 0)),
                  pl.BlockSpec((128, D), lambda i: (0, 0)),
                  pl.BlockSpec((1, 128), lambda i: (0, 0))],
        out_specs=out_specs,
        compiler_params=_params("parallel"),
        name="mlstm_gates",
    )(x, gain, wt, b)


def _mlstm_kernel(q_ref, k_ref, v_ref, o_ref, gc_ref, gr_ref, on_ref,
                  h_ref, c_out, n_out, m_out, c_s, n_s, m_s, *, dk, dv):
    H = MLSTM_HEADS
    c = pl.program_id(1)
    L = q_ref.shape[0]

    @pl.when(c == 0)
    def _():
        c_s[...] = jnp.zeros_like(c_s)
        n_s[...] = jnp.zeros_like(n_s)
        m_s[...] = jnp.zeros_like(m_s)

    row = lax.broadcasted_iota(I32, (L, L), 0)
    col = lax.broadcasted_iota(I32, (L, L), 1)
    tril = row >= col
    for h in range(H):
        q = q_ref[:, h * dk:(h + 1) * dk]
        k = k_ref[:, h * dk:(h + 1) * dk] * (dk ** -0.5)
        v = v_ref[:, h * dv:(h + 1) * dv]
        ig_c = gc_ref[:, h:h + 1]
        lf_c = gc_ref[:, H + h:H + h + 1]
        ig_r = gr_ref[h:h + 1, :]
        lf_r = gr_ref[H + h:H + h + 1, :]
        b_c = jnp.sum(jnp.where(tril, lf_r, 0.0), axis=1, keepdims=True)
        b_r = jnp.sum(jnp.where(row <= col, lf_c, 0.0), axis=0, keepdims=True)
        m_prev = m_s[h][:, 0:1]
        C = c_s[h]
        n = n_s[h]
        dmat = jnp.where(tril, b_c - b_r + ig_r, -jnp.inf)
        inter = b_c + m_prev
        m_t = jnp.maximum(inter, jnp.max(dmat, axis=1, keepdims=True))
        w_intra = jnp.exp(dmat - m_t)
        w_inter = jnp.exp(inter - m_t)
        qb = q.astype(BF16)
        vb = v.astype(BF16)
        s = _dot_nt(qb, k.astype(BF16)) * w_intra
        num = w_inter * _dot(qb, C.astype(BF16)) + _dot(s.astype(BF16), vb)
        den = w_inter * jnp.sum(q * n, axis=1, keepdims=True) + jnp.sum(s, axis=1, keepdims=True)
        hh = num / jnp.maximum(jnp.abs(den), jnp.exp(-m_t))
        b_last = b_c[L - 1:L, :]
        g_r = b_last - b_r + ig_r
        g_c = b_last - b_c + ig_c
        m_new = jnp.maximum(b_last + m_prev, jnp.max(g_r, axis=1, keepdims=True))
        decay = jnp.exp(b_last + m_prev - m_new)
        kw = k * jnp.exp(g_c - m_new)
        c_s[h] = decay * C + _dot_tn(kw.astype(BF16), vb)
        n_s[h] = decay * n + jnp.sum(kw, axis=0, keepdims=True)
        m_s[h] = jnp.broadcast_to(m_new, m_s.shape[1:])
        sl = slice(h * dv, (h + 1) * dv)
        h_ref[:, sl] = (_rms(hh) * on_ref[:, sl] * _sigmoid(o_ref[:, sl])).astype(h_ref.dtype)

    @pl.when(c == pl.num_programs(1) - 1)
    def _():
        c_out[0] = c_s[...]
        n_out[0] = n_s[...]
        m_out[0] = m_s[...]


def _mlstm_prompt(proj, gc, gr, out_norm, B, S):
    H = MLSTM_HEADS
    T = B * S
    hv = out_norm.shape[1]
    dv = hv // H
    dk = dv // 2
    hk = H * dk
    L = MLSTM_CHUNK
    nc = S // L
    G = gc.shape[1]
    rows = lambda b, c: b * nc + c
    return pl.pallas_call(
        functools.partial(_mlstm_kernel, dk=dk, dv=dv),
        out_shape=[jax.ShapeDtypeStruct((T, hv), BF16),
                   jax.ShapeDtypeStruct((B, H, dk, dv), F32),
                   jax.ShapeDtypeStruct((B, H, 1, dk), F32),
                   jax.ShapeDtypeStruct((B, H, 1, 128), F32)],
        grid=(B, nc),
        in_specs=[pl.BlockSpec((L, hk), lambda b, c: (rows(b, c), 0)),
                  pl.BlockSpec((L, hk), lambda b, c: (rows(b, c), 1)),
                  pl.BlockSpec((L, hv), lambda b, c: (rows(b, c), 1)),
                  pl.BlockSpec((L, hv), lambda b, c: (rows(b, c), 2)),
                  pl.BlockSpec((L, G), lambda b, c: (rows(b, c), 0)),
                  pl.BlockSpec((G, L), lambda b, c: (0, rows(b, c))),
                  pl.BlockSpec((1, hv), lambda b, c: (0, 0))],
        out_specs=[pl.BlockSpec((L, hv), lambda b, c: (rows(b, c), 0)),
                   pl.BlockSpec((1, H, dk, dv), lambda b, c: (b, 0, 0, 0)),
                   pl.BlockSpec((1, H, 1, dk), lambda b, c: (b, 0, 0, 0)),
                   pl.BlockSpec((1, H, 1, 128), lambda b, c: (b, 0, 0, 0))],
        scratch_shapes=[pltpu.VMEM((H, dk, dv), F32), pltpu.VMEM((H, 1, dk), F32),
                        pltpu.VMEM((H, 1, 128), F32)],
        compiler_params=_params("arbitrary", "arbitrary"),
        name="mlstm_prompt",
    )(proj, proj, proj, proj, gc, gr, out_norm)


def _mlstm_step_kernel(qr_ref, qc_ref, kr_ref, kc_ref, v_ref, o_ref, ig_ref, lf_ref, on_ref,
                       c_ref, n_ref, m_ref, h_ref, c_out, n_out, m_out):
    H = MLSTM_HEADS
    dk = qr_ref.shape[-1]
    dv = v_ref.shape[-1]
    for h in range(H):
        q_r = qr_ref[0, h]
        q_c = qc_ref[0, h]
        k_r = kr_ref[0, h] * (dk ** -0.5)
        k_c = kc_ref[0, h] * (dk ** -0.5)
        v = v_ref[0, h]
        ig = ig_ref[0, h]
        lf = lf_ref[0, h]
        C = c_ref[0, h]
        n = n_ref[0, h]
        m = m_ref[0, h]
        inter = lf + m
        m_t = jnp.maximum(inter, ig)
        w_intra = jnp.exp(ig - m_t)
        w_inter = jnp.exp(inter - m_t)
        s = jnp.sum(q_r * k_r, axis=1, keepdims=True) * w_intra
        num = w_inter * jnp.sum(q_c * C, axis=0, keepdims=True) + s * v
        den = w_inter * jnp.sum(q_r * n, axis=1, keepdims=True) + s
        hh = num / jnp.maximum(jnp.abs(den), jnp.exp(-m_t))
        decay = jnp.exp(inter - m_t)
        w_k = jnp.exp(ig - m_t)
        c_out[0, h] = decay * C + (k_c * w_k) * v
        n_out[0, h] = decay * n + k_r * w_k
        m_out[0, h] = m_t
        sl = slice(h * dv, (h + 1) * dv)
        h_ref[0, :, sl] = _rms(hh) * on_ref[:, sl] * _sigmoid(o_ref[0, :, sl])


def _mlstm_decode(proj, gc, out_norm, C0, n0, m0):
    H = MLSTM_HEADS
    Bd, _, dk, dv = C0.shape
    hk, hv = H * dk, H * dv
    q = proj[:Bd, :hk].reshape(Bd, H, dk)
    k = proj[:Bd, hk:2 * hk].reshape(Bd, H, dk)
    v = proj[:Bd, 2 * hk:2 * hk + hv].reshape(Bd, H, 1, dv)
    o = proj[:Bd, 2 * hk + hv:2 * hk + 2 * hv].reshape(Bd, 1, hv)
    ig = gc[:Bd, :H].reshape(Bd, H, 1, 1)
    lf = gc[:Bd, H:].reshape(Bd, H, 1, 1)
    spec = lambda *tail: pl.BlockSpec((1, H) + tail, lambda b: (b, 0, 0, 0))
    return pl.pallas_call(
        _mlstm_step_kernel,
        out_shape=[jax.ShapeDtypeStruct((Bd, 1, hv), F32),
                   jax.ShapeDtypeStruct((Bd, H, dk, dv), F32),
                   jax.ShapeDtypeStruct((Bd, H, 1, dk), F32),
                   jax.ShapeDtypeStruct((Bd, H, 1, 1), F32)],
        grid=(Bd,),
        in_specs=[spec(1, dk), spec(dk, 1), spec(1, dk), spec(dk, 1), spec(1, dv),
                  pl.BlockSpec((1, 1, hv), lambda b: (b, 0, 0)),
                  spec(1, 1), spec(1, 1),
                  pl.BlockSpec((1, hv), lambda b: (0, 0)),
                  spec(dk, dv), spec(1, dk), spec(1, 1)],
        out_specs=[pl.BlockSpec((1, 1, hv), lambda b: (b, 0, 0)),
                   spec(dk, dv), spec(1, dk), spec(1, 1)],
        compiler_params=_params("parallel"),
        name="mlstm_decode",
    )(q.reshape(Bd, H, 1, dk), q.reshape(Bd, H, dk, 1), k.reshape(Bd, H, 1, dk),
      k.reshape(Bd, H, dk, 1), v, o, ig, lf, out_norm, C0, n0.reshape(Bd, H, 1, dk),
      m0.reshape(Bd, H, 1, 1))


def _rows(start, size, stride):
    return pl.ds(start, size) if stride == 1 else pl.ds(start, size, stride=stride)


def _attn_kernel(*refs, dilations, seq):
    G = len(dilations)
    q_refs, k_refs, v_refs = refs[:G], refs[G:2 * G], refs[2 * G:3 * G]
    out_ref = refs[3 * G]
    o_scr, lse_scr, s_scr, p_scr = refs[3 * G + 1:]
    bq = ATTN_BLOCK
    D = ATTN_HEAD_DIM
    scale = D ** -0.5 * math.log2(math.e)
    row = lax.broadcasted_iota(I32, (bq, 2 * bq), 0)
    col = lax.broadcasted_iota(I32, (bq, 2 * bq), 1)
    bias_full = jnp.where(jnp.logical_and(col - row >= 0, col - row <= bq), 0.0, -jnp.inf)
    row1 = lax.broadcasted_iota(I32, (bq, bq), 0)
    col1 = lax.broadcasted_iota(I32, (bq, bq), 1)
    bias_first = jnp.where(row1 >= col1, 0.0, -jnp.inf)
    for g, r in enumerate(dilations):
        nb = seq // (r * bq)
        blocks = [(rho, u) for rho in range(r) for u in range(nb)]
        for i, (rho, u) in enumerate(blocks):
            q = q_refs[g][_rows(rho + u * bq * r, bq, r), :].astype(BF16)
            if u == 0:
                k = k_refs[g][_rows(rho, bq, r), :].astype(BF16)
                s_scr[i, :, :bq] = _dot_nt(q, k) * scale + bias_first
            else:
                k = k_refs[g][_rows(rho + (u - 1) * bq * r, 2 * bq, r), :].astype(BF16)
                s_scr[i] = _dot_nt(q, k) * scale + bias_full
        for i, (rho, u) in enumerate(blocks):
            s = s_scr[i, :, :bq] if u == 0 else s_scr[i]
            mx = jnp.max(s, axis=1, keepdims=True)
            e = jnp.exp2(s - mx)
            l = jnp.sum(e, axis=1, keepdims=True)
            p = (e / l).astype(BF16)
            if u == 0:
                p_scr[i, :, :bq] = p
            else:
                p_scr[i] = p
            lse_scr[g, _rows(rho + u * bq * r, bq, r), :] = jnp.broadcast_to(mx + jnp.log2(l), (bq, D))
        for i, (rho, u) in enumerate(blocks):
            if u == 0:
                v = v_refs[g][_rows(rho, bq, r), :].astype(BF16)
                o = _dot(p_scr[i, :, :bq], v)
            else:
                v = v_refs[g][_rows(rho + (u - 1) * bq * r, 2 * bq, r), :].astype(BF16)
                o = _dot(p_scr[i], v)
            o_scr[g, _rows(rho + u * bq * r, bq, r), :] = o
    lses = [lse_scr[g] for g in range(G)]
    mx = functools.reduce(jnp.maximum, lses)
    es = [jnp.exp2(l - mx) for l in lses]
    tot = functools.reduce(lambda a, b: a + b, es)
    acc = (es[0] / tot) * o_scr[0]
    for g in range(1, G):
        acc = acc + (es[g] / tot) * o_scr[g]
    out_ref[...] = acc.astype(out_ref.dtype)


def _attention(q, kv, B, S, heads):
    D = ATTN_HEAD_DIM
    T = B * S
    G = len(DILATED_GROUPS)
    dil = tuple(d for _, d in DILATED_GROUPS)
    assert all(w // d == ATTN_BLOCK and S % (d * ATTN_BLOCK) == 0 for w, d in DILATED_GROUPS)
    nblk = S // ATTN_BLOCK
    spec = lambda col: pl.BlockSpec((S, D), lambda b, h: (b, col(h)))
    in_specs = ([spec(lambda h, g=g: g * heads + h) for g in range(G)]
                + [spec(lambda h, g=g: 2 * g * heads + h) for g in range(G)]
                + [spec(lambda h, g=g: (2 * g + 1) * heads + h) for g in range(G)])
    return pl.pallas_call(
        functools.partial(_attn_kernel, dilations=dil, seq=S),
        out_shape=jax.ShapeDtypeStruct((T, heads * D), BF16),
        grid=(B, heads),
        in_specs=in_specs,
        out_specs=pl.BlockSpec((S, D), lambda b, h: (b, h)),
        scratch_shapes=[pltpu.VMEM((G, S, D), F32), pltpu.VMEM((G, S, D), F32),
                        pltpu.VMEM((nblk, ATTN_BLOCK, 2 * ATTN_BLOCK), F32),
                        pltpu.VMEM((nblk, ATTN_BLOCK, 2 * ATTN_BLOCK), BF16)],
        compiler_params=_params("parallel", "parallel"),
        name="dilated_attention",
    )(*([q] * G), *([kv] * (2 * G)))


def _kv_rows_kernel(x_ref, o_ref, *, heads):
    o_ref[0] = x_ref[...].reshape(x_ref.shape[0], 2, heads, ATTN_HEAD_DIM)


def _kv_rows(kv, g, keep, B, S, heads):
    W2 = 2 * heads * ATTN_HEAD_DIM
    tm = min(keep, 512)
    first = (S - keep) // tm
    per_seq = S // tm
    return pl.pallas_call(
        functools.partial(_kv_rows_kernel, heads=heads),
        out_shape=jax.ShapeDtypeStruct((B, keep, 2, heads, ATTN_HEAD_DIM), F32),
        grid=(B, keep // tm),
        in_specs=[pl.BlockSpec((tm, W2), lambda b, i: (b * per_seq + first + i, g))],
        out_specs=pl.BlockSpec((1, tm, 2, heads, ATTN_HEAD_DIM), lambda b, i: (b, i, 0, 0, 0)),
        compiler_params=_params("parallel", "parallel"),
        name=f"kv_rows_g{g}",
    )(kv)


def _decode_attn_kernel(q_ref, kvn_ref, *refs):
    G = len(refs) - 1
    buf_refs = refs[:G]
    out_ref = refs[G]
    scale = ATTN_HEAD_DIM ** -0.5
    outs, lses = [], []
    for g in range(G):
        q = q_ref[g]
        k_new = kvn_ref[g, 0]
        v_new = kvn_ref[g, 1]
        k_buf = buf_refs[g][:, 0]
        v_buf = buf_refs[g][:, 1]
        s_buf = jnp.sum(k_buf * q[None], axis=-1, keepdims=True) * scale
        s_new = jnp.sum(k_new * q, axis=-1, keepdims=True) * scale
        mx = jnp.maximum(jnp.max(s_buf, axis=0), s_new)
        e_buf = jnp.exp(s_buf - mx[None])
        e_new = jnp.exp(s_new - mx)
        l = jnp.sum(e_buf, axis=0) + e_new
        outs.append(jnp.sum((e_buf / l[None]) * v_buf, axis=0) + (e_new / l) * v_new)
        lses.append(mx + jnp.log(l))
    mx = functools.reduce(jnp.maximum, lses)
    es = [jnp.exp(l - mx) for l in lses]
    tot = functools.reduce(lambda a, b: a + b, es)
    acc = (es[0] / tot) * outs[0]
    for g in range(1, G):
        acc = acc + (es[g] / tot) * outs[g]
    out_ref[...] = acc


def _decode_attention(q, kv_new, caches):
    Bd, G, H, D = q.shape
    steps = ATTN_BLOCK
    views = []
    for cache, (window, dilation) in zip(caches, DILATED_GROUPS):
        Lb = cache.shape[1]
        assert window // dilation == steps and Lb == steps * dilation
        views.append(cache.reshape(Bd, steps, dilation, 2, H, D))
    return pl.pallas_call(
        _decode_attn_kernel,
        out_shape=jax.ShapeDtypeStruct((Bd, H, D), F32),
        grid=(Bd,),
        in_specs=[pl.BlockSpec((None, G, H, D), lambda b: (b, 0, 0, 0)),
                  pl.BlockSpec((None, G, 2, H, D), lambda b: (b, 0, 0, 0, 0))]
                 + [pl.BlockSpec((None, steps, None, 2, H, D), lambda b: (b, 0, 0, 0, 0, 0)) for _ in views],
        out_specs=pl.BlockSpec((None, H, D), lambda b: (b, 0, 0)),
        compiler_params=_params("parallel"),
        name="decode_attention",
    )(q, kv_new, *views)


def _router_kernel(h_ref, g_ref, wt_ref, b_ref, xn_ref, o_ref):
    xn = _rms(h_ref[...]) * g_ref[...]
    xn_ref[...] = xn
    logits = _dot_nt_3pass(xn, wt_ref[...]) + b_ref[...]
    e = jnp.exp(logits - jnp.max(logits, axis=1, keepdims=True))
    p = e / jnp.sum(e, axis=1, keepdims=True)
    E = p.shape[1]
    idx = lax.broadcasted_iota(I32, p.shape, 1)
    p1 = jnp.max(p, axis=1, keepdims=True)
    i1 = jnp.min(jnp.where(p == p1, idx, E), axis=1, keepdims=True)
    rest = jnp.where(idx == i1, -1.0, p)
    p2 = jnp.max(rest, axis=1, keepdims=True)
    i2 = jnp.min(jnp.where(rest == p2, idx, E), axis=1, keepdims=True)
    tot = p1 + p2
    o_ref[...] = jnp.where(idx == 0, i1.astype(F32),
                 jnp.where(idx == 1, i2.astype(F32),
                 jnp.where(idx == 2, p1 / tot,
                 jnp.where(idx == 3, p2 / tot, 0.0))))


def _router(h, gain, w_router_t, b_router):
    T, D = h.shape
    E = w_router_t.shape[0]
    tm = _row_tile(T, 512)
    return pl.pallas_call(
        _router_kernel,
        out_shape=[jax.ShapeDtypeStruct((T, D), F32), jax.ShapeDtypeStruct((T, E), F32)],
        grid=(T // tm,),
        in_specs=[pl.BlockSpec((tm, D), lambda i: (i, 0)),
                  pl.BlockSpec((1, D), lambda i: (0, 0)),
                  pl.BlockSpec((E, D), lambda i: (0, 0)),
                  pl.BlockSpec((1, E), lambda i: (0, 0))],
        out_specs=[pl.BlockSpec((tm, D), lambda i: (i, 0)), pl.BlockSpec((tm, E), lambda i: (i, 0))],
        compiler_params=_params("parallel"),
        name="router",
    )(h, gain, w_router_t, b_router.reshape(1, E))


def _route_plan(top_i, tm, n_tiles):
    T = top_i.shape[0]
    e_flat = top_i.reshape(-1)
    onehot = (e_flat[:, None] == jnp.arange(N_EXPERTS, dtype=I32)[None, :]).astype(I32)
    csum = jnp.cumsum(onehot, axis=0)
    rank = jnp.sum(onehot * csum, axis=1) - 1
    counts = csum[-1]
    gsz = ((counts + tm - 1) // tm) * tm
    gend = jnp.cumsum(gsz)
    gstart = gend - gsz
    pos = (jnp.sum(onehot * gstart[None, :], axis=1) + rank).astype(I32)
    src = jnp.zeros((n_tiles * tm,), I32).at[pos].set(jnp.arange(TOP_K * T, dtype=I32) // TOP_K)
    tile_start = jnp.arange(n_tiles, dtype=I32) * tm
    tile_valid = (tile_start < gend[-1]).astype(I32)
    tile_e = jnp.sum((tile_start[:, None] >= gend[None, :]).astype(I32), axis=1)
    last_e = jnp.sum((gend[-1] - 1 >= gend).astype(I32))
    tile_e = jnp.where(tile_valid > 0, tile_e, last_e).astype(I32)
    return pos, src, tile_e, tile_valid


def _scatter_plan(top_i, tm, n_tiles):
    T = top_i.shape[0]
    R = n_tiles * tm
    e_flat = top_i.reshape(-1)
    onehot = (e_flat[:, None] == jnp.arange(N_EXPERTS, dtype=I32)[None, :]).astype(I32)
    csum = jnp.cumsum(onehot, axis=0)
    rank = jnp.sum(onehot * csum, axis=1) - 1
    counts = csum[-1]
    gsz = ((counts + tm - 1) // tm) * tm
    gend = jnp.cumsum(gsz)
    gstart = gend - gsz
    pos = (jnp.sum(onehot * gstart[None, :], axis=1) + rank).astype(I32)
    a = jnp.arange(TOP_K * T, dtype=I32)
    dst_assigned = jnp.zeros((R,), I32).at[pos].set((a % TOP_K) * T + a // TOP_K)
    r = jnp.arange(R, dtype=I32)
    e_row = jnp.minimum(jnp.sum((r[:, None] >= gend[None, :]).astype(I32), axis=1), N_EXPERTS - 1)
    row_onehot = (e_row[:, None] == jnp.arange(N_EXPERTS, dtype=I32)[None, :]).astype(I32)
    in_group = r - jnp.sum(row_onehot * gstart[None, :], axis=1)
    assigned = jnp.logical_and(r < gend[-1], in_group < jnp.sum(row_onehot * counts[None, :], axis=1))
    pad_rank = jnp.cumsum(jnp.logical_not(assigned).astype(I32)) - 1
    dst = jnp.where(assigned, dst_assigned, TOP_K * T + pad_rank).astype(I32)
    tile_start = jnp.arange(n_tiles, dtype=I32) * tm
    tile_valid = (tile_start < gend[-1]).astype(I32)
    tile_e = jnp.sum((tile_start[:, None] >= gend[None, :]).astype(I32), axis=1)
    last_e = jnp.sum((gend[-1] - 1 >= gend).astype(I32))
    tile_e = jnp.where(tile_valid > 0, tile_e, last_e).astype(I32)
    return dst, tile_e, tile_valid


def _scatter_experts_kernel(te_ref, tv_ref, tok_ref, dst_ref, x_hbm, wg_ref, wu_ref, wd_ref, y_hbm,
                            xf_ref, xb_ref, acc_ref, out_ref, gsem, ssem, zsem, *, tm, per_step):
    i = pl.program_id(0)
    c = pl.program_id(1)
    n_tiles = pl.num_programs(0)
    nf = pl.num_programs(1)
    valid = tv_ref[i] > 0
    prefetch = jnp.logical_and(i + 1 < n_tiles, tv_ref[jnp.minimum(i + 1, n_tiles - 1)] > 0)
    drain = jnp.logical_and(i > 0, tv_ref[jnp.maximum(i - 1, 0)] > 0)
    D = xb_ref.shape[1]
    dn = min(D, 512)

    def gather_copy(tile, r):
        return pltpu.make_async_copy(x_hbm.at[pl.ds(tok_ref[tile * tm + r], 1)], xf_ref.at[pl.ds(r, 1)],
                                     gsem.at[0])

    def scatter_copy(tile, r):
        return pltpu.make_async_copy(out_ref.at[pl.ds(r, 1)], y_hbm.at[pl.ds(dst_ref[tile * tm + r], 1)],
                                     ssem.at[0])

    @pl.when(jnp.logical_and(i == 0, c == 0))
    def _():
        def start(r, carry):
            gather_copy(0, r).start()
            return carry

        lax.fori_loop(0, tm, start, 0)

    @pl.when(c == 0)
    def _():
        acc_ref[...] = jnp.zeros_like(acc_ref)

    @pl.when(jnp.logical_and(jnp.logical_not(valid), c == 0))
    def _():
        fill = pltpu.make_async_copy(acc_ref, y_hbm.at[pl.ds(pl.multiple_of(i * tm, 8), tm)], zsem.at[0])
        fill.start()
        fill.wait()

    @pl.when(jnp.logical_and(valid, c == 0))
    def _():
        for _ in range(tm):
            pltpu.make_async_copy(x_hbm.at[pl.ds(0, 1)], xf_ref.at[pl.ds(0, 1)], gsem.at[0]).wait()
        xb_ref[...] = xf_ref[...].astype(BF16)

    def chunk(compute, with_prefetch, with_drain):
        if with_prefetch:
            for j in range(per_step):
                gather_copy(i + 1, c * per_step + j).start()
        if with_drain:
            for j in range(per_step):
                scatter_copy(i - 1, c * per_step + j).start()
        if compute:
            x = xb_ref[...]
            g = _dot(x, wg_ref[0].astype(BF16))
            u = _dot(x, wu_ref[0].astype(BF16))
            act = (g * _sigmoid(g) * u).astype(BF16)
            for n0 in range(0, D, dn):
                acc_ref[:, n0:n0 + dn] += _dot(act, wd_ref[0, :, n0:n0 + dn].astype(BF16))

    for compute, pf, dr in ((True, True, True), (True, True, False), (True, False, True),
                            (True, False, False), (False, False, True)):
        cond = valid if compute else jnp.logical_not(valid)
        cond = jnp.logical_and(cond, prefetch if pf else jnp.logical_not(prefetch))
        cond = jnp.logical_and(cond, drain if dr else jnp.logical_not(drain))
        pl.when(cond)(functools.partial(chunk, compute, pf, dr))

    @pl.when(jnp.logical_and(drain, c == nf - 1))
    def _():
        for _ in range(tm):
            pltpu.make_async_copy(out_ref.at[pl.ds(0, 1)], y_hbm.at[pl.ds(0, 1)], ssem.at[0]).wait()

    @pl.when(jnp.logical_and(valid, c == nf - 1))
    def _():
        out_ref[...] = acc_ref[...]


def _scatter_experts(xn, plan, w_gate, w_up, w_down, tm, tf):
    dst, tile_e, tile_valid = plan
    T, D = xn.shape
    Fe = w_gate.shape[-1]
    nf = Fe // tf
    n_tiles = tile_e.shape[0]
    per_step = tm // nf
    assert per_step * nf == tm
    tok = jnp.where(dst >= TOP_K * T, 0, dst % T).astype(I32)

    def chunk(i, c, tv):
        return jnp.where(tv[i] > 0, c, nf - 1)

    weight = lambda shape, index: pl.BlockSpec((None, 1) + shape,
                                               lambda i, c, te, tv, tok, dst: (0, te[i]) + index(chunk(i, c, tv)))
    return pl.pallas_call(
        functools.partial(_scatter_experts_kernel, tm=tm, per_step=per_step),
        out_shape=jax.ShapeDtypeStruct((n_tiles * tm, D), F32),
        grid_spec=pltpu.PrefetchScalarGridSpec(
            num_scalar_prefetch=4,
            grid=(n_tiles, nf),
            in_specs=[pl.BlockSpec(memory_space=pl.ANY),
                      weight((D, tf), lambda ch: (0, ch)),
                      weight((D, tf), lambda ch: (0, ch)),
                      weight((tf, D), lambda ch: (ch, 0))],
            out_specs=pl.BlockSpec(memory_space=pl.ANY),
            scratch_shapes=[pltpu.VMEM((tm, D), F32), pltpu.VMEM((tm, D), BF16),
                            pltpu.VMEM((tm, D), F32), pltpu.VMEM((tm, D), F32),
                            pltpu.SemaphoreType.DMA((1,)), pltpu.SemaphoreType.DMA((1,)),
                            pltpu.SemaphoreType.DMA((1,))]),
        compiler_params=_params("arbitrary", "arbitrary"),
        name="experts_scatter",
    )(tile_e, tile_valid, tok, dst, xn, w_gate, w_up, w_down)


def _mix_kernel(h_ref, gate_ref, y0_ref, y1_ref, fg_ref, o_ref):
    hsum = h_ref[...] + (gate_ref[:, 0:1] * y0_ref[...] + gate_ref[:, 1:2] * y1_ref[...])
    o_ref[...] = _rms(hsum) * fg_ref[...]


def _mix(h, gates, y_slots, final_gain):
    T, D = h.shape
    tm = _row_tile(T, 512)
    nt = T // tm
    return pl.pallas_call(
        _mix_kernel,
        out_shape=jax.ShapeDtypeStruct((T, D), F32),
        grid=(nt,),
        in_specs=[pl.BlockSpec((tm, D), lambda i: (i, 0)),
                  pl.BlockSpec((tm, TOP_K), lambda i: (i, 0)),
                  pl.BlockSpec((tm, D), lambda i: (i, 0)),
                  pl.BlockSpec((tm, D), lambda i: (i + nt, 0)),
                  pl.BlockSpec((1, D), lambda i: (0, 0))],
        out_specs=pl.BlockSpec((tm, D), lambda i: (i, 0)),
        compiler_params=_params("parallel"),
        name="moe_mix",
    )(h, gates, y_slots, y_slots, final_gain)


def _experts_kernel(te_ref, tv_ref, src_ref, x_hbm, wg_ref, wu_ref, wd_ref, o_ref, xf_ref, xb_ref, sem,
                    *, tm, per_step, accurate):
    i = pl.program_id(0)
    c = pl.program_id(1)
    n_tiles = pl.num_programs(0)
    n_rows = per_step * pl.num_programs(1)
    valid = tv_ref[i] > 0
    nxt = jnp.minimum(i + 1, n_tiles - 1)
    prefetch = jnp.logical_and(i + 1 < n_tiles, tv_ref[nxt] > 0)
    D = o_ref.shape[1]
    dn = min(D, 512)

    def row_copy(tile, r):
        src = src_ref[tile * tm + jnp.minimum(r, tm - 1)]
        return pltpu.make_async_copy(x_hbm.at[pl.ds(src, 1)], xf_ref.at[pl.ds(r, 1)], sem.at[0])

    @pl.when(jnp.logical_and(i == 0, c == 0))
    def _():
        def start(r, carry):
            row_copy(0, r).start()
            return carry

        lax.fori_loop(0, n_rows, start, 0)

    @pl.when(c == 0)
    def _():
        o_ref[...] = jnp.zeros_like(o_ref)

    @pl.when(jnp.logical_and(valid, c == 0))
    def _():
        def wait(r, carry):
            row_copy(i, r).wait()
            return carry

        lax.fori_loop(0, n_rows, wait, 0)
        xb_ref[...] = xf_ref[pl.ds(0, tm), :].astype(xb_ref.dtype)

    def mm(x, w):
        wh = w.astype(BF16)
        return _dot_3pass(x, w, wh) if accurate else _dot(x, wh)

    def chunk(with_prefetch):
        if with_prefetch:
            for j in range(per_step):
                row_copy(i + 1, c * per_step + j).start()
        x = xb_ref[...]
        g = mm(x, wg_ref[0])
        u = mm(x, wu_ref[0])
        act = (g * _sigmoid(g) * u).astype(xb_ref.dtype)
        for n0 in range(0, D, dn):
            o_ref[:, n0:n0 + dn] += mm(act, wd_ref[0, :, n0:n0 + dn])

    pl.when(jnp.logical_and(valid, prefetch))(functools.partial(chunk, True))
    pl.when(jnp.logical_and(valid, jnp.logical_not(prefetch)))(functools.partial(chunk, False))


def _experts(xn, pos_plan, w_gate, w_up, w_down, tm, tf, accurate=False):
    _, src, tile_e, tile_valid = pos_plan
    T, D = xn.shape
    Fe = w_gate.shape[-1]
    nf = Fe // tf
    n_tiles = tile_e.shape[0]
    per_step = -(-tm // nf)

    def chunk(i, c, tv):
        return jnp.where(tv[i] > 0, c, nf - 1)

    return pl.pallas_call(
        functools.partial(_experts_kernel, tm=tm, per_step=per_step, accurate=accurate),
        out_shape=jax.ShapeDtypeStruct((n_tiles * tm, D), F32),
        grid_spec=pltpu.PrefetchScalarGridSpec(
            num_scalar_prefetch=3,
            grid=(n_tiles, nf),
            in_specs=[pl.BlockSpec(memory_space=pl.ANY),
                      pl.BlockSpec((None, 1, D, tf), lambda i, c, te, tv, src: (0, te[i], 0, chunk(i, c, tv))),
                      pl.BlockSpec((None, 1, D, tf), lambda i, c, te, tv, src: (0, te[i], 0, chunk(i, c, tv))),
                      pl.BlockSpec((None, 1, tf, D), lambda i, c, te, tv, src: (0, te[i], chunk(i, c, tv), 0))],
            out_specs=pl.BlockSpec((tm, D), lambda i, c, te, tv, src: (i, 0)),
            scratch_shapes=[pltpu.VMEM((per_step * nf, D), F32), pltpu.VMEM((tm, D), F32 if accurate else BF16),
                            pltpu.SemaphoreType.DMA((1,))]),
        compiler_params=_params("arbitrary", "arbitrary"),
        name="experts",
    )(tile_e, tile_valid, src, xn, w_gate, w_up, w_down)


def _combine_kernel(pos_ref, h_ref, gate_ref, y_hbm, fg_ref, o_ref, ya_ref, yb_ref, sem, *, tm):
    i = pl.program_id(0)

    def copies(r):
        t = i * tm + r
        return (pltpu.make_async_copy(y_hbm.at[pl.ds(pos_ref[TOP_K * t], 1)],
                                      ya_ref.at[pl.ds(r, 1)], sem.at[0]),
                pltpu.make_async_copy(y_hbm.at[pl.ds(pos_ref[TOP_K * t + 1], 1)],
                                      yb_ref.at[pl.ds(r, 1)], sem.at[1]))

    def start(r, carry):
        a, b = copies(r)
        a.start()
        b.start()
        return carry

    def wait(r, carry):
        a, b = copies(r)
        a.wait()
        b.wait()
        return carry

    lax.fori_loop(0, tm, start, 0)
    lax.fori_loop(0, tm, wait, 0)
    hsum = h_ref[...] + (gate_ref[:, 0:1] * ya_ref[...] + gate_ref[:, 1:2] * yb_ref[...])
    o_ref[...] = _rms(hsum) * fg_ref[...]


def _combine(h, gates, y_sorted, pos, final_gain):
    T, D = h.shape
    tm = _row_tile(T, 256)
    return pl.pallas_call(
        functools.partial(_combine_kernel, tm=tm),
        out_shape=jax.ShapeDtypeStruct((T, D), F32),
        grid_spec=pltpu.PrefetchScalarGridSpec(
            num_scalar_prefetch=1,
            grid=(T // tm,),
            in_specs=[pl.BlockSpec((tm, D), lambda i, pos: (i, 0)),
                      pl.BlockSpec((tm, TOP_K), lambda i, pos: (i, 0)),
                      pl.BlockSpec(memory_space=pl.ANY),
                      pl.BlockSpec((1, D), lambda i, pos: (0, 0))],
            out_specs=pl.BlockSpec((tm, D), lambda i, pos: (i, 0)),
            scratch_shapes=[pltpu.VMEM((tm, D), F32), pltpu.VMEM((tm, D), F32),
                            pltpu.SemaphoreType.DMA((2,))]),
        compiler_params=_params("arbitrary"),
        name="moe_combine",
    )(pos, h, gates, y_sorted, final_gain)


def _forward(xp, xs, state, kv_caches, B, S, Bd, p):
    D = xp.shape[1]
    hv = p["mlstm_out_norm"].shape[-1]
    hk = hv // 2
    heads = p["attn_w_out"].shape[1] // ATTN_HEAD_DIM
    W = heads * ATTN_HEAD_DIM
    n_groups = len(DILATED_GROUPS)
    pad_rows = lambda a: jnp.zeros((SAMPLE_ROWS, a.shape[1]), F32).at[:Bd].set(a)

    gain0 = p["norm_mix"][0:1]
    w_gate_t = p["mlstm_w_in"][0, :, 2 * hk + 2 * hv:].T
    xn, gc, gr = _gates(xp, gain0, w_gate_t, p["mlstm_b_gates"][0], BF16, True)
    xn_s, gc_s = _gates(xs, gain0, w_gate_t, p["mlstm_b_gates"][0], F32, False)
    proj, proj_s = _matmul(xn, xn_s, p["mlstm_w_in"], 2 * hk + 2 * hv, F32)
    hg, C_p, n_p, m_p = _mlstm_prompt(proj, gc, gr, p["mlstm_out_norm"], B, S)
    hg_s, C_s, n_s, m_s = _mlstm_decode(proj_s, gc_s, p["mlstm_out_norm"], *state)
    h1, h1_s = _matmul(hg, pad_rows(hg_s[:, 0]), p["mlstm_w_out"], D, F32, res=xp, res_tail=xs)

    (xn,) = _norm(h1, p["norm_ffn"][0:1], [BF16])
    (xn_s,) = _norm(h1_s, p["norm_ffn"][0:1], [F32])
    act, act_s = _glu(xn, xn_s, p["ffn_w_gate"], p["ffn_w_up"])
    h2, h2_s = _matmul(act, act_s, p["ffn_w_down"], D, F32, res=h1, res_tail=h1_s, tn=512)

    kvq_gains = jnp.concatenate([p["kv_norm"][None], p["norm_mix"][1:2]])
    xkv, xq = _norm(h2, kvq_gains, [BF16, BF16])
    xkv_s, xq_s = _norm(h2_s, kvq_gains, [F32, F32])
    kv, kv_s = _matmul(xkv, xkv_s, p["w_kv"], n_groups * 2 * W, F32)
    q, q_s = _matmul(xq, xq_s, p["attn_w_q"], n_groups * W, F32)
    att = _attention(q, kv, B, S, heads)
    kv_rows_p = [_kv_rows(kv, g, min(window, S), B, S, heads)
                 for g, (window, _) in enumerate(DILATED_GROUPS)]
    kv5 = kv_s[:Bd].reshape(Bd, n_groups, 2, heads, ATTN_HEAD_DIM)
    att_s = _decode_attention(q_s[:Bd].reshape(Bd, n_groups, heads, ATTN_HEAD_DIM), kv5, kv_caches)
    kv_rows_s = [kv5[:, None, g] for g in range(n_groups)]
    h3, h3_s = _matmul(att, pad_rows(att_s.reshape(Bd, W)), p["attn_w_out"], D, F32, res=h2, res_tail=h2_s)

    y_p = _moe_and_final_norm(h3, p, PROMPT_EXPERT_ROWS, False)
    y_s = _moe_and_final_norm(h3_s, p, SAMPLE_ROWS, True)

    states_p = (C_p[None], n_p[:, :, 0][None], m_p[:, :, 0, 0][None])
    states_s = (C_s[None], n_s[:, :, 0][None], m_s[:, :, 0, 0][None])
    return y_p, y_s[:Bd], states_p, kv_rows_p, states_s, kv_rows_s


def _moe_and_final_norm(h, p, tm_e, accurate):
    T = h.shape[0]
    xn, route = _router(h, p["norm_ffn"][1:2], p["moe_w_router"][0].T, p["moe_b_router"][0])
    top_i = route[:, :TOP_K].astype(I32)
    gates = route[:, TOP_K:2 * TOP_K]
    n_tiles = -(-(TOP_K * T) // tm_e) + N_EXPERTS
    weights = (p["moe_w_gate"], p["moe_w_up"], p["moe_w_down"])
    if accurate:
        plan = _route_plan(top_i, tm_e, n_tiles)
        y_sorted = _experts(xn, plan, *weights, tm_e, EXPERT_CHUNK, True)
        return _combine(h, gates, y_sorted, plan[0], p["final_norm"][None])
    y_slots = _scatter_experts(xn, _scatter_plan(top_i, tm_e, n_tiles), *weights, tm_e, EXPERT_CHUNK)
    return _mix(h, gates, y_slots, p["final_norm"][None])


def kernel(x_prompt, x_sample, state_mlstm_C, state_mlstm_n, state_mlstm_m, cache_kv_w128, cache_kv_w512, cache_kv_w2048, norm_mix, norm_ffn, mlstm_w_in, mlstm_b_gates, mlstm_out_norm, mlstm_w_out, kv_norm, w_kv, attn_w_q, attn_w_out, ffn_w_gate, ffn_w_up, ffn_w_down, moe_w_router, moe_b_router, moe_w_gate, moe_w_up, moe_w_down, final_norm):
    assert norm_mix.shape[0] == 2 and mlstm_w_in.shape[0] == 1 and attn_w_q.shape[0] == 1
    p = dict(norm_mix=norm_mix, norm_ffn=norm_ffn, mlstm_w_in=mlstm_w_in, mlstm_b_gates=mlstm_b_gates,
             mlstm_out_norm=mlstm_out_norm, mlstm_w_out=mlstm_w_out, kv_norm=kv_norm, w_kv=w_kv,
             attn_w_q=attn_w_q, attn_w_out=attn_w_out, ffn_w_gate=ffn_w_gate, ffn_w_up=ffn_w_up,
             ffn_w_down=ffn_w_down, moe_w_router=moe_w_router, moe_b_router=moe_b_router,
             moe_w_gate=moe_w_gate, moe_w_up=moe_w_up, moe_w_down=moe_w_down, final_norm=final_norm)
    B, S, D = x_prompt.shape
    Bd = x_sample.shape[0]
    assert x_sample.shape[1] == 1 and Bd <= SAMPLE_ROWS

    xs = jnp.zeros((SAMPLE_ROWS, D), F32).at[:Bd].set(x_sample[:, 0])
    state = (state_mlstm_C[0], state_mlstm_n[0], state_mlstm_m[0])
    y_p, y_s, states_p, kv_rows_p, states_s, kv_rows_s = _forward(
        x_prompt.reshape(B * S, D), xs, state, (cache_kv_w128, cache_kv_w512, cache_kv_w2048), B, S, Bd, p)
    return (y_p.reshape(B, S, D), y_s[:, None], *states_p, *kv_rows_p, *states_s, *kv_rows_s)
```

```python
import functools
import math

import jax
import jax.numpy as jnp
from jax import lax
from jax.experimental import pallas as pl
from jax.experimental.pallas import tpu as pltpu

F32 = jnp.float32
BF16 = jnp.bfloat16
I32 = jnp.int32

EPS = 1e-6
GATE_SOFTCAP = 15.0
MLSTM_HEADS = 4
ATTN_HEAD_DIM = 128
DILATED_GROUPS = ((128, 1), (512, 4), (2048, 16))
N_EXPERTS = 8
TOP_K = 2
MLSTM_CHUNK = 256
ATTN_BLOCK = 128
SAMPLE_ROWS = 16
EXPERT_CHUNK = 256
PROMPT_EXPERT_ROWS = 1056
SAMPLE_EXPERT_ROWS = 176

VMEM_LIMIT = 56 * 1024 * 1024


def _params(*sem):
    return pltpu.CompilerParams(dimension_semantics=sem, vmem_limit_bytes=VMEM_LIMIT)


def _dot(a, b):
    return jnp.dot(a, b, preferred_element_type=F32)


def _dot_nt(a, b):
    return lax.dot_general(a, b, (((1,), (1,)), ((), ())), preferred_element_type=F32)


def _dot_tn(a, b):
    return lax.dot_general(a, b, (((0,), (0,)), ((), ())), preferred_element_type=F32)


def _split(x):
    hi = x.astype(BF16)
    lo = (x - hi.astype(F32)).astype(BF16)
    return hi, lo


def _dot_nt_3pass(x, w):
    xh, xl = _split(x)
    wh, wl = _split(w)
    return _dot_nt(xh, wh) + (_dot_nt(xl, wh) + _dot_nt(xh, wl))


def _dot_3pass(x, w, wh, cols=256):
    n = x.shape[0]
    xh, xl = _split(x)
    xs = jnp.concatenate([xh, xl], axis=0)
    outs = []
    for c0 in range(0, w.shape[1], cols):
        wh_c = wh[:, c0:c0 + cols]
        wl_c = (w[:, c0:c0 + cols] - wh_c.astype(F32)).astype(BF16)
        r = _dot(xs, wh_c)
        outs.append(r[:n] + (r[n:] + _dot(xh, wl_c)))
    return outs[0] if len(outs) == 1 else jnp.concatenate(outs, axis=1)


def _stack_tail(a_ref, at_ref, as_ref):
    tm = a_ref.shape[0]
    n = at_ref.shape[0]
    xh, xl = _split(at_ref[...])
    as_ref[0:tm, :] = a_ref[...]
    as_ref[tm:tm + n, :] = xh
    as_ref[tm + n:, :] = xl
    return xh


def _dot_stacked(as_ref, xh, w_ref, wh_ref, cols=256):
    n = xh.shape[0]
    tm = as_ref.shape[0] - 2 * n
    acc = _dot(as_ref[...], wh_ref[...])
    los = []
    for c0 in range(0, w_ref.shape[1], cols):
        wl_c = (w_ref[:, c0:c0 + cols] - wh_ref[:, c0:c0 + cols].astype(F32)).astype(BF16)
        los.append(_dot(xh, wl_c))
    lo = los[0] if len(los) == 1 else jnp.concatenate(los, axis=1)
    return acc[:tm], acc[tm:tm + n] + (acc[tm + n:] + lo)


def _sigmoid(x):
    return 1.0 / (1.0 + jnp.exp(-x))


def _rms(x):
    return x * lax.rsqrt(jnp.mean(x * x, axis=-1, keepdims=True) + EPS)


def _norm_kernel(x_ref, g_ref, *o_refs):
    y = _rms(x_ref[...])
    for i, o_ref in enumerate(o_refs):
        o_ref[...] = (y * g_ref[i:i + 1, :]).astype(o_ref.dtype)


def _row_tile(M, cap):
    t = (min(cap, M) // 16) * 16
    while M % t:
        t -= 16
    return t


def _norm(x, gains, dtypes):
    T, D = x.shape
    tm = _row_tile(T, 1024)
    return pl.pallas_call(
        _norm_kernel,
        out_shape=[jax.ShapeDtypeStruct((T, D), dt) for dt in dtypes],
        grid=(T // tm,),
        in_specs=[pl.BlockSpec((tm, D), lambda i: (i, 0)),
                  pl.BlockSpec(gains.shape, lambda i: (0, 0))],
        out_specs=[pl.BlockSpec((tm, D), lambda i: (i, 0)) for _ in dtypes],
        compiler_params=_params("parallel"),
        name="rmsnorm",
    )(x, gains)


def _mm_rows(M, K):
    return _row_tile(M, 1024 if K <= 4096 else 512)


def _weight_spec(w, tn):
    if w.ndim == 3:
        return pl.BlockSpec((None, w.shape[1], tn), lambda j, i: (0, 0, j))
    return pl.BlockSpec((w.shape[0], tn), lambda j, i: (0, j))


def _mm_kernel(a_ref, at_ref, w_ref, *rest, has_res):
    if has_res:
        r_ref, rt_ref, o_ref, ot_ref, wb_ref, as_ref = rest
    else:
        o_ref, ot_ref, wb_ref, as_ref = rest
    i = pl.program_id(1)
    last = pl.num_programs(1) - 1

    @pl.when(i == 0)
    def _():
        wb_ref[...] = w_ref[...].astype(BF16)

    @pl.when(i < last)
    def _():
        acc = _dot(a_ref[...], wb_ref[...])
        o_ref[...] = (r_ref[...] + acc if has_res else acc).astype(o_ref.dtype)

    @pl.when(i == last)
    def _():
        xh = _stack_tail(a_ref, at_ref, as_ref)
        acc, acc_t = _dot_stacked(as_ref, xh, w_ref, wb_ref)
        o_ref[...] = (r_ref[...] + acc if has_res else acc).astype(o_ref.dtype)
        ot_ref[...] = rt_ref[...] + acc_t if has_res else acc_t


def _matmul(a, a_tail, w, n_cols, out_dtype, res=None, res_tail=None, tn=1024):
    M, K = a.shape
    n = a_tail.shape[0]
    tm = _mm_rows(M, K)
    tn = min(tn, n_cols)
    in_specs = [pl.BlockSpec((tm, K), lambda j, i: (i, 0)),
                pl.BlockSpec((n, K), lambda j, i: (0, 0)), _weight_spec(w, tn)]
    args = [a, a_tail, w]
    if res is not None:
        in_specs += [pl.BlockSpec((tm, tn), lambda j, i: (i, j)), pl.BlockSpec((n, tn), lambda j, i: (0, j))]
        args += [res, res_tail]
    return pl.pallas_call(
        functools.partial(_mm_kernel, has_res=res is not None),
        out_shape=[jax.ShapeDtypeStruct((M, n_cols), out_dtype), jax.ShapeDtypeStruct((n, n_cols), F32)],
        grid=(n_cols // tn, M // tm),
        in_specs=in_specs,
        out_specs=[pl.BlockSpec((tm, tn), lambda j, i: (i, j)), pl.BlockSpec((n, tn), lambda j, i: (0, j))],
        scratch_shapes=[pltpu.VMEM((K, tn), BF16), pltpu.VMEM((tm + 2 * n, K), BF16)],
        compiler_params=_params("arbitrary", "arbitrary"),
        name="matmul",
    )(*args)


def _glu_kernel(a_ref, at_ref, wg_ref, wu_ref, o_ref, ot_ref, wgb_ref, wub_ref, as_ref):
    i = pl.program_id(1)
    last = pl.num_programs(1) - 1

    @pl.when(i == 0)
    def _():
        wgb_ref[...] = wg_ref[...].astype(BF16)
        wub_ref[...] = wu_ref[...].astype(BF16)

    @pl.when(i < last)
    def _():
        a = a_ref[...]
        g = _dot(a, wgb_ref[...])
        u = _dot(a, wub_ref[...])
        o_ref[...] = (g * _sigmoid(g) * u).astype(o_ref.dtype)

    @pl.when(i == last)
    def _():
        xh = _stack_tail(a_ref, at_ref, as_ref)
        g, g_t = _dot_stacked(as_ref, xh, wg_ref, wgb_ref)
        u, u_t = _dot_stacked(as_ref, xh, wu_ref, wub_ref)
        o_ref[...] = (g * _sigmoid(g) * u).astype(o_ref.dtype)
        ot_ref[...] = g_t * _sigmoid(g_t) * u_t


def _glu(a, a_tail, wg, wu, tf=512):
    M, K = a.shape
    n = a_tail.shape[0]
    F = wg.shape[-1]
    tm = _mm_rows(M, K)
    return pl.pallas_call(
        _glu_kernel,
        out_shape=[jax.ShapeDtypeStruct((M, F), BF16), jax.ShapeDtypeStruct((n, F), F32)],
        grid=(F // tf, M // tm),
        in_specs=[pl.BlockSpec((tm, K), lambda j, i: (i, 0)),
                  pl.BlockSpec((n, K), lambda j, i: (0, 0)), _weight_spec(wg, tf), _weight_spec(wu, tf)],
        out_specs=[pl.BlockSpec((tm, tf), lambda j, i: (i, j)), pl.BlockSpec((n, tf), lambda j, i: (0, j))],
        scratch_shapes=[pltpu.VMEM((K, tf), BF16), pltpu.VMEM((K, tf), BF16),
                        pltpu.VMEM((tm + 2 * n, K), BF16)],
        compiler_params=_params("arbitrary", "arbitrary"),
        name="swiglu_up",
    )(a, a_tail, wg, wu)


def _gates_kernel(x_ref, g_ref, wt_ref, b_ref, xn_ref, gc_ref, *gr_refs):
    H = MLSTM_HEADS
    xn = _rms(x_ref[...]) * g_ref[...]
    xn_ref[...] = xn.astype(xn_ref.dtype)
    pre = _dot_nt_3pass(xn, wt_ref[...]) + b_ref[...]
    t = GATE_SOFTCAP * jnp.tanh(pre / GATE_SOFTCAP)
    ls = jnp.minimum(t, 0.0) - jnp.log1p(jnp.exp(-jnp.abs(t)))
    lane = lax.broadcasted_iota(I32, t.shape, 1)
    fin = jnp.where(lane < H, t, ls)
    gc_ref[...] = fin[:, :2 * H]
    if gr_refs:
        gr_refs[0][...] = fin.T[:2 * H, :]


def _gates(x, gain, w_gate_t, b_gates, xn_dtype, head_major):
    T, D = x.shape
    G = w_gate_t.shape[0]
    tm = min(T, 512)
    wt = jnp.zeros((128, D), F32).at[:G].set(w_gate_t)
    b = jnp.zeros((1, 128), F32).at[0, :G].set(b_gates)
    out_shape = [jax.ShapeDtypeStruct((T, D), xn_dtype), jax.ShapeDtypeStruct((T, G), F32)]
    out_specs = [pl.BlockSpec((tm, D), lambda i: (i, 0)), pl.BlockSpec((tm, G), lambda i: (i, 0))]
    if head_major:
        out_shape.append(jax.ShapeDtypeStruct((G, T), F32))
        out_specs.append(pl.BlockSpec((G, tm), lambda i: (0, i)))
    return pl.pallas_call(
        _gates_kernel,
        out_shape=out_shape,
        grid=(T // tm,),
        in_specs=[pl.BlockSpec((tm, D), lambda i: (i, 0)),
                  pl.BlockSpec((1, D), lambda i: (0, 0)),
                  pl.BlockSpec((128, D), lambda i: (0, 0)),
                  pl.BlockSpec((1, 128), lambda i: (0, 0))],
        out_specs=out_specs,
        compiler_params=_params("parallel"),
        name="mlstm_gates",
    )(x, gain, wt, b)


def _mlstm_kernel(q_ref, k_ref, v_ref, o_ref, gc_ref, gr_ref, on_ref,
                  h_ref, c_out, n_out, m_out, c_s, n_s, m_s, *, dk, dv):
    H = MLSTM_HEADS
    c = pl.program_id(1)
    L = q_ref.shape[0]

    @pl.when(c == 0)
    def _():
        c_s[...] = jnp.zeros_like(c_s)
        n_s[...] = jnp.zeros_like(n_s)
        m_s[...] = jnp.zeros_like(m_s)

    row = lax.broadcasted_iota(I32, (L, L), 0)
    col = lax.broadcasted_iota(I32, (L, L), 1)
    tril = row >= col
    for h in range(H):
        q = q_ref[:, h * dk:(h + 1) * dk]
        k = k_ref[:, h * dk:(h + 1) * dk] * (dk ** -0.5)
        v = v_ref[:, h * dv:(h + 1) * dv]
        ig_c = gc_ref[:, h:h + 1]
        lf_c = gc_ref[:, H + h:H + h + 1]
        ig_r = gr_ref[h:h + 1, :]
        lf_r = gr_ref[H + h:H + h + 1, :]
        b_c = jnp.sum(jnp.where(tril, lf_r, 0.0), axis=1, keepdims=True)
        b_r = jnp.sum(jnp.where(row <= col, lf_c, 0.0), axis=0, keepdims=True)
        m_prev = m_s[h][:, 0:1]
        C = c_s[h]
        n = n_s[h]
        dmat = jnp.where(tril, b_c - b_r + ig_r, -jnp.inf)
        inter = b_c + m_prev
        m_t = jnp.maximum(inter, jnp.max(dmat, axis=1, keepdims=True))
        w_intra = jnp.exp(dmat - m_t)
        w_inter = jnp.exp(inter - m_t)
        qb = q.astype(BF16)
        vb = v.astype(BF16)
        s = _dot_nt(qb, k.astype(BF16)) * w_intra
        num = w_inter * _dot(qb, C.astype(BF16)) + _dot(s.astype(BF16), vb)
        den = w_inter * jnp.sum(q * n, axis=1, keepdims=True) + jnp.sum(s, axis=1, keepdims=True)
        hh = num / jnp.maximum(jnp.abs(den), jnp.exp(-m_t))
        b_last = b_c[L - 1:L, :]
        g_r = b_last - b_r + ig_r
        g_c = b_last - b_c + ig_c
        m_new = jnp.maximum(b_last + m_prev, jnp.max(g_r, axis=1, keepdims=True))
        decay = jnp.exp(b_last + m_prev - m_new)
        kw = k * jnp.exp(g_c - m_new)
        c_s[h] = decay * C + _dot_tn(kw.astype(BF16), vb)
        n_s[h] = decay * n + jnp.sum(kw, axis=0, keepdims=True)
        m_s[h] = jnp.broadcast_to(m_new, m_s.shape[1:])
        sl = slice(h * dv, (h + 1) * dv)
        h_ref[:, sl] = (_rms(hh) * on_ref[:, sl] * _sigmoid(o_ref[:, sl])).astype(h_ref.dtype)

    @pl.when(c == pl.num_programs(1) - 1)
    def _():
        c_out[0] = c_s[...]
        n_out[0] = n_s[...]
        m_out[0] = m_s[...]


def _mlstm_prompt(proj, gc, gr, out_norm, B, S):
    H = MLSTM_HEADS
    T = B * S
    hv = out_norm.shape[1]
    dv = hv // H
    dk = dv // 2
    hk = H * dk
    L = MLSTM_CHUNK
    nc = S // L
    G = gc.shape[1]
    rows = lambda b, c: b * nc + c
    return pl.pallas_call(
        functools.partial(_mlstm_kernel, dk=dk, dv=dv),
        out_shape=[jax.ShapeDtypeStruct((T, hv), BF16),
                   jax.ShapeDtypeStruct((B, H, dk, dv), F32),
                   jax.ShapeDtypeStruct((B, H, 1, dk), F32),
                   jax.ShapeDtypeStruct((B, H, 1, 128), F32)],
        grid=(B, nc),
        in_specs=[pl.BlockSpec((L, hk), lambda b, c: (rows(b, c), 0)),
                  pl.BlockSpec((L, hk), lambda b, c: (rows(b, c), 1)),
                  pl.BlockSpec((L, hv), lambda b, c: (rows(b, c), 1)),
                  pl.BlockSpec((L, hv), lambda b, c: (rows(b, c), 2)),
                  pl.BlockSpec((L, G), lambda b, c: (rows(b, c), 0)),
                  pl.BlockSpec((G, L), lambda b, c: (0, rows(b, c))),
                  pl.BlockSpec((1, hv), lambda b, c: (0, 0))],
        out_specs=[pl.BlockSpec((L, hv), lambda b, c: (rows(b, c), 0)),
                   pl.BlockSpec((1, H, dk, dv), lambda b, c: (b, 0, 0, 0)),
                   pl.BlockSpec((1, H, 1, dk), lambda b, c: (b, 0, 0, 0)),
                   pl.BlockSpec((1, H, 1, 128), lambda b, c: (b, 0, 0, 0))],
        scratch_shapes=[pltpu.VMEM((H, dk, dv), F32), pltpu.VMEM((H, 1, dk), F32),
                        pltpu.VMEM((H, 1, 128), F32)],
        compiler_params=_params("arbitrary", "arbitrary"),
        name="mlstm_prompt",
    )(proj, proj, proj, proj, gc, gr, out_norm)


def _mlstm_step_kernel(qr_ref, qc_ref, kr_ref, kc_ref, v_ref, o_ref, ig_ref, lf_ref, on_ref,
                       c_ref, n_ref, m_ref, h_ref, c_out, n_out, m_out):
    H = MLSTM_HEADS
    dk = qr_ref.shape[-1]
    dv = v_ref.shape[-1]
    for h in range(H):
        q_r = qr_ref[0, h]
        q_c = qc_ref[0, h]
        k_r = kr_ref[0, h] * (dk ** -0.5)
        k_c = kc_ref[0, h] * (dk ** -0.5)
        v = v_ref[0, h]
        ig = ig_ref[0, h]
        lf = lf_ref[0, h]
        C = c_ref[0, h]
        n = n_ref[0, h]
        m = m_ref[0, h]
        inter = lf + m
        m_t = jnp.maximum(inter, ig)
        w_intra = jnp.exp(ig - m_t)
        w_inter = jnp.exp(inter - m_t)
        s = jnp.sum(q_r * k_r, axis=1, keepdims=True) * w_intra
        num = w_inter * jnp.sum(q_c * C, axis=0, keepdims=True) + s * v
        den = w_inter * jnp.sum(q_r * n, axis=1, keepdims=True) + s
        hh = num / jnp.maximum(jnp.abs(den), jnp.exp(-m_t))
        decay = jnp.exp(inter - m_t)
        w_k = jnp.exp(ig - m_t)
        c_out[0, h] = decay * C + (k_c * w_k) * v
        n_out[0, h] = decay * n + k_r * w_k
        m_out[0, h] = m_t
        sl = slice(h * dv, (h + 1) * dv)
        h_ref[0, :, sl] = _rms(hh) * on_ref[:, sl] * _sigmoid(o_ref[0, :, sl])


def _mlstm_decode(proj, gc, out_norm, C0, n0, m0):
    H = MLSTM_HEADS
    Bd, _, dk, dv = C0.shape
    hk, hv = H * dk, H * dv
    q = proj[:Bd, :hk].reshape(Bd, H, dk)
    k = proj[:Bd, hk:2 * hk].reshape(Bd, H, dk)
    v = proj[:Bd, 2 * hk:2 * hk + hv].reshape(Bd, H, 1, dv)
    o = proj[:Bd, 2 * hk + hv:2 * hk + 2 * hv].reshape(Bd, 1, hv)
    ig = gc[:Bd, :H].reshape(Bd, H, 1, 1)
    lf = gc[:Bd, H:].reshape(Bd, H, 1, 1)
    spec = lambda *tail: pl.BlockSpec((1, H) + tail, lambda b: (b, 0, 0, 0))
    return pl.pallas_call(
        _mlstm_step_kernel,
        out_shape=[jax.ShapeDtypeStruct((Bd, 1, hv), F32),
                   jax.ShapeDtypeStruct((Bd, H, dk, dv), F32),
                   jax.ShapeDtypeStruct((Bd, H, 1, dk), F32),
                   jax.ShapeDtypeStruct((Bd, H, 1, 1), F32)],
        grid=(Bd,),
        in_specs=[spec(1, dk), spec(dk, 1), spec(1, dk), spec(dk, 1), spec(1, dv),
                  pl.BlockSpec((1, 1, hv), lambda b: (b, 0, 0)),
                  spec(1, 1), spec(1, 1),
                  pl.BlockSpec((1, hv), lambda b: (0, 0)),
                  spec(dk, dv), spec(1, dk), spec(1, 1)],
        out_specs=[pl.BlockSpec((1, 1, hv), lambda b: (b, 0, 0)),
                   spec(dk, dv), spec(1, dk), spec(1, 1)],
        compiler_params=_params("parallel"),
        name="mlstm_decode",
    )(q.reshape(Bd, H, 1, dk), q.reshape(Bd, H, dk, 1), k.reshape(Bd, H, 1, dk),
      k.reshape(Bd, H, dk, 1), v, o, ig, lf, out_norm, C0, n0.reshape(Bd, H, 1, dk),
      m0.reshape(Bd, H, 1, 1))


def _rows(start, size, stride):
    return pl.ds(start, size) if stride == 1 else pl.ds(start, size, stride=stride)


def _attn_kernel(*refs, dilations, seq):
    G = len(dilations)
    q_refs, k_refs, v_refs = refs[:G], refs[G:2 * G], refs[2 * G:3 * G]
    out_ref = refs[3 * G]
    o_scr, lse_scr, s_scr, p_scr = refs[3 * G + 1:]
    bq = ATTN_BLOCK
    D = ATTN_HEAD_DIM
    scale = D ** -0.5 * math.log2(math.e)
    row = lax.broadcasted_iota(I32, (bq, 2 * bq), 0)
    col = lax.broadcasted_iota(I32, (bq, 2 * bq), 1)
    bias_full = jnp.where(jnp.logical_and(col - row >= 0, col - row <= bq), 0.0, -jnp.inf)
    row1 = lax.broadcasted_iota(I32, (bq, bq), 0)
    col1 = lax.broadcasted_iota(I32, (bq, bq), 1)
    bias_first = jnp.where(row1 >= col1, 0.0, -jnp.inf)
    for g, r in enumerate(dilations):
        nb = seq // (r * bq)
        blocks = [(rho, u) for rho in range(r) for u in range(nb)]
        for i, (rho, u) in enumerate(blocks):
            q = q_refs[g][_rows(rho + u * bq * r, bq, r), :].astype(BF16)
            if u == 0:
                k = k_refs[g][_rows(rho, bq, r), :].astype(BF16)
                s_scr[i, :, :bq] = _dot_nt(q, k) * scale + bias_first
            else:
                k = k_refs[g][_rows(rho + (u - 1) * bq * r, 2 * bq, r), :].astype(BF16)
                s_scr[i] = _dot_nt(q, k) * scale + bias_full
        for i, (rho, u) in enumerate(blocks):
            s = s_scr[i, :, :bq] if u == 0 else s_scr[i]
            mx = jnp.max(s, axis=1, keepdims=True)
            e = jnp.exp2(s - mx)
            l = jnp.sum(e, axis=1, keepdims=True)
            p = (e / l).astype(BF16)
            if u == 0:
                p_scr[i, :, :bq] = p
            else:
                p_scr[i] = p
            lse_scr[g, _rows(rho + u * bq * r, bq, r), :] = jnp.broadcast_to(mx + jnp.log2(l), (bq, D))
        for i, (rho, u) in enumerate(blocks):
            if u == 0:
                v = v_refs[g][_rows(rho, bq, r), :].astype(BF16)
                o = _dot(p_scr[i, :, :bq], v)
            else:
                v = v_refs[g][_rows(rho + (u - 1) * bq * r, 2 * bq, r), :].astype(BF16)
                o = _dot(p_scr[i], v)
            o_scr[g, _rows(rho + u * bq * r, bq, r), :] = o
    lses = [lse_scr[g] for g in range(G)]
    mx = functools.reduce(jnp.maximum, lses)
    es = [jnp.exp2(l - mx) for l in lses]
    tot = functools.reduce(lambda a, b: a + b, es)
    acc = (es[0] / tot) * o_scr[0]
    for g in range(1, G):
        acc = acc + (es[g] / tot) * o_scr[g]
    out_ref[...] = acc.astype(out_ref.dtype)


def _attention(q, kv, B, S, heads):
    D = ATTN_HEAD_DIM
    T = B * S
    G = len(DILATED_GROUPS)
    dil = tuple(d for _, d in DILATED_GROUPS)
    assert all(w // d == ATTN_BLOCK and S % (d * ATTN_BLOCK) == 0 for w, d in DILATED_GROUPS)
    nblk = S // ATTN_BLOCK
    spec = lambda col: pl.BlockSpec((S, D), lambda b, h: (b, col(h)))
    in_specs = ([spec(lambda h, g=g: g * heads + h) for g in range(G)]
                + [spec(lambda h, g=g: 2 * g * heads + h) for g in range(G)]
                + [spec(lambda h, g=g: (2 * g + 1) * heads + h) for g in range(G)])
    return pl.pallas_call(
        functools.partial(_attn_kernel, dilations=dil, seq=S),
        out_shape=jax.ShapeDtypeStruct((T, heads * D), BF16),
        grid=(B, heads),
        in_specs=in_specs,
        out_specs=pl.BlockSpec((S, D), lambda b, h: (b, h)),
        scratch_shapes=[pltpu.VMEM((G, S, D), F32), pltpu.VMEM((G, S, D), F32),
                        pltpu.VMEM((nblk, ATTN_BLOCK, 2 * ATTN_BLOCK), F32),
                        pltpu.VMEM((nblk, ATTN_BLOCK, 2 * ATTN_BLOCK), BF16)],
        compiler_params=_params("parallel", "parallel"),
        name="dilated_attention",
    )(*([q] * G), *([kv] * (2 * G)))


def _kv_rows_kernel(x_ref, o_ref, *, heads):
    o_ref[0] = x_ref[...].reshape(x_ref.shape[0], 2, heads, ATTN_HEAD_DIM)


def _kv_rows(kv, g, keep, B, S, heads):
    W2 = 2 * heads * ATTN_HEAD_DIM
    tm = min(keep, 512)
    first = (S - keep) // tm
    per_seq = S // tm
    return pl.pallas_call(
        functools.partial(_kv_rows_kernel, heads=heads),
        out_shape=jax.ShapeDtypeStruct((B, keep, 2, heads, ATTN_HEAD_DIM), F32),
        grid=(B, keep // tm),
        in_specs=[pl.BlockSpec((tm, W2), lambda b, i: (b * per_seq + first + i, g))],
        out_specs=pl.BlockSpec((1, tm, 2, heads, ATTN_HEAD_DIM), lambda b, i: (b, i, 0, 0, 0)),
        compiler_params=_params("parallel", "parallel"),
        name=f"kv_rows_g{g}",
    )(kv)


def _decode_attn_kernel(q_ref, kvn_ref, *refs):
    G = len(refs) - 1
    buf_refs = refs[:G]
    out_ref = refs[G]
    scale = ATTN_HEAD_DIM ** -0.5
    outs, lses = [], []
    for g in range(G):
        q = q_ref[g]
        k_new = kvn_ref[g, 0]
        v_new = kvn_ref[g, 1]
        k_buf = buf_refs[g][:, 0]
        v_buf = buf_refs[g][:, 1]
        s_buf = jnp.sum(k_buf * q[None], axis=-1, keepdims=True) * scale
        s_new = jnp.sum(k_new * q, axis=-1, keepdims=True) * scale
        mx = jnp.maximum(jnp.max(s_buf, axis=0), s_new)
        e_buf = jnp.exp(s_buf - mx[None])
        e_new = jnp.exp(s_new - mx)
        l = jnp.sum(e_buf, axis=0) + e_new
        outs.append(jnp.sum((e_buf / l[None]) * v_buf, axis=0) + (e_new / l) * v_new)
        lses.append(mx + jnp.log(l))
    mx = functools.reduce(jnp.maximum, lses)
    es = [jnp.exp(l - mx) for l in lses]
    tot = functools.reduce(lambda a, b: a + b, es)
    acc = (es[0] / tot) * outs[0]
    for g in range(1, G):
        acc = acc + (es[g] / tot) * outs[g]
    out_ref[...] = acc


def _decode_attention(q, kv_new, caches):
    Bd, G, H, D = q.shape
    steps = ATTN_BLOCK
    views = []
    for cache, (window, dilation) in zip(caches, DILATED_GROUPS):
        Lb = cache.shape[1]
        assert window // dilation == steps and Lb == steps * dilation
        views.append(cache.reshape(Bd, steps, dilation, 2, H, D))
    return pl.pallas_call(
        _decode_attn_kernel,
        out_shape=jax.ShapeDtypeStruct((Bd, H, D), F32),
        grid=(Bd,),
        in_specs=[pl.BlockSpec((None, G, H, D), lambda b: (b, 0, 0, 0)),
                  pl.BlockSpec((None, G, 2, H, D), lambda b: (b, 0, 0, 0, 0))]
                 + [pl.BlockSpec((None, steps, None, 2, H, D), lambda b: (b, 0, 0, 0, 0, 0)) for _ in views],
        out_specs=pl.BlockSpec((None, H, D), lambda b: (b, 0, 0)),
        compiler_params=_params("parallel"),
        name="decode_attention",
    )(q, kv_new, *views)


def _router_kernel(h_ref, g_ref, wt_ref, b_ref, xn_ref, o_ref):
    xn = _rms(h_ref[...]) * g_ref[...]
    xn_ref[...] = xn
    logits = _dot_nt_3pass(xn, wt_ref[...]) + b_ref[...]
    e = jnp.exp(logits - jnp.max(logits, axis=1, keepdims=True))
    p = e / jnp.sum(e, axis=1, keepdims=True)
    E = p.shape[1]
    idx = lax.broadcasted_iota(I32, p.shape, 1)
    p1 = jnp.max(p, axis=1, keepdims=True)
    i1 = jnp.min(jnp.where(p == p1, idx, E), axis=1, keepdims=True)
    rest = jnp.where(idx == i1, -1.0, p)
    p2 = jnp.max(rest, axis=1, keepdims=True)
    i2 = jnp.min(jnp.where(rest == p2, idx, E), axis=1, keepdims=True)
    tot = p1 + p2
    o_ref[...] = jnp.where(idx == 0, i1.astype(F32),
                 jnp.where(idx == 1, i2.astype(F32),
                 jnp.where(idx == 2, p1 / tot,
                 jnp.where(idx == 3, p2 / tot, 0.0))))


def _router(h, gain, w_router_t, b_router):
    T, D = h.shape
    E = w_router_t.shape[0]
    tm = _row_tile(T, 512)
    return pl.pallas_call(
        _router_kernel,
        out_shape=[jax.ShapeDtypeStruct((T, D), F32), jax.ShapeDtypeStruct((T, E), F32)],
        grid=(T // tm,),
        in_specs=[pl.BlockSpec((tm, D), lambda i: (i, 0)),
                  pl.BlockSpec((1, D), lambda i: (0, 0)),
                  pl.BlockSpec((E, D), lambda i: (0, 0)),
                  pl.BlockSpec((1, E), lambda i: (0, 0))],
        out_specs=[pl.BlockSpec((tm, D), lambda i: (i, 0)), pl.BlockSpec((tm, E), lambda i: (i, 0))],
        compiler_params=_params("parallel"),
        name="router",
    )(h, gain, w_router_t, b_router.reshape(1, E))


def _scatter_plan(top_i, tm, n_tiles):
    T = top_i.shape[0]
    R = n_tiles * tm
    e_flat = top_i.reshape(-1)
    onehot = (e_flat[:, None] == jnp.arange(N_EXPERTS, dtype=I32)[None, :]).astype(I32)
    csum = jnp.cumsum(onehot, axis=0)
    rank = jnp.sum(onehot * csum, axis=1) - 1
    counts = csum[-1]
    gsz = ((counts + tm - 1) // tm) * tm
    gend = jnp.cumsum(gsz)
    gstart = gend - gsz
    pos = (jnp.sum(onehot * gstart[None, :], axis=1) + rank).astype(I32)
    a = jnp.arange(TOP_K * T, dtype=I32)
    dst_assigned = jnp.zeros((R,), I32).at[pos].set((a % TOP_K) * T + a // TOP_K)
    r = jnp.arange(R, dtype=I32)
    e_row = jnp.minimum(jnp.sum((r[:, None] >= gend[None, :]).astype(I32), axis=1), N_EXPERTS - 1)
    row_onehot = (e_row[:, None] == jnp.arange(N_EXPERTS, dtype=I32)[None, :]).astype(I32)
    in_group = r - jnp.sum(row_onehot * gstart[None, :], axis=1)
    assigned = jnp.logical_and(r < gend[-1], in_group < jnp.sum(row_onehot * counts[None, :], axis=1))
    pad_rank = jnp.cumsum(jnp.logical_not(assigned).astype(I32)) - 1
    dst = jnp.where(assigned, dst_assigned, TOP_K * T + pad_rank).astype(I32)
    tile_start = jnp.arange(n_tiles, dtype=I32) * tm
    tile_valid = (tile_start < gend[-1]).astype(I32)
    tile_e = jnp.sum((tile_start[:, None] >= gend[None, :]).astype(I32), axis=1)
    last_e = jnp.sum((gend[-1] - 1 >= gend).astype(I32))
    tile_e = jnp.where(tile_valid > 0, tile_e, last_e).astype(I32)
    return dst, tile_e, tile_valid


def _scatter_experts_kernel(te_ref, tv_ref, tok_ref, dst_ref, x_hbm, wg_ref, wu_ref, wd_ref, y_hbm,
                            xf_ref, xb_ref, acc_ref, out_ref, gsem, ssem, zsem, *, tm, per_step, accurate):
    i = pl.program_id(0)
    c = pl.program_id(1)
    n_tiles = pl.num_programs(0)
    nf = pl.num_programs(1)
    valid = tv_ref[i] > 0
    prefetch = jnp.logical_and(i + 1 < n_tiles, tv_ref[jnp.minimum(i + 1, n_tiles - 1)] > 0)
    drain = jnp.logical_and(i > 0, tv_ref[jnp.maximum(i - 1, 0)] > 0)
    D = xb_ref.shape[1]
    dn = min(D, 512)

    def gather_copy(tile, r):
        return pltpu.make_async_copy(x_hbm.at[pl.ds(tok_ref[tile * tm + r], 1)], xf_ref.at[pl.ds(r, 1)],
                                     gsem.at[0])

    def scatter_copy(tile, r):
        return pltpu.make_async_copy(out_ref.at[pl.ds(r, 1)], y_hbm.at[pl.ds(dst_ref[tile * tm + r], 1)],
                                     ssem.at[0])

    @pl.when(jnp.logical_and(i == 0, c == 0))
    def _():
        def start(r, carry):
            gather_copy(0, r).start()
            return carry

        lax.fori_loop(0, tm, start, 0)

    @pl.when(c == 0)
    def _():
        acc_ref[...] = jnp.zeros_like(acc_ref)

    @pl.when(jnp.logical_and(jnp.logical_not(valid), c == 0))
    def _():
        fill = pltpu.make_async_copy(acc_ref, y_hbm.at[pl.ds(pl.multiple_of(i * tm, 8), tm)], zsem.at[0])
        fill.start()
        fill.wait()

    @pl.when(jnp.logical_and(valid, c == 0))
    def _():
        for _ in range(tm):
            pltpu.make_async_copy(x_hbm.at[pl.ds(0, 1)], xf_ref.at[pl.ds(0, 1)], gsem.at[0]).wait()
        xb_ref[...] = xf_ref[...].astype(xb_ref.dtype)

    def mm(x, w):
        wh = w.astype(BF16)
        return _dot_3pass(x, w, wh) if accurate else _dot(x, wh)

    def chunk(compute, with_prefetch, with_drain):
        if with_prefetch:
            for j in range(per_step):
                gather_copy(i + 1, c * per_step + j).start()
        if with_drain:
            for j in range(per_step):
                scatter_copy(i - 1, c * per_step + j).start()
        if compute:
            x = xb_ref[...]
            g = mm(x, wg_ref[0])
            u = mm(x, wu_ref[0])
            act = (g * _sigmoid(g) * u).astype(xb_ref.dtype)
            for n0 in range(0, D, dn):
                acc_ref[:, n0:n0 + dn] += mm(act, wd_ref[0, :, n0:n0 + dn])

    for compute, pf, dr in ((True, True, True), (True, True, False), (True, False, True),
                            (True, False, False), (False, False, True)):
        cond = valid if compute else jnp.logical_not(valid)
        cond = jnp.logical_and(cond, prefetch if pf else jnp.logical_not(prefetch))
        cond = jnp.logical_and(cond, drain if dr else jnp.logical_not(drain))
        pl.when(cond)(functools.partial(chunk, compute, pf, dr))

    @pl.when(jnp.logical_and(drain, c == nf - 1))
    def _():
        for _ in range(tm):
            pltpu.make_async_copy(out_ref.at[pl.ds(0, 1)], y_hbm.at[pl.ds(0, 1)], ssem.at[0]).wait()

    @pl.when(jnp.logical_and(valid, c == nf - 1))
    def _():
        out_ref[...] = acc_ref[...]


def _scatter_experts(xn, plan, w_gate, w_up, w_down, tm, tf, accurate):
    dst, tile_e, tile_valid = plan
    T, D = xn.shape
    Fe = w_gate.shape[-1]
    nf = Fe // tf
    n_tiles = tile_e.shape[0]
    per_step = tm // nf
    assert per_step * nf == tm
    tok = jnp.where(dst >= TOP_K * T, 0, dst % T).astype(I32)

    def chunk(i, c, tv):
        return jnp.where(tv[i] > 0, c, nf - 1)

    weight = lambda shape, index: pl.BlockSpec((None, 1) + shape,
                                               lambda i, c, te, tv, tok, dst: (0, te[i]) + index(chunk(i, c, tv)))
    return pl.pallas_call(
        functools.partial(_scatter_experts_kernel, tm=tm, per_step=per_step, accurate=accurate),
        out_shape=jax.ShapeDtypeStruct((n_tiles * tm, D), F32),
        grid_spec=pltpu.PrefetchScalarGridSpec(
            num_scalar_prefetch=4,
            grid=(n_tiles, nf),
            in_specs=[pl.BlockSpec(memory_space=pl.ANY),
                      weight((D, tf), lambda ch: (0, ch)),
                      weight((D, tf), lambda ch: (0, ch)),
                      weight((tf, D), lambda ch: (ch, 0))],
            out_specs=pl.BlockSpec(memory_space=pl.ANY),
            scratch_shapes=[pltpu.VMEM((tm, D), F32), pltpu.VMEM((tm, D), F32 if accurate else BF16),
                            pltpu.VMEM((tm, D), F32), pltpu.VMEM((tm, D), F32),
                            pltpu.SemaphoreType.DMA((1,)), pltpu.SemaphoreType.DMA((1,)),
                            pltpu.SemaphoreType.DMA((1,))]),
        compiler_params=_params("arbitrary", "arbitrary"),
        name="experts_scatter",
    )(tile_e, tile_valid, tok, dst, xn, w_gate, w_up, w_down)


def _mix_kernel(h_ref, gate_ref, y0_ref, y1_ref, fg_ref, o_ref):
    hsum = h_ref[...] + (gate_ref[:, 0:1] * y0_ref[...] + gate_ref[:, 1:2] * y1_ref[...])
    o_ref[...] = _rms(hsum) * fg_ref[...]


def _mix(h, gates, y_slots, final_gain):
    T, D = h.shape
    tm = _row_tile(T, 512)
    nt = T // tm
    return pl.pallas_call(
        _mix_kernel,
        out_shape=jax.ShapeDtypeStruct((T, D), F32),
        grid=(nt,),
        in_specs=[pl.BlockSpec((tm, D), lambda i: (i, 0)),
                  pl.BlockSpec((tm, TOP_K), lambda i: (i, 0)),
                  pl.BlockSpec((tm, D), lambda i: (i, 0)),
                  pl.BlockSpec((tm, D), lambda i: (i + nt, 0)),
                  pl.BlockSpec((1, D), lambda i: (0, 0))],
        out_specs=pl.BlockSpec((tm, D), lambda i: (i, 0)),
        compiler_params=_params("parallel"),
        name="moe_mix",
    )(h, gates, y_slots, y_slots, final_gain)


def _forward(xp, xs, state, kv_caches, B, S, Bd, p):
    D = xp.shape[1]
    hv = p["mlstm_out_norm"].shape[-1]
    hk = hv // 2
    heads = p["attn_w_out"].shape[1] // ATTN_HEAD_DIM
    W = heads * ATTN_HEAD_DIM
    n_groups = len(DILATED_GROUPS)
    pad_rows = lambda a: jnp.zeros((SAMPLE_ROWS, a.shape[1]), F32).at[:Bd].set(a)

    gain0 = p["norm_mix"][0:1]
    w_gate_t = p["mlstm_w_in"][0, :, 2 * hk + 2 * hv:].T
    xn, gc, gr = _gates(xp, gain0, w_gate_t, p["mlstm_b_gates"][0], BF16, True)
    xn_s, gc_s = _gates(xs, gain0, w_gate_t, p["mlstm_b_gates"][0], F32, False)
    proj, proj_s = _matmul(xn, xn_s, p["mlstm_w_in"], 2 * hk + 2 * hv, F32)
    hg, C_p, n_p, m_p = _mlstm_prompt(proj, gc, gr, p["mlstm_out_norm"], B, S)
    hg_s, C_s, n_s, m_s = _mlstm_decode(proj_s, gc_s, p["mlstm_out_norm"], *state)
    h1, h1_s = _matmul(hg, pad_rows(hg_s[:, 0]), p["mlstm_w_out"], D, F32, res=xp, res_tail=xs)

    (xn,) = _norm(h1, p["norm_ffn"][0:1], [BF16])
    (xn_s,) = _norm(h1_s, p["norm_ffn"][0:1], [F32])
    act, act_s = _glu(xn, xn_s, p["ffn_w_gate"], p["ffn_w_up"])
    h2, h2_s = _matmul(act, act_s, p["ffn_w_down"], D, F32, res=h1, res_tail=h1_s, tn=512)

    kvq_gains = jnp.concatenate([p["kv_norm"][None], p["norm_mix"][1:2]])
    xkv, xq = _norm(h2, kvq_gains, [BF16, BF16])
    xkv_s, xq_s = _norm(h2_s, kvq_gains, [F32, F32])
    kv, kv_s = _matmul(xkv, xkv_s, p["w_kv"], n_groups * 2 * W, F32)
    q, q_s = _matmul(xq, xq_s, p["attn_w_q"], n_groups * W, F32)
    att = _attention(q, kv, B, S, heads)
    kv_rows_p = [_kv_rows(kv, g, min(window, S), B, S, heads)
                 for g, (window, _) in enumerate(DILATED_GROUPS)]
    kv5 = kv_s[:Bd].reshape(Bd, n_groups, 2, heads, ATTN_HEAD_DIM)
    att_s = _decode_attention(q_s[:Bd].reshape(Bd, n_groups, heads, ATTN_HEAD_DIM), kv5, kv_caches)
    kv_rows_s = [kv5[:, None, g] for g in range(n_groups)]
    h3, h3_s = _matmul(att, pad_rows(att_s.reshape(Bd, W)), p["attn_w_out"], D, F32, res=h2, res_tail=h2_s)

    y_p = _moe_and_final_norm(h3, p, PROMPT_EXPERT_ROWS, False)
    y_s = _moe_and_final_norm(h3_s, p, SAMPLE_EXPERT_ROWS, True)

    states_p = (C_p[None], n_p[:, :, 0][None], m_p[:, :, 0, 0][None])
    states_s = (C_s[None], n_s[:, :, 0][None], m_s[:, :, 0, 0][None])
    return y_p, y_s[:Bd], states_p, kv_rows_p, states_s, kv_rows_s


def _moe_and_final_norm(h, p, tm_e, accurate):
    T = h.shape[0]
    xn, route = _router(h, p["norm_ffn"][1:2], p["moe_w_router"][0].T, p["moe_b_router"][0])
    top_i = route[:, :TOP_K].astype(I32)
    gates = route[:, TOP_K:2 * TOP_K]
    n_tiles = -(-(TOP_K * T) // tm_e) + N_EXPERTS
    y_slots = _scatter_experts(xn, _scatter_plan(top_i, tm_e, n_tiles), p["moe_w_gate"], p["moe_w_up"],
                               p["moe_w_down"], tm_e, EXPERT_CHUNK, accurate)
    return _mix(h, gates, y_slots, p["final_norm"][None])


def kernel(x_prompt, x_sample, state_mlstm_C, state_mlstm_n, state_mlstm_m, cache_kv_w128, cache_kv_w512, cache_kv_w2048, norm_mix, norm_ffn, mlstm_w_in, mlstm_b_gates, mlstm_out_norm, mlstm_w_out, kv_norm, w_kv, attn_w_q, attn_w_out, ffn_w_gate, ffn_w_up, ffn_w_down, moe_w_router, moe_b_router, moe_w_gate, moe_w_up, moe_w_down, final_norm):
    assert norm_mix.shape[0] == 2 and mlstm_w_in.shape[0] == 1 and attn_w_q.shape[0] == 1
    p = dict(norm_mix=norm_mix, norm_ffn=norm_ffn, mlstm_w_in=mlstm_w_in, mlstm_b_gates=mlstm_b_gates,
             mlstm_out_norm=mlstm_out_norm, mlstm_w_out=mlstm_w_out, kv_norm=kv_norm, w_kv=w_kv,
             attn_w_q=attn_w_q, attn_w_out=attn_w_out, ffn_w_gate=ffn_w_gate, ffn_w_up=ffn_w_up,
             ffn_w_down=ffn_w_down, moe_w_router=moe_w_router, moe_b_router=moe_b_router,
             moe_w_gate=moe_w_gate, moe_w_up=moe_w_up, moe_w_down=moe_w_down, final_norm=final_norm)
    B, S, D = x_prompt.shape
    Bd = x_sample.shape[0]
    assert x_sample.shape[1] == 1 and Bd <= SAMPLE_ROWS

    xs = jnp.zeros((SAMPLE_ROWS, D), F32).at[:Bd].set(x_sample[:, 0])
    state = (state_mlstm_C[0], state_mlstm_n[0], state_mlstm_m[0])
    y_p, y_s, states_p, kv_rows_p, states_s, kv_rows_s = _forward(
        x_prompt.reshape(B * S, D), xs, state, (cache_kv_w128, cache_kv_w512, cache_kv_w2048), B, S, Bd, p)
    return (y_p.reshape(B, S, D), y_s[:, None], *states_p, *kv_rows_p, *states_s, *kv_rows_s)
```

```python
import functools
import math

import jax
import jax.numpy as jnp
from jax import lax
from jax.experimental import pallas as pl
from jax.experimental.pallas import tpu as pltpu

F32 = jnp.float32
BF16 = jnp.bfloat16
I32 = jnp.int32

EPS = 1e-6
GATE_SOFTCAP = 15.0
MLSTM_HEADS = 4
ATTN_HEAD_DIM = 128
DILATED_GROUPS = ((128, 1), (512, 4), (2048, 16))
N_EXPERTS = 8
TOP_K = 2
MLSTM_CHUNK = 256
ATTN_BLOCK = 128
SAMPLE_ROWS = 16
EXPERT_CHUNK = 256
PROMPT_EXPERT_ROWS = 880

VMEM_LIMIT = 56 * 1024 * 1024


def _params(*sem):
    return pltpu.CompilerParams(dimension_semantics=sem, vmem_limit_bytes=VMEM_LIMIT)


def _dot(a, b):
    return jnp.dot(a, b, preferred_element_type=F32)


def _dot_nt(a, b):
    return lax.dot_general(a, b, (((1,), (1,)), ((), ())), preferred_element_type=F32)


def _dot_tn(a, b):
    return lax.dot_general(a, b, (((0,), (0,)), ((), ())), preferred_element_type=F32)


def _split(x):
    hi = x.astype(BF16)
    lo = (x - hi.astype(F32)).astype(BF16)
    return hi, lo


def _dot_nt_3pass(x, w):
    xh, xl = _split(x)
    wh, wl = _split(w)
    return _dot_nt(xh, wh) + (_dot_nt(xl, wh) + _dot_nt(xh, wl))


def _dot_3pass(x, w, wh, cols=256):
    n = x.shape[0]
    xh, xl = _split(x)
    xs = jnp.concatenate([xh, xl], axis=0)
    outs = []
    for c0 in range(0, w.shape[1], cols):
        wh_c = wh[:, c0:c0 + cols]
        wl_c = (w[:, c0:c0 + cols] - wh_c.astype(F32)).astype(BF16)
        r = _dot(xs, wh_c)
        outs.append(r[:n] + (r[n:] + _dot(xh, wl_c)))
    return outs[0] if len(outs) == 1 else jnp.concatenate(outs, axis=1)


def _stack_tail(a_ref, at_ref, as_ref):
    tm = a_ref.shape[0]
    n = at_ref.shape[0]
    xh, xl = _split(at_ref[...])
    as_ref[0:tm, :] = a_ref[...]
    as_ref[tm:tm + n, :] = xh
    as_ref[tm + n:, :] = xl
    return xh


def _dot_stacked(as_ref, xh, w_ref, wh_ref, cols=256):
    n = xh.shape[0]
    tm = as_ref.shape[0] - 2 * n
    acc = _dot(as_ref[...], wh_ref[...])
    los = []
    for c0 in range(0, w_ref.shape[1], cols):
        wl_c = (w_ref[:, c0:c0 + cols] - wh_ref[:, c0:c0 + cols].astype(F32)).astype(BF16)
        los.append(_dot(xh, wl_c))
    lo = los[0] if len(los) == 1 else jnp.concatenate(los, axis=1)
    return acc[:tm], acc[tm:tm + n] + (acc[tm + n:] + lo)


def _sigmoid(x):
    return 1.0 / (1.0 + jnp.exp(-x))


def _rms(x):
    return x * lax.rsqrt(jnp.mean(x * x, axis=-1, keepdims=True) + EPS)


def _norm_kernel(x_ref, g_ref, *o_refs):
    y = _rms(x_ref[...])
    for i, o_ref in enumerate(o_refs):
        o_ref[...] = (y * g_ref[i:i + 1, :]).astype(o_ref.dtype)


def _row_tile(M, cap):
    t = (min(cap, M) // 16) * 16
    while M % t:
        t -= 16
    return t


def _norm(x, gains, dtypes):
    T, D = x.shape
    tm = _row_tile(T, 1024)
    return pl.pallas_call(
        _norm_kernel,
        out_shape=[jax.ShapeDtypeStruct((T, D), dt) for dt in dtypes],
        grid=(T // tm,),
        in_specs=[pl.BlockSpec((tm, D), lambda i: (i, 0)),
                  pl.BlockSpec(gains.shape, lambda i: (0, 0))],
        out_specs=[pl.BlockSpec((tm, D), lambda i: (i, 0)) for _ in dtypes],
        compiler_params=_params("parallel"),
        name="rmsnorm",
    )(x, gains)


def _mm_rows(M, K):
    return _row_tile(M, 1024 if K <= 4096 else 512)


def _weight_spec(w, tn):
    if w.ndim == 3:
        return pl.BlockSpec((None, w.shape[1], tn), lambda j, i: (0, 0, j))
    return pl.BlockSpec((w.shape[0], tn), lambda j, i: (0, j))


def _mm_kernel(a_ref, at_ref, w_ref, *rest, has_res):
    if has_res:
        r_ref, rt_ref, o_ref, ot_ref, wb_ref, as_ref = rest
    else:
        o_ref, ot_ref, wb_ref, as_ref = rest
    i = pl.program_id(1)
    last = pl.num_programs(1) - 1

    @pl.when(i == 0)
    def _():
        wb_ref[...] = w_ref[...].astype(BF16)

    @pl.when(i < last)
    def _():
        acc = _dot(a_ref[...], wb_ref[...])
        o_ref[...] = (r_ref[...] + acc if has_res else acc).astype(o_ref.dtype)

    @pl.when(i == last)
    def _():
        xh = _stack_tail(a_ref, at_ref, as_ref)
        acc, acc_t = _dot_stacked(as_ref, xh, w_ref, wb_ref)
        o_ref[...] = (r_ref[...] + acc if has_res else acc).astype(o_ref.dtype)
        ot_ref[...] = rt_ref[...] + acc_t if has_res else acc_t


def _matmul(a, a_tail, w, n_cols, out_dtype, res=None, res_tail=None, tn=1024):
    M, K = a.shape
    n = a_tail.shape[0]
    tm = _mm_rows(M, K)
    tn = min(tn, n_cols)
    in_specs = [pl.BlockSpec((tm, K), lambda j, i: (i, 0)),
                pl.BlockSpec((n, K), lambda j, i: (0, 0)), _weight_spec(w, tn)]
    args = [a, a_tail, w]
    if res is not None:
        in_specs += [pl.BlockSpec((tm, tn), lambda j, i: (i, j)), pl.BlockSpec((n, tn), lambda j, i: (0, j))]
        args += [res, res_tail]
    return pl.pallas_call(
        functools.partial(_mm_kernel, has_res=res is not None),
        out_shape=[jax.ShapeDtypeStruct((M, n_cols), out_dtype), jax.ShapeDtypeStruct((n, n_cols), F32)],
        grid=(n_cols // tn, M // tm),
        in_specs=in_specs,
        out_specs=[pl.BlockSpec((tm, tn), lambda j, i: (i, j)), pl.BlockSpec((n, tn), lambda j, i: (0, j))],
        scratch_shapes=[pltpu.VMEM((K, tn), BF16), pltpu.VMEM((tm + 2 * n, K), BF16)],
        compiler_params=_params("arbitrary", "arbitrary"),
        name="matmul",
    )(*args)


def _glu_kernel(a_ref, at_ref, wg_ref, wu_ref, o_ref, ot_ref, wgb_ref, wub_ref, as_ref):
    i = pl.program_id(1)
    last = pl.num_programs(1) - 1

    @pl.when(i == 0)
    def _():
        wgb_ref[...] = wg_ref[...].astype(BF16)
        wub_ref[...] = wu_ref[...].astype(BF16)

    @pl.when(i < last)
    def _():
        a = a_ref[...]
        g = _dot(a, wgb_ref[...])
        u = _dot(a, wub_ref[...])
        o_ref[...] = (g * _sigmoid(g) * u).astype(o_ref.dtype)

    @pl.when(i == last)
    def _():
        xh = _stack_tail(a_ref, at_ref, as_ref)
        g, g_t = _dot_stacked(as_ref, xh, wg_ref, wgb_ref)
        u, u_t = _dot_stacked(as_ref, xh, wu_ref, wub_ref)
        o_ref[...] = (g * _sigmoid(g) * u).astype(o_ref.dtype)
        ot_ref[...] = g_t * _sigmoid(g_t) * u_t


def _glu(a, a_tail, wg, wu, tf=512):
    M, K = a.shape
    n = a_tail.shape[0]
    F = wg.shape[-1]
    tm = _mm_rows(M, K)
    return pl.pallas_call(
        _glu_kernel,
        out_shape=[jax.ShapeDtypeStruct((M, F), BF16), jax.ShapeDtypeStruct((n, F), F32)],
        grid=(F // tf, M // tm),
        in_specs=[pl.BlockSpec((tm, K), lambda j, i: (i, 0)),
                  pl.BlockSpec((n, K), lambda j, i: (0, 0)), _weight_spec(wg, tf), _weight_spec(wu, tf)],
        out_specs=[pl.BlockSpec((tm, tf), lambda j, i: (i, j)), pl.BlockSpec((n, tf), lambda j, i: (0, j))],
        scratch_shapes=[pltpu.VMEM((K, tf), BF16), pltpu.VMEM((K, tf), BF16),
                        pltpu.VMEM((tm + 2 * n, K), BF16)],
        compiler_params=_params("arbitrary", "arbitrary"),
        name="swiglu_up",
    )(a, a_tail, wg, wu)


def _gates_kernel(x_ref, g_ref, wt_ref, b_ref, xn_ref, gc_ref, *gr_refs):
    H = MLSTM_HEADS
    xn = _rms(x_ref[...]) * g_ref[...]
    xn_ref[...] = xn.astype(xn_ref.dtype)
    pre = _dot_nt_3pass(xn, wt_ref[...]) + b_ref[...]
    t = GATE_SOFTCAP * jnp.tanh(pre / GATE_SOFTCAP)
    ls = jnp.minimum(t, 0.0) - jnp.log1p(jnp.exp(-jnp.abs(t)))
    lane = lax.broadcasted_iota(I32, t.shape, 1)
    fin = jnp.where(lane < H, t, ls)
    gc_ref[...] = fin[:, :2 * H]
    if gr_refs:
        gr_refs[0][...] = fin.T[:2 * H, :]


def _gates(x, gain, w_gate_t, b_gates, xn_dtype, head_major):
    T, D = x.shape
    G = w_gate_t.shape[0]
    tm = min(T, 512)
    wt = jnp.zeros((128, D), F32).at[:G].set(w_gate_t)
    b = jnp.zeros((1, 128), F32).at[0, :G].set(b_gates)
    out_shape = [jax.ShapeDtypeStruct((T, D), xn_dtype), jax.ShapeDtypeStruct((T, G), F32)]
    out_specs = [pl.BlockSpec((tm, D), lambda i: (i, 0)), pl.BlockSpec((tm, G), lambda i: (i, 0))]
    if head_major:
        out_shape.append(jax.ShapeDtypeStruct((G, T), F32))
        out_specs.append(pl.BlockSpec((G, tm), lambda i: (0, i)))
    return pl.pallas_call(
        _gates_kernel,
        out_shape=out_shape,
        grid=(T // tm,),
        in_specs=[pl.BlockSpec((tm, D), lambda i: (i, 0)),
                  pl.BlockSpec((1, D), lambda i: (0, 0)),
                  pl.BlockSpec((128, D), lambda i: (0, 0)),
                  pl.BlockSpec((1, 128), lambda i: (0, 0))],
        out_specs=out_specs,
        compiler_params=_params("parallel"),
        name="mlstm_gates",
    )(x, gain, wt, b)


def _mlstm_kernel(q_ref, k_ref, v_ref, o_ref, gc_ref, gr_ref, on_ref,
                  h_ref, c_out, n_out, m_out, c_s, n_s, m_s, *, dk, dv):
    H = MLSTM_HEADS
    c = pl.program_id(1)
    L = q_ref.shape[0]

    @pl.when(c == 0)
    def _():
        c_s[...] = jnp.zeros_like(c_s)
        n_s[...] = jnp.zeros_like(n_s)
        m_s[...] = jnp.zeros_like(m_s)

    row = lax.broadcasted_iota(I32, (L, L), 0)
    col = lax.broadcasted_iota(I32, (L, L), 1)
    tril = row >= col
    for h in range(H):
        q = q_ref[:, h * dk:(h + 1) * dk]
        k = k_ref[:, h * dk:(h + 1) * dk] * (dk ** -0.5)
        v = v_ref[:, h * dv:(h + 1) * dv]
        ig_c = gc_ref[:, h:h + 1]
        lf_c = gc_ref[:, H + h:H + h + 1]
        ig_r = gr_ref[h:h + 1, :]
        lf_r = gr_ref[H + h:H + h + 1, :]
        b_c = jnp.sum(jnp.where(tril, lf_r, 0.0), axis=1, keepdims=True)
        b_r = jnp.sum(jnp.where(row <= col, lf_c, 0.0), axis=0, keepdims=True)
        m_prev = m_s[h][:, 0:1]
        C = c_s[h]
        n = n_s[h]
        dmat = jnp.where(tril, b_c - b_r + ig_r, -jnp.inf)
        inter = b_c + m_prev
        m_t = jnp.maximum(inter, jnp.max(dmat, axis=1, keepdims=True))
        w_intra = jnp.exp(dmat - m_t)
        w_inter = jnp.exp(inter - m_t)
        qb = q.astype(BF16)
        vb = v.astype(BF16)
        s = _dot_nt(qb, k.astype(BF16)) * w_intra
        num = w_inter * _dot(qb, C.astype(BF16)) + _dot(s.astype(BF16), vb)
        den = w_inter * jnp.sum(q * n, axis=1, keepdims=True) + jnp.sum(s, axis=1, keepdims=True)
        hh = num / jnp.maximum(jnp.abs(den), jnp.exp(-m_t))
        b_last = b_c[L - 1:L, :]
        g_r = b_last - b_r + ig_r
        g_c = b_last - b_c + ig_c
        m_new = jnp.maximum(b_last + m_prev, jnp.max(g_r, axis=1, keepdims=True))
        decay = jnp.exp(b_last + m_prev - m_new)
        kw = k * jnp.exp(g_c - m_new)
        c_s[h] = decay * C + _dot_tn(kw.astype(BF16), vb)
        n_s[h] = decay * n + jnp.sum(kw, axis=0, keepdims=True)
        m_s[h] = jnp.broadcast_to(m_new, m_s.shape[1:])
        sl = slice(h * dv, (h + 1) * dv)
        h_ref[:, sl] = (_rms(hh) * on_ref[:, sl] * _sigmoid(o_ref[:, sl])).astype(h_ref.dtype)

    @pl.when(c == pl.num_programs(1) - 1)
    def _():
        c_out[0] = c_s[...]
        n_out[0] = n_s[...]
        m_out[0] = m_s[...]


def _mlstm_prompt(proj, gc, gr, out_norm, B, S):
    H = MLSTM_HEADS
    T = B * S
    hv = out_norm.shape[1]
    dv = hv // H
    dk = dv // 2
    hk = H * dk
    L = MLSTM_CHUNK
    nc = S // L
    G = gc.shape[1]
    rows = lambda b, c: b * nc + c
    return pl.pallas_call(
        functools.partial(_mlstm_kernel, dk=dk, dv=dv),
        out_shape=[jax.ShapeDtypeStruct((T, hv), BF16),
                   jax.ShapeDtypeStruct((B, H, dk, dv), F32),
                   jax.ShapeDtypeStruct((B, H, 1, dk), F32),
                   jax.ShapeDtypeStruct((B, H, 1, 128), F32)],
        grid=(B, nc),
        in_specs=[pl.BlockSpec((L, hk), lambda b, c: (rows(b, c), 0)),
                  pl.BlockSpec((L, hk), lambda b, c: (rows(b, c), 1)),
                  pl.BlockSpec((L, hv), lambda b, c: (rows(b, c), 1)),
                  pl.BlockSpec((L, hv), lambda b, c: (rows(b, c), 2)),
                  pl.BlockSpec((L, G), lambda b, c: (rows(b, c), 0)),
                  pl.BlockSpec((G, L), lambda b, c: (0, rows(b, c))),
                  pl.BlockSpec((1, hv), lambda b, c: (0, 0))],
        out_specs=[pl.BlockSpec((L, hv), lambda b, c: (rows(b, c), 0)),
                   pl.BlockSpec((1, H, dk, dv), lambda b, c: (b, 0, 0, 0)),
                   pl.BlockSpec((1, H, 1, dk), lambda b, c: (b, 0, 0, 0)),
                   pl.BlockSpec((1, H, 1, 128), lambda b, c: (b, 0, 0, 0))],
        scratch_shapes=[pltpu.VMEM((H, dk, dv), F32), pltpu.VMEM((H, 1, dk), F32),
                        pltpu.VMEM((H, 1, 128), F32)],
        compiler_params=_params("arbitrary", "arbitrary"),
        name="mlstm_prompt",
    )(proj, proj, proj, proj, gc, gr, out_norm)


def _mlstm_step_kernel(qr_ref, qc_ref, kr_ref, kc_ref, v_ref, o_ref, ig_ref, lf_ref, on_ref,
                       c_ref, n_ref, m_ref, h_ref, c_out, n_out, m_out):
    H = MLSTM_HEADS
    dk = qr_ref.shape[-1]
    dv = v_ref.shape[-1]
    for h in range(H):
        q_r = qr_ref[0, h]
        q_c = qc_ref[0, h]
        k_r = kr_ref[0, h] * (dk ** -0.5)
        k_c = kc_ref[0, h] * (dk ** -0.5)
        v = v_ref[0, h]
        ig = ig_ref[0, h]
        lf = lf_ref[0, h]
        C = c_ref[0, h]
        n = n_ref[0, h]
        m = m_ref[0, h]
        inter = lf + m
        m_t = jnp.maximum(inter, ig)
        w_intra = jnp.exp(ig - m_t)
        w_inter = jnp.exp(inter - m_t)
        s = jnp.sum(q_r * k_r, axis=1, keepdims=True) * w_intra
        num = w_inter * jnp.sum(q_c * C, axis=0, keepdims=True) + s * v
        den = w_inter * jnp.sum(q_r * n, axis=1, keepdims=True) + s
        hh = num / jnp.maximum(jnp.abs(den), jnp.exp(-m_t))
        decay = jnp.exp(inter - m_t)
        w_k = jnp.exp(ig - m_t)
        c_out[0, h] = decay * C + (k_c * w_k) * v
        n_out[0, h] = decay * n + k_r * w_k
        m_out[0, h] = m_t
        sl = slice(h * dv, (h + 1) * dv)
        h_ref[0, :, sl] = _rms(hh) * on_ref[:, sl] * _sigmoid(o_ref[0, :, sl])


def _mlstm_decode(proj, gc, out_norm, C0, n0, m0):
    H = MLSTM_HEADS
    Bd, _, dk, dv = C0.shape
    hk, hv = H * dk, H * dv
    q = proj[:Bd, :hk].reshape(Bd, H, dk)
    k = proj[:Bd, hk:2 * hk].reshape(Bd, H, dk)
    v = proj[:Bd, 2 * hk:2 * hk + hv].reshape(Bd, H, 1, dv)
    o = proj[:Bd, 2 * hk + hv:2 * hk + 2 * hv].reshape(Bd, 1, hv)
    ig = gc[:Bd, :H].reshape(Bd, H, 1, 1)
    lf = gc[:Bd, H:].reshape(Bd, H, 1, 1)
    spec = lambda *tail: pl.BlockSpec((1, H) + tail, lambda b: (b, 0, 0, 0))
    return pl.pallas_call(
        _mlstm_step_kernel,
        out_shape=[jax.ShapeDtypeStruct((Bd, 1, hv), F32),
                   jax.ShapeDtypeStruct((Bd, H, dk, dv), F32),
                   jax.ShapeDtypeStruct((Bd, H, 1, dk), F32),
                   jax.ShapeDtypeStruct((Bd, H, 1, 1), F32)],
        grid=(Bd,),
        in_specs=[spec(1, dk), spec(dk, 1), spec(1, dk), spec(dk, 1), spec(1, dv),
                  pl.BlockSpec((1, 1, hv), lambda b: (b, 0, 0)),
                  spec(1, 1), spec(1, 1),
                  pl.BlockSpec((1, hv), lambda b: (0, 0)),
                  spec(dk, dv), spec(1, dk), spec(1, 1)],
        out_specs=[pl.BlockSpec((1, 1, hv), lambda b: (b, 0, 0)),
                   spec(dk, dv), spec(1, dk), spec(1, 1)],
        compiler_params=_params("parallel"),
        name="mlstm_decode",
    )(q.reshape(Bd, H, 1, dk), q.reshape(Bd, H, dk, 1), k.reshape(Bd, H, 1, dk),
      k.reshape(Bd, H, dk, 1), v, o, ig, lf, out_norm, C0, n0.reshape(Bd, H, 1, dk),
      m0.reshape(Bd, H, 1, 1))


def _rows(start, size, stride):
    return pl.ds(start, size) if stride == 1 else pl.ds(start, size, stride=stride)


def _attn_kernel(*refs, dilations, seq):
    G = len(dilations)
    q_refs, k_refs, v_refs = refs[:G], refs[G:2 * G], refs[2 * G:3 * G]
    out_ref = refs[3 * G]
    o_scr, lse_scr, s_scr, p_scr = refs[3 * G + 1:]
    bq = ATTN_BLOCK
    D = ATTN_HEAD_DIM
    scale = D ** -0.5 * math.log2(math.e)
    row = lax.broadcasted_iota(I32, (bq, 2 * bq), 0)
    col = lax.broadcasted_iota(I32, (bq, 2 * bq), 1)
    bias_full = jnp.where(jnp.logical_and(col - row >= 0, col - row <= bq), 0.0, -jnp.inf)
    row1 = lax.broadcasted_iota(I32, (bq, bq), 0)
    col1 = lax.broadcasted_iota(I32, (bq, bq), 1)
    bias_first = jnp.where(row1 >= col1, 0.0, -jnp.inf)
    for g, r in enumerate(dilations):
        nb = seq // (r * bq)
        blocks = [(rho, u) for rho in range(r) for u in range(nb)]
        for i, (rho, u) in enumerate(blocks):
            q = q_refs[g][_rows(rho + u * bq * r, bq, r), :].astype(BF16)
            if u == 0:
                k = k_refs[g][_rows(rho, bq, r), :].astype(BF16)
                s_scr[i, :, :bq] = _dot_nt(q, k) * scale + bias_first
            else:
                k = k_refs[g][_rows(rho + (u - 1) * bq * r, 2 * bq, r), :].astype(BF16)
                s_scr[i] = _dot_nt(q, k) * scale + bias_full
        for i, (rho, u) in enumerate(blocks):
            s = s_scr[i, :, :bq] if u == 0 else s_scr[i]
            mx = jnp.max(s, axis=1, keepdims=True)
            e = jnp.exp2(s - mx)
            l = jnp.sum(e, axis=1, keepdims=True)
            p = (e / l).astype(BF16)
            if u == 0:
                p_scr[i, :, :bq] = p
            else:
                p_scr[i] = p
            lse_scr[g, _rows(rho + u * bq * r, bq, r), :] = jnp.broadcast_to(mx + jnp.log2(l), (bq, D))
        for i, (rho, u) in enumerate(blocks):
            if u == 0:
                v = v_refs[g][_rows(rho, bq, r), :].astype(BF16)
                o = _dot(p_scr[i, :, :bq], v)
            else:
                v = v_refs[g][_rows(rho + (u - 1) * bq * r, 2 * bq, r), :].astype(BF16)
                o = _dot(p_scr[i], v)
            o_scr[g, _rows(rho + u * bq * r, bq, r), :] = o
    lses = [lse_scr[g] for g in range(G)]
    mx = functools.reduce(jnp.maximum, lses)
    es = [jnp.exp2(l - mx) for l in lses]
    tot = functools.reduce(lambda a, b: a + b, es)
    acc = (es[0] / tot) * o_scr[0]
    for g in range(1, G):
        acc = acc + (es[g] / tot) * o_scr[g]
    out_ref[...] = acc.astype(out_ref.dtype)


def _attention(q, kv, B, S, heads):
    D = ATTN_HEAD_DIM
    T = B * S
    G = len(DILATED_GROUPS)
    dil = tuple(d for _, d in DILATED_GROUPS)
    assert all(w // d == ATTN_BLOCK and S % (d * ATTN_BLOCK) == 0 for w, d in DILATED_GROUPS)
    nblk = S // ATTN_BLOCK
    spec = lambda col: pl.BlockSpec((S, D), lambda b, h: (b, col(h)))
    in_specs = ([spec(lambda h, g=g: g * heads + h) for g in range(G)]
                + [spec(lambda h, g=g: 2 * g * heads + h) for g in range(G)]
                + [spec(lambda h, g=g: (2 * g + 1) * heads + h) for g in range(G)])
    return pl.pallas_call(
        functools.partial(_attn_kernel, dilations=dil, seq=S),
        out_shape=jax.ShapeDtypeStruct((T, heads * D), BF16),
        grid=(B, heads),
        in_specs=in_specs,
        out_specs=pl.BlockSpec((S, D), lambda b, h: (b, h)),
        scratch_shapes=[pltpu.VMEM((G, S, D), F32), pltpu.VMEM((G, S, D), F32),
                        pltpu.VMEM((nblk, ATTN_BLOCK, 2 * ATTN_BLOCK), F32),
                        pltpu.VMEM((nblk, ATTN_BLOCK, 2 * ATTN_BLOCK), BF16)],
        compiler_params=_params("parallel", "parallel"),
        name="dilated_attention",
    )(*([q] * G), *([kv] * (2 * G)))


def _kv_rows_kernel(x_ref, o_ref, *, heads):
    o_ref[0] = x_ref[...].reshape(x_ref.shape[0], 2, heads, ATTN_HEAD_DIM)


def _kv_rows(kv, g, keep, B, S, heads):
    W2 = 2 * heads * ATTN_HEAD_DIM
    tm = min(keep, 512)
    first = (S - keep) // tm
    per_seq = S // tm
    return pl.pallas_call(
        functools.partial(_kv_rows_kernel, heads=heads),
        out_shape=jax.ShapeDtypeStruct((B, keep, 2, heads, ATTN_HEAD_DIM), F32),
        grid=(B, keep // tm),
        in_specs=[pl.BlockSpec((tm, W2), lambda b, i: (b * per_seq + first + i, g))],
        out_specs=pl.BlockSpec((1, tm, 2, heads, ATTN_HEAD_DIM), lambda b, i: (b, i, 0, 0, 0)),
        compiler_params=_params("parallel", "parallel"),
        name=f"kv_rows_g{g}",
    )(kv)


def _decode_attn_kernel(q_ref, kvn_ref, *refs):
    G = len(refs) - 1
    buf_refs = refs[:G]
    out_ref = refs[G]
    scale = ATTN_HEAD_DIM ** -0.5
    outs, lses = [], []
    for g in range(G):
        q = q_ref[g]
        k_new = kvn_ref[g, 0]
        v_new = kvn_ref[g, 1]
        k_buf = buf_refs[g][:, 0]
        v_buf = buf_refs[g][:, 1]
        s_buf = jnp.sum(k_buf * q[None], axis=-1, keepdims=True) * scale
        s_new = jnp.sum(k_new * q, axis=-1, keepdims=True) * scale
        mx = jnp.maximum(jnp.max(s_buf, axis=0), s_new)
        e_buf = jnp.exp(s_buf - mx[None])
        e_new = jnp.exp(s_new - mx)
        l = jnp.sum(e_buf, axis=0) + e_new
        outs.append(jnp.sum((e_buf / l[None]) * v_buf, axis=0) + (e_new / l) * v_new)
        lses.append(mx + jnp.log(l))
    mx = functools.reduce(jnp.maximum, lses)
    es = [jnp.exp(l - mx) for l in lses]
    tot = functools.reduce(lambda a, b: a + b, es)
    acc = (es[0] / tot) * outs[0]
    for g in range(1, G):
        acc = acc + (es[g] / tot) * outs[g]
    out_ref[...] = acc


def _decode_attention(q, kv_new, caches):
    Bd, G, H, D = q.shape
    steps = ATTN_BLOCK
    views = []
    for cache, (window, dilation) in zip(caches, DILATED_GROUPS):
        Lb = cache.shape[1]
        assert window // dilation == steps and Lb == steps * dilation
        views.append(cache.reshape(Bd, steps, dilation, 2, H, D))
    return pl.pallas_call(
        _decode_attn_kernel,
        out_shape=jax.ShapeDtypeStruct((Bd, H, D), F32),
        grid=(Bd,),
        in_specs=[pl.BlockSpec((None, G, H, D), lambda b: (b, 0, 0, 0)),
                  pl.BlockSpec((None, G, 2, H, D), lambda b: (b, 0, 0, 0, 0))]
                 + [pl.BlockSpec((None, steps, None, 2, H, D), lambda b: (b, 0, 0, 0, 0, 0)) for _ in views],
        out_specs=pl.BlockSpec((None, H, D), lambda b: (b, 0, 0)),
        compiler_params=_params("parallel"),
        name="decode_attention",
    )(q, kv_new, *views)


def _router_kernel(h_ref, g_ref, wt_ref, b_ref, xn_ref, o_ref):
    xn = _rms(h_ref[...]) * g_ref[...]
    xn_ref[...] = xn
    logits = _dot_nt_3pass(xn, wt_ref[...]) + b_ref[...]
    e = jnp.exp(logits - jnp.max(logits, axis=1, keepdims=True))
    p = e / jnp.sum(e, axis=1, keepdims=True)
    E = p.shape[1]
    idx = lax.broadcasted_iota(I32, p.shape, 1)
    p1 = jnp.max(p, axis=1, keepdims=True)
    i1 = jnp.min(jnp.where(p == p1, idx, E), axis=1, keepdims=True)
    rest = jnp.where(idx == i1, -1.0, p)
    p2 = jnp.max(rest, axis=1, keepdims=True)
    i2 = jnp.min(jnp.where(rest == p2, idx, E), axis=1, keepdims=True)
    tot = p1 + p2
    o_ref[...] = jnp.where(idx == 0, i1.astype(F32),
                 jnp.where(idx == 1, i2.astype(F32),
                 jnp.where(idx == 2, p1 / tot,
                 jnp.where(idx == 3, p2 / tot, 0.0))))


def _router(h, gain, w_router_t, b_router):
    T, D = h.shape
    E = w_router_t.shape[0]
    tm = _row_tile(T, 512)
    return pl.pallas_call(
        _router_kernel,
        out_shape=[jax.ShapeDtypeStruct((T, D), F32), jax.ShapeDtypeStruct((T, E), F32)],
        grid=(T // tm,),
        in_specs=[pl.BlockSpec((tm, D), lambda i: (i, 0)),
                  pl.BlockSpec((1, D), lambda i: (0, 0)),
                  pl.BlockSpec((E, D), lambda i: (0, 0)),
                  pl.BlockSpec((1, E), lambda i: (0, 0))],
        out_specs=[pl.BlockSpec((tm, D), lambda i: (i, 0)), pl.BlockSpec((tm, E), lambda i: (i, 0))],
        compiler_params=_params("parallel"),
        name="router",
    )(h, gain, w_router_t, b_router.reshape(1, E))


def _route_plan(top_i, tm, n_tiles):
    T = top_i.shape[0]
    e_flat = top_i.reshape(-1)
    onehot = (e_flat[:, None] == jnp.arange(N_EXPERTS, dtype=I32)[None, :]).astype(I32)
    csum = jnp.cumsum(onehot, axis=0)
    rank = jnp.sum(onehot * csum, axis=1) - 1
    counts = csum[-1]
    gsz = ((counts + tm - 1) // tm) * tm
    gend = jnp.cumsum(gsz)
    gstart = gend - gsz
    pos = (jnp.sum(onehot * gstart[None, :], axis=1) + rank).astype(I32)
    src = jnp.zeros((n_tiles * tm,), I32).at[pos].set(jnp.arange(TOP_K * T, dtype=I32) // TOP_K)
    tile_start = jnp.arange(n_tiles, dtype=I32) * tm
    tile_valid = (tile_start < gend[-1]).astype(I32)
    tile_e = jnp.sum((tile_start[:, None] >= gend[None, :]).astype(I32), axis=1)
    last_e = jnp.sum((gend[-1] - 1 >= gend).astype(I32))
    tile_e = jnp.where(tile_valid > 0, tile_e, last_e).astype(I32)
    return pos, src, tile_e, tile_valid


def _scatter_plan(top_i, tm, n_tiles):
    T = top_i.shape[0]
    R = n_tiles * tm
    e_flat = top_i.reshape(-1)
    onehot = (e_flat[:, None] == jnp.arange(N_EXPERTS, dtype=I32)[None, :]).astype(I32)
    csum = jnp.cumsum(onehot, axis=0)
    rank = jnp.sum(onehot * csum, axis=1) - 1
    counts = csum[-1]
    gsz = ((counts + tm - 1) // tm) * tm
    gend = jnp.cumsum(gsz)
    gstart = gend - gsz
    pos = (jnp.sum(onehot * gstart[None, :], axis=1) + rank).astype(I32)
    a = jnp.arange(TOP_K * T, dtype=I32)
    dst_assigned = jnp.zeros((R,), I32).at[pos].set((a % TOP_K) * T + a // TOP_K)
    r = jnp.arange(R, dtype=I32)
    e_row = jnp.minimum(jnp.sum((r[:, None] >= gend[None, :]).astype(I32), axis=1), N_EXPERTS - 1)
    row_onehot = (e_row[:, None] == jnp.arange(N_EXPERTS, dtype=I32)[None, :]).astype(I32)
    in_group = r - jnp.sum(row_onehot * gstart[None, :], axis=1)
    assigned = jnp.logical_and(r < gend[-1], in_group < jnp.sum(row_onehot * counts[None, :], axis=1))
    pad_rank = jnp.cumsum(jnp.logical_not(assigned).astype(I32)) - 1
    dst = jnp.where(assigned, dst_assigned, TOP_K * T + pad_rank).astype(I32)
    tile_start = jnp.arange(n_tiles, dtype=I32) * tm
    tile_valid = (tile_start < gend[-1]).astype(I32)
    tile_e = jnp.sum((tile_start[:, None] >= gend[None, :]).astype(I32), axis=1)
    last_e = jnp.sum((gend[-1] - 1 >= gend).astype(I32))
    tile_e = jnp.where(tile_valid > 0, tile_e, last_e).astype(I32)
    return dst, tile_e, tile_valid


def _scatter_experts_kernel(te_ref, tv_ref, tok_ref, dst_ref, x_hbm, wg_ref, wu_ref, wd_ref, y_hbm,
                            xf_ref, xb_ref, acc_ref, out_ref, gsem, ssem, zsem, *, tm, per_step):
    i = pl.program_id(0)
    c = pl.program_id(1)
    n_tiles = pl.num_programs(0)
    nf = pl.num_programs(1)
    valid = tv_ref[i] > 0
    prefetch = jnp.logical_and(i + 1 < n_tiles, tv_ref[jnp.minimum(i + 1, n_tiles - 1)] > 0)
    drain = jnp.logical_and(i > 0, tv_ref[jnp.maximum(i - 1, 0)] > 0)
    D = xb_ref.shape[1]
    dn = min(D, 512)

    def gather_copy(tile, r):
        return pltpu.make_async_copy(x_hbm.at[pl.ds(tok_ref[tile * tm + r], 1)], xf_ref.at[pl.ds(r, 1)],
                                     gsem.at[0])

    def scatter_copy(tile, r):
        return pltpu.make_async_copy(out_ref.at[pl.ds(r, 1)], y_hbm.at[pl.ds(dst_ref[tile * tm + r], 1)],
                                     ssem.at[0])

    @pl.when(jnp.logical_and(i == 0, c == 0))
    def _():
        def start(r, carry):
            gather_copy(0, r).start()
            return carry

        lax.fori_loop(0, tm, start, 0)

    @pl.when(c == 0)
    def _():
        acc_ref[...] = jnp.zeros_like(acc_ref)

    @pl.when(jnp.logical_and(jnp.logical_not(valid), c == 0))
    def _():
        fill = pltpu.make_async_copy(acc_ref, y_hbm.at[pl.ds(pl.multiple_of(i * tm, 8), tm)], zsem.at[0])
        fill.start()
        fill.wait()

    @pl.when(jnp.logical_and(valid, c == 0))
    def _():
        for _ in range(tm):
            pltpu.make_async_copy(x_hbm.at[pl.ds(0, 1)], xf_ref.at[pl.ds(0, 1)], gsem.at[0]).wait()
        xb_ref[...] = xf_ref[...].astype(BF16)

    def chunk(compute, with_prefetch, with_drain):
        if with_prefetch:
            for j in range(per_step):
                gather_copy(i + 1, c * per_step + j).start()
        if with_drain:
            for j in range(per_step):
                scatter_copy(i - 1, c * per_step + j).start()
        if compute:
            x = xb_ref[...]
            g = _dot(x, wg_ref[0].astype(BF16))
            u = _dot(x, wu_ref[0].astype(BF16))
            act = (g * _sigmoid(g) * u).astype(BF16)
            for n0 in range(0, D, dn):
                acc_ref[:, n0:n0 + dn] += _dot(act, wd_ref[0, :, n0:n0 + dn].astype(BF16))

    for compute, pf, dr in ((True, True, True), (True, True, False), (True, False, True),
                            (True, False, False), (False, False, True)):
        cond = valid if compute else jnp.logical_not(valid)
        cond = jnp.logical_and(cond, prefetch if pf else jnp.logical_not(prefetch))
        cond = jnp.logical_and(cond, drain if dr else jnp.logical_not(drain))
        pl.when(cond)(functools.partial(chunk, compute, pf, dr))

    @pl.when(jnp.logical_and(drain, c == nf - 1))
    def _():
        for _ in range(tm):
            pltpu.make_async_copy(out_ref.at[pl.ds(0, 1)], y_hbm.at[pl.ds(0, 1)], ssem.at[0]).wait()

    @pl.when(jnp.logical_and(valid, c == nf - 1))
    def _():
        out_ref[...] = acc_ref[...]


def _scatter_experts(xn, plan, w_gate, w_up, w_down, tm, tf):
    dst, tile_e, tile_valid = plan
    T, D = xn.shape
    Fe = w_gate.shape[-1]
    nf = Fe // tf
    n_tiles = tile_e.shape[0]
    per_step = tm // nf
    assert per_step * nf == tm
    tok = jnp.where(dst >= TOP_K * T, 0, dst % T).astype(I32)

    def chunk(i, c, tv):
        return jnp.where(tv[i] > 0, c, nf - 1)

    weight = lambda shape, index: pl.BlockSpec((None, 1) + shape,
                                               lambda i, c, te, tv, tok, dst: (0, te[i]) + index(chunk(i, c, tv)))
    return pl.pallas_call(
        functools.partial(_scatter_experts_kernel, tm=tm, per_step=per_step),
        out_shape=jax.ShapeDtypeStruct((n_tiles * tm, D), F32),
        grid_spec=pltpu.PrefetchScalarGridSpec(
            num_scalar_prefetch=4,
            grid=(n_tiles, nf),
            in_specs=[pl.BlockSpec(memory_space=pl.ANY),
                      weight((D, tf), lambda ch: (0, ch)),
                      weight((D, tf), lambda ch: (0, ch)),
                      weight((tf, D), lambda ch: (ch, 0))],
            out_specs=pl.BlockSpec(memory_space=pl.ANY),
            scratch_shapes=[pltpu.VMEM((tm, D), F32), pltpu.VMEM((tm, D), BF16),
                            pltpu.VMEM((tm, D), F32), pltpu.VMEM((tm, D), F32),
                            pltpu.SemaphoreType.DMA((1,)), pltpu.SemaphoreType.DMA((1,)),
                            pltpu.SemaphoreType.DMA((1,))]),
        compiler_params=_params("arbitrary", "arbitrary"),
        name="experts_scatter",
    )(tile_e, tile_valid, tok, dst, xn, w_gate, w_up, w_down)


def _mix_kernel(h_ref, gate_ref, y0_ref, y1_ref, fg_ref, o_ref):
    hsum = h_ref[...] + (gate_ref[:, 0:1] * y0_ref[...] + gate_ref[:, 1:2] * y1_ref[...])
    o_ref[...] = _rms(hsum) * fg_ref[...]


def _mix(h, gates, y_slots, final_gain):
    T, D = h.shape
    tm = _row_tile(T, 512)
    nt = T // tm
    return pl.pallas_call(
        _mix_kernel,
        out_shape=jax.ShapeDtypeStruct((T, D), F32),
        grid=(nt,),
        in_specs=[pl.BlockSpec((tm, D), lambda i: (i, 0)),
                  pl.BlockSpec((tm, TOP_K), lambda i: (i, 0)),
                  pl.BlockSpec((tm, D), lambda i: (i, 0)),
                  pl.BlockSpec((tm, D), lambda i: (i + nt, 0)),
                  pl.BlockSpec((1, D), lambda i: (0, 0))],
        out_specs=pl.BlockSpec((tm, D), lambda i: (i, 0)),
        compiler_params=_params("parallel"),
        name="moe_mix",
    )(h, gates, y_slots, y_slots, final_gain)


def _experts_kernel(te_ref, tv_ref, src_ref, x_hbm, wg_ref, wu_ref, wd_ref, o_ref, xf_ref, xb_ref, sem,
                    *, tm, per_step, accurate):
    i = pl.program_id(0)
    c = pl.program_id(1)
    n_tiles = pl.num_programs(0)
    n_rows = per_step * pl.num_programs(1)
    valid = tv_ref[i] > 0
    nxt = jnp.minimum(i + 1, n_tiles - 1)
    prefetch = jnp.logical_and(i + 1 < n_tiles, tv_ref[nxt] > 0)
    D = o_ref.shape[1]
    dn = min(D, 512)

    def row_copy(tile, r):
        src = src_ref[tile * tm + jnp.minimum(r, tm - 1)]
        return pltpu.make_async_copy(x_hbm.at[pl.ds(src, 1)], xf_ref.at[pl.ds(r, 1)], sem.at[0])

    @pl.when(jnp.logical_and(i == 0, c == 0))
    def _():
        def start(r, carry):
            row_copy(0, r).start()
            return carry

        lax.fori_loop(0, n_rows, start, 0)

    @pl.when(c == 0)
    def _():
        o_ref[...] = jnp.zeros_like(o_ref)

    @pl.when(jnp.logical_and(valid, c == 0))
    def _():
        def wait(r, carry):
            row_copy(i, r).wait()
            return carry

        lax.fori_loop(0, n_rows, wait, 0)
        xb_ref[...] = xf_ref[pl.ds(0, tm), :].astype(xb_ref.dtype)

    def mm(x, w):
        wh = w.astype(BF16)
        return _dot_3pass(x, w, wh) if accurate else _dot(x, wh)

    def chunk(with_prefetch):
        if with_prefetch:
            for j in range(per_step):
                row_copy(i + 1, c * per_step + j).start()
        x = xb_ref[...]
        g = mm(x, wg_ref[0])
        u = mm(x, wu_ref[0])
        act = (g * _sigmoid(g) * u).astype(xb_ref.dtype)
        for n0 in range(0, D, dn):
            o_ref[:, n0:n0 + dn] += mm(act, wd_ref[0, :, n0:n0 + dn])

    pl.when(jnp.logical_and(valid, prefetch))(functools.partial(chunk, True))
    pl.when(jnp.logical_and(valid, jnp.logical_not(prefetch)))(functools.partial(chunk, False))


def _experts(xn, pos_plan, w_gate, w_up, w_down, tm, tf, accurate=False):
    _, src, tile_e, tile_valid = pos_plan
    T, D = xn.shape
    Fe = w_gate.shape[-1]
    nf = Fe // tf
    n_tiles = tile_e.shape[0]
    per_step = -(-tm // nf)

    def chunk(i, c, tv):
        return jnp.where(tv[i] > 0, c, nf - 1)

    return pl.pallas_call(
        functools.partial(_experts_kernel, tm=tm, per_step=per_step, accurate=accurate),
        out_shape=jax.ShapeDtypeStruct((n_tiles * tm, D), F32),
        grid_spec=pltpu.PrefetchScalarGridSpec(
            num_scalar_prefetch=3,
            grid=(n_tiles, nf),
            in_specs=[pl.BlockSpec(memory_space=pl.ANY),
                      pl.BlockSpec((None, 1, D, tf), lambda i, c, te, tv, src: (0, te[i], 0, chunk(i, c, tv))),
                      pl.BlockSpec((None, 1, D, tf), lambda i, c, te, tv, src: (0, te[i], 0, chunk(i, c, tv))),
                      pl.BlockSpec((None, 1, tf, D), lambda i, c, te, tv, src: (0, te[i], chunk(i, c, tv), 0))],
            out_specs=pl.BlockSpec((tm, D), lambda i, c, te, tv, src: (i, 0)),
            scratch_shapes=[pltpu.VMEM((per_step * nf, D), F32), pltpu.VMEM((tm, D), F32 if accurate else BF16),
                            pltpu.SemaphoreType.DMA((1,))]),
        compiler_params=_params("arbitrary", "arbitrary"),
        name="experts",
    )(tile_e, tile_valid, src, xn, w_gate, w_up, w_down)


def _combine_kernel(pos_ref, h_ref, gate_ref, y_hbm, fg_ref, o_ref, ya_ref, yb_ref, sem, *, tm):
    i = pl.program_id(0)

    def copies(r):
        t = i * tm + r
        return (pltpu.make_async_copy(y_hbm.at[pl.ds(pos_ref[TOP_K * t], 1)],
                                      ya_ref.at[pl.ds(r, 1)], sem.at[0]),
                pltpu.make_async_copy(y_hbm.at[pl.ds(pos_ref[TOP_K * t + 1], 1)],
                                      yb_ref.at[pl.ds(r, 1)], sem.at[1]))

    def start(r, carry):
        a, b = copies(r)
        a.start()
        b.start()
        return carry

    def wait(r, carry):
        a, b = copies(r)
        a.wait()
        b.wait()
        return carry

    lax.fori_loop(0, tm, start, 0)
    lax.fori_loop(0, tm, wait, 0)
    hsum = h_ref[...] + (gate_ref[:, 0:1] * ya_ref[...] + gate_ref[:, 1:2] * yb_ref[...])
    o_ref[...] = _rms(hsum) * fg_ref[...]


def _combine(h, gates, y_sorted, pos, final_gain):
    T, D = h.shape
    tm = _row_tile(T, 256)
    return pl.pallas_call(
        functools.partial(_combine_kernel, tm=tm),
        out_shape=jax.ShapeDtypeStruct((T, D), F32),
        grid_spec=pltpu.PrefetchScalarGridSpec(
            num_scalar_prefetch=1,
            grid=(T // tm,),
            in_specs=[pl.BlockSpec((tm, D), lambda i, pos: (i, 0)),
                      pl.BlockSpec((tm, TOP_K), lambda i, pos: (i, 0)),
                      pl.BlockSpec(memory_space=pl.ANY),
                      pl.BlockSpec((1, D), lambda i, pos: (0, 0))],
            out_specs=pl.BlockSpec((tm, D), lambda i, pos: (i, 0)),
            scratch_shapes=[pltpu.VMEM((tm, D), F32), pltpu.VMEM((tm, D), F32),
                            pltpu.SemaphoreType.DMA((2,))]),
        compiler_params=_params("arbitrary"),
        name="moe_combine",
    )(pos, h, gates, y_sorted, final_gain)


def _forward(xp, xs, state, kv_caches, B, S, Bd, p):
    D = xp.shape[1]
    hv = p["mlstm_out_norm"].shape[-1]
    hk = hv // 2
    heads = p["attn_w_out"].shape[1] // ATTN_HEAD_DIM
    W = heads * ATTN_HEAD_DIM
    n_groups = len(DILATED_GROUPS)
    pad_rows = lambda a: jnp.zeros((SAMPLE_ROWS, a.shape[1]), F32).at[:Bd].set(a)

    gain0 = p["norm_mix"][0:1]
    w_gate_t = p["mlstm_w_in"][0, :, 2 * hk + 2 * hv:].T
    xn, gc, gr = _gates(xp, gain0, w_gate_t, p["mlstm_b_gates"][0], BF16, True)
    xn_s, gc_s = _gates(xs, gain0, w_gate_t, p["mlstm_b_gates"][0], F32, False)
    proj, proj_s = _matmul(xn, xn_s, p["mlstm_w_in"], 2 * hk + 2 * hv, F32)
    hg, C_p, n_p, m_p = _mlstm_prompt(proj, gc, gr, p["mlstm_out_norm"], B, S)
    hg_s, C_s, n_s, m_s = _mlstm_decode(proj_s, gc_s, p["mlstm_out_norm"], *state)
    h1, h1_s = _matmul(hg, pad_rows(hg_s[:, 0]), p["mlstm_w_out"], D, F32, res=xp, res_tail=xs)

    (xn,) = _norm(h1, p["norm_ffn"][0:1], [BF16])
    (xn_s,) = _norm(h1_s, p["norm_ffn"][0:1], [F32])
    act, act_s = _glu(xn, xn_s, p["ffn_w_gate"], p["ffn_w_up"])
    h2, h2_s = _matmul(act, act_s, p["ffn_w_down"], D, F32, res=h1, res_tail=h1_s, tn=512)

    kvq_gains = jnp.concatenate([p["kv_norm"][None], p["norm_mix"][1:2]])
    xkv, xq = _norm(h2, kvq_gains, [BF16, BF16])
    xkv_s, xq_s = _norm(h2_s, kvq_gains, [F32, F32])
    kv, kv_s = _matmul(xkv, xkv_s, p["w_kv"], n_groups * 2 * W, F32)
    q, q_s = _matmul(xq, xq_s, p["attn_w_q"], n_groups * W, F32)
    att = _attention(q, kv, B, S, heads)
    kv_rows_p = [_kv_rows(kv, g, min(window, S), B, S, heads)
                 for g, (window, _) in enumerate(DILATED_GROUPS)]
    kv5 = kv_s[:Bd].reshape(Bd, n_groups, 2, heads, ATTN_HEAD_DIM)
    att_s = _decode_attention(q_s[:Bd].reshape(Bd, n_groups, heads, ATTN_HEAD_DIM), kv5, kv_caches)
    kv_rows_s = [kv5[:, None, g] for g in range(n_groups)]
    h3, h3_s = _matmul(att, pad_rows(att_s.reshape(Bd, W)), p["attn_w_out"], D, F32, res=h2, res_tail=h2_s)

    y_p = _moe_and_final_norm(h3, p, PROMPT_EXPERT_ROWS, False)
    y_s = _moe_and_final_norm(h3_s, p, SAMPLE_ROWS, True, n_real=Bd)

    states_p = (C_p[None], n_p[:, :, 0][None], m_p[:, :, 0, 0][None])
    states_s = (C_s[None], n_s[:, :, 0][None], m_s[:, :, 0, 0][None])
    return y_p, y_s[:Bd], states_p, kv_rows_p, states_s, kv_rows_s


def _moe_and_final_norm(h, p, tm_e, accurate, n_real=None):
    T = h.shape[0]
    xn, route = _router(h, p["norm_ffn"][1:2], p["moe_w_router"][0].T, p["moe_b_router"][0])
    top_i = route[:, :TOP_K].astype(I32)
    if n_real is not None:
        top_i = jnp.where(jnp.arange(T, dtype=I32)[:, None] < n_real, top_i, top_i[0:1])
    gates = route[:, TOP_K:2 * TOP_K]
    n_tiles = -(-(TOP_K * T) // tm_e) + N_EXPERTS
    weights = (p["moe_w_gate"], p["moe_w_up"], p["moe_w_down"])
    if accurate:
        plan = _route_plan(top_i, tm_e, n_tiles)
        y_sorted = _experts(xn, plan, *weights, tm_e, EXPERT_CHUNK, True)
        return _combine(h, gates, y_sorted, plan[0], p["final_norm"][None])
    y_slots = _scatter_experts(xn, _scatter_plan(top_i, tm_e, n_tiles), *weights, tm_e, EXPERT_CHUNK)
    return _mix(h, gates, y_slots, p["final_norm"][None])


def kernel(x_prompt, x_sample, state_mlstm_C, state_mlstm_n, state_mlstm_m, cache_kv_w128, cache_kv_w512, cache_kv_w2048, norm_mix, norm_ffn, mlstm_w_in, mlstm_b_gates, mlstm_out_norm, mlstm_w_out, kv_norm, w_kv, attn_w_q, attn_w_out, ffn_w_gate, ffn_w_up, ffn_w_down, moe_w_router, moe_b_router, moe_w_gate, moe_w_up, moe_w_down, final_norm):
    assert norm_mix.shape[0] == 2 and mlstm_w_in.shape[0] == 1 and attn_w_q.shape[0] == 1
    p = dict(norm_mix=norm_mix, norm_ffn=norm_ffn, mlstm_w_in=mlstm_w_in, mlstm_b_gates=mlstm_b_gates,
             mlstm_out_norm=mlstm_out_norm, mlstm_w_out=mlstm_w_out, kv_norm=kv_norm, w_kv=w_kv,
             attn_w_q=attn_w_q, attn_w_out=attn_w_out, ffn_w_gate=ffn_w_gate, ffn_w_up=ffn_w_up,
             ffn_w_down=ffn_w_down, moe_w_router=moe_w_router, moe_b_router=moe_b_router,
             moe_w_gate=moe_w_gate, moe_w_up=moe_w_up, moe_w_down=moe_w_down, final_norm=final_norm)
    B, S, D = x_prompt.shape
    Bd = x_sample.shape[0]
    assert x_sample.shape[1] == 1 and Bd <= SAMPLE_ROWS

    xs = jnp.zeros((SAMPLE_ROWS, D), F32).at[:Bd].set(x_sample[:, 0])
    state = (state_mlstm_C[0], state_mlstm_n[0], state_mlstm_m[0])
    y_p, y_s, states_p, kv_rows_p, states_s, kv_rows_s = _forward(
        x_prompt.reshape(B * S, D), xs, state, (cache_kv_w128, cache_kv_w512, cache_kv_w2048), B, S, Bd, p)
    return (y_p.reshape(B, S, D), y_s[:, None], *states_p, *kv_rows_p, *states_s, *kv_rows_s)
```
